```python
import math
import jax, jax.numpy as jnp
from jax import lax
import numpy as np

D_MODEL = 1024
BATCH = 2
SEQ = 8192
DEPTH = 2
DEC_BATCH = 128
DEC_SEQ = 8
PAST_LEN = 2048
PAGE_SIZE = 128

DH = 64
H_A = D_MODEL // 128
W_A = H_A * DH
MOBA_BLOCK = 256
MOBA_TOPK = 3
H_B = D_MODEL // 256
DV_B = 2 * DH
QK_B = H_B * 2 * DH
W_B = H_B * DV_B
D_IN = 3 * W_A + 2 * QK_B + W_B
N_BUCKETS = 32
MAX_DISTANCE = 128
D_FF = 11 * D_MODEL // 4
D_UP = 2 * D_FF
CONV_W = 3
D_PLE = 256
Q_BLOCK = 128
MOBA_ROWS = 128
ALPHA = (2 * DEPTH) ** 0.25
BETA = (8 * DEPTH) ** -0.25
LN_EPS = 1e-5
RMS_EPS = 1e-5
NEG_INF = -1e30
F32 = jnp.float32

kernel_name = 'hybrid_moba_diffattn_convffn_step'


def _chunk(n, target):
    c = max(1, min(n, target))
    while n % c:
        c -= 1
    return c


def _t5_bucket(dist):
    dist = jnp.maximum(dist, 0)
    exact = N_BUCKETS // 2
    logv = jnp.log(jnp.maximum(dist, 1).astype(F32) / exact) / math.log(MAX_DISTANCE / exact)
    large = jnp.minimum(exact + (logv * (N_BUCKETS - exact)).astype(jnp.int32), N_BUCKETS - 1)
    return jnp.where(dist < exact, dist, large)


def _layernorm(x, g, b):
    xf = x.astype(F32)
    mu = jnp.mean(xf, -1, keepdims=True)
    var = jnp.mean(jnp.square(xf - mu), -1, keepdims=True)
    return ((xf - mu) * lax.rsqrt(var + LN_EPS) * g + b).astype(x.dtype)


def _moba_attention(q, k, v, q_pos, tab):
    B, Q, H, dh = q.shape
    L = k.shape[1]
    nb = -(-L // MOBA_BLOCK)
    pad = nb * MOBA_BLOCK - L
    kp = jnp.pad(k, ((0, 0), (0, pad), (0, 0), (0, 0))).reshape(B, nb, MOBA_BLOCK, H, dh)
    vp = jnp.pad(v, ((0, 0), (0, pad), (0, 0), (0, 0))).reshape(B, nb, MOBA_BLOCK, H, dh)
    kmean = jnp.mean(kp, axis=2, dtype=F32)
    k_top = min(MOBA_TOPK, nb)
    n_sel = k_top + 1
    scale = dh ** -0.5
    bi = jnp.arange(B)[:, None, None, None]
    hi = jnp.arange(H)[None, None, :, None]
    offs = jnp.arange(MOBA_BLOCK, dtype=jnp.int32)

    def chunk(args):
        qc, pc = args
        c = pc.shape[0]
        cur = pc // MOBA_BLOCK
        gate = jnp.einsum('bqhd,bnhd->bqhn', qc.astype(F32), kmean)
        eligible = jnp.arange(nb)[None, :] < cur[:, None]
        gate = jnp.where(eligible[None, :, None, :], gate, NEG_INF)
        _, top = lax.top_k(gate, k_top)
        own = jnp.broadcast_to(cur[None, :, None, None], top.shape[:3] + (1,))
        sel = jnp.concatenate([top.astype(jnp.int32), own.astype(jnp.int32)], -1)
        sel_ok = jnp.concatenate([jnp.arange(k_top)[None, :] < cur[:, None],
                                  jnp.ones((c, 1), bool)], -1)
        kg = kp[bi, sel, :, hi]
        vg = vp[bi, sel, :, hi]
        kpos = sel[..., None] * MOBA_BLOCK + offs
        dist = pc[None, :, None, None, None] - kpos
        bias = tab[jnp.arange(H)[None, None, :, None, None], _t5_bucket(dist)]
        s = jnp.einsum('bqhd,bqhnkd->bqhnk', qc, kg).astype(F32) * scale + bias
        mask = (dist >= 0) & sel_ok[None, :, None, :, None]
        s = jnp.where(mask, s, NEG_INF).reshape(B, c, H, n_sel * MOBA_BLOCK)
        a = jax.nn.softmax(s, axis=-1).reshape(B, c, H, n_sel, MOBA_BLOCK).astype(v.dtype)
        return jnp.einsum('bqhnk,bqhnkd->bqhd', a, vg)

    c = _chunk(Q, max(1, MOBA_ROWS // B))
    nq = Q // c
    qs = q.reshape(B, nq, c, H, dh).transpose(1, 0, 2, 3, 4)
    out = lax.map(chunk, (qs, q_pos.reshape(nq, c)))
    return out.transpose(1, 0, 2, 3, 4).reshape(B, Q, H * dh)


def _diff_attention(q1, q2, k1, k2, v, q_pos, tab, lam):
    B, Q, H, dh = q1.shape
    L = k1.shape[1]
    kpos = jnp.arange(L, dtype=jnp.int32)
    scale = dh ** -0.5

    def block(args):
        a1, a2, pc = args
        dist = pc[:, None] - kpos[None, :]
        bias = tab[:, _t5_bucket(dist)]
        mask = dist >= 0

        def probs(qq, kk):
            s = jnp.einsum('bqhd,bkhd->bhqk', qq, kk).astype(F32) * scale + bias
            return jax.nn.softmax(jnp.where(mask, s, NEG_INF), axis=-1)

        a = probs(a1, k1) - lam * probs(a2, k2)
        return jnp.einsum('bhqk,bkhd->bqhd', a.astype(v.dtype), v)

    c = _chunk(Q, Q_BLOCK)
    nq = Q // c
    sw = lambda t: t.reshape(B, nq, c, H, dh).transpose(1, 0, 2, 3, 4)
    out = lax.map(block, (sw(q1), sw(q2), q_pos.reshape(nq, c)))
    return out.transpose(1, 0, 2, 3, 4).reshape(B, Q, H, v.shape[-1])


def _gather_pages(pool_l, page_table):
    g = pool_l[page_table]
    return g.reshape((page_table.shape[0], -1) + pool_l.shape[2:])


def _layer(l, x, p_l, past, conv_prev, pos0, w):
    B, S, _ = x.shape
    q_pos = pos0 + jnp.arange(S, dtype=jnp.int32)
    h = x @ w['w_in'][l]
    qa, ka, va, qb, kb, vb = jnp.split(h, np.cumsum([W_A, W_A, W_A, QK_B, QK_B]).tolist(), axis=-1)
    qa = qa.reshape(B, S, H_A, DH)
    ka = ka.reshape(B, S, H_A, DH)
    va = va.reshape(B, S, H_A, DH)
    qb = qb.reshape(B, S, H_B, 2 * DH)
    kb = kb.reshape(B, S, H_B, 2 * DH)
    vb = vb.reshape(B, S, H_B, DV_B)
    if past is None:
        k_a, v_a, k_b, v_b = ka, va, kb, vb
    else:
        k_a, v_a, k_b, v_b = (jnp.concatenate([c.astype(n.dtype), n], axis=1)
                              for c, n in zip(past, (ka, va, kb, vb)))
    tab = w['rel_bias']
    y_a = _moba_attention(qa, k_a, v_a, q_pos, tab[:, :H_A].T)
    lam_init = 0.8 - 0.6 * math.exp(-0.3 * l)
    lam = (jnp.exp(jnp.sum(w['diff_lq1'][l].astype(F32) * w['diff_lk1'][l].astype(F32)))
           - jnp.exp(jnp.sum(w['diff_lq2'][l].astype(F32) * w['diff_lk2'][l].astype(F32))) + lam_init)
    o_b = _diff_attention(qb[..., :DH], qb[..., DH:], k_b[..., :DH], k_b[..., DH:], v_b,
                          q_pos, tab[:, H_A:].T, lam)
    of = o_b.astype(F32)
    y_b = (of * lax.rsqrt(jnp.mean(of * of, -1, keepdims=True) + RMS_EPS)
           * w['diff_subln_g'][l] * (1.0 - lam_init)).astype(x.dtype).reshape(B, S, W_B)
    g_a, g_b = jnp.split(jax.nn.sigmoid(x @ w['w_gate'][l] + w['b_gate'][l]), 2, axis=-1)
    mix = (g_a * (y_a @ w['w_branch_a'][l]) + g_b * (y_b @ w['w_branch_b'][l])) @ w['w_out'][l]
    x = _layernorm(ALPHA * x + mix, w['ln1_g'][l], w['ln1_b'][l])
    u = x @ w['w_up'][l]
    full = jnp.concatenate([conv_prev.astype(u.dtype), u], axis=1)
    cw = w['conv_w'][l]
    conv = sum(cw[j] * full[:, j:j + S] for j in range(CONV_W)) + w['conv_b'][l]
    c_gate, c_val = jnp.split(conv, 2, axis=-1)
    f = (jax.nn.gelu(c_gate) * c_val) @ w['w_down'][l]
    x = _layernorm(ALPHA * x + f, w['ln2_g'][l], w['ln2_b'][l])
    x = x + jax.nn.sigmoid(x @ w['w_ple_gate'][l]) * (p_l @ w['w_ple'][l])
    return x, (ka, va, kb, vb, full[:, -(CONV_W - 1):])


def setup_inputs(seed: int = 0) -> dict:
    key = jax.random.key(seed)
    ks = iter(jax.random.split(key, 48))
    nrm = lambda shape, scale: jax.random.normal(next(ks), shape, F32) * scale
    n_pages = PAST_LEN // PAGE_SIZE
    n_used = DEC_BATCH * n_pages
    n_phys = (n_used * 5) // 4
    page_table = jax.random.permutation(next(ks), n_phys)[:n_used].reshape(DEC_BATCH, n_pages).astype(jnp.int32)
    col_scale = jnp.concatenate([jnp.full((2 * W_A,), 1.0, F32), jnp.full((W_A,), BETA, F32),
                                 jnp.full((2 * QK_B,), 1.0, F32), jnp.full((W_B,), BETA, F32)]) * D_MODEL ** -0.5
    return {
        'x_prompt': nrm((BATCH, SEQ, D_MODEL), 1.0),
        'x_sample': nrm((DEC_BATCH, DEC_SEQ, D_MODEL), 1.0),
        'cache_a_k': nrm((DEPTH, n_phys, PAGE_SIZE, H_A, DH), 1.0),
        'cache_a_v': nrm((DEPTH, n_phys, PAGE_SIZE, H_A, DH), BETA),
        'cache_b_k': nrm((DEPTH, n_phys, PAGE_SIZE, H_B, 2 * DH), 1.0),
        'cache_b_v': nrm((DEPTH, n_phys, PAGE_SIZE, H_B, DV_B), BETA),
        'state_conv': nrm((DEPTH, DEC_BATCH, CONV_W - 1, D_UP), BETA),
        'page_table': page_table,
        'p_prompt': nrm((DEPTH, BATCH, SEQ, D_PLE), 1.0),
        'p_sample': nrm((DEPTH, DEC_BATCH, DEC_SEQ, D_PLE), 1.0),
        'w_in': nrm((DEPTH, D_MODEL, D_IN), 1.0) * col_scale,
        'w_gate': nrm((DEPTH, D_MODEL, 2 * D_MODEL), D_MODEL ** -0.5),
        'b_gate': nrm((DEPTH, 2 * D_MODEL), 0.02),
        'w_branch_a': nrm((DEPTH, W_A, D_MODEL), BETA * W_A ** -0.5),
        'w_branch_b': nrm((DEPTH, W_B, D_MODEL), BETA * W_B ** -0.5),
        'w_out': nrm((DEPTH, D_MODEL, D_MODEL), BETA * D_MODEL ** -0.5),
        'rel_bias': nrm((N_BUCKETS, H_A + H_B), 0.5),
        'diff_lq1': nrm((DEPTH, DH), 0.1),
        'diff_lk1': nrm((DEPTH, DH), 0.1),
        'diff_lq2': nrm((DEPTH, DH), 0.1),
        'diff_lk2': nrm((DEPTH, DH), 0.1),
        'diff_subln_g': 1.0 + nrm((DEPTH, DV_B), 0.05),
        'ln1_g': 1.0 + nrm((DEPTH, D_MODEL), 0.05),
        'ln1_b': nrm((DEPTH, D_MODEL), 0.02),
        'w_up': nrm((DEPTH, D_MODEL, D_UP), BETA * D_MODEL ** -0.5),
        'conv_w': nrm((DEPTH, CONV_W, D_UP), CONV_W ** -0.5),
        'conv_b': nrm((DEPTH, D_UP), 0.02),
        'w_down': nrm((DEPTH, D_FF, D_MODEL), BETA * D_FF ** -0.5),
        'ln2_g': 1.0 + nrm((DEPTH, D_MODEL), 0.05),
        'ln2_b': nrm((DEPTH, D_MODEL), 0.02),
        'w_ple': nrm((DEPTH, D_PLE, D_MODEL), D_PLE ** -0.5),
        'w_ple_gate': nrm((DEPTH, D_MODEL, D_MODEL), D_MODEL ** -0.5),
    }


def reference(x_prompt, x_sample, cache_a_k, cache_a_v, cache_b_k, cache_b_v, state_conv, page_table,
              p_prompt, p_sample, w_in, w_gate, b_gate, w_branch_a, w_branch_b, w_out, rel_bias,
              diff_lq1, diff_lk1, diff_lq2, diff_lk2, diff_subln_g, ln1_g, ln1_b, w_up, conv_w, conv_b,
              w_down, ln2_g, ln2_b, w_ple, w_ple_gate):
    w = {'w_in': w_in, 'w_gate': w_gate, 'b_gate': b_gate, 'w_branch_a': w_branch_a,
         'w_branch_b': w_branch_b, 'w_out': w_out, 'rel_bias': rel_bias, 'diff_lq1': diff_lq1,
         'diff_lk1': diff_lk1, 'diff_lq2': diff_lq2, 'diff_lk2': diff_lk2, 'diff_subln_g': diff_subln_g,
         'ln1_g': ln1_g, 'ln1_b': ln1_b, 'w_up': w_up, 'conv_w': conv_w, 'conv_b': conv_b,
         'w_down': w_down, 'ln2_g': ln2_g, 'ln2_b': ln2_b, 'w_ple': w_ple, 'w_ple_gate': w_ple_gate}
    past_len = page_table.shape[1] * cache_a_k.shape[2]
    conv0 = jnp.zeros((x_prompt.shape[0], CONV_W - 1, D_UP), x_prompt.dtype)
    y_p, y_s = x_prompt, x_sample
    pr, sa = [], []
    for l in range(DEPTH):
        y_p, rows = _layer(l, y_p, p_prompt[l], None, conv0, 0, w)
        pr.append(rows)
        past = tuple(_gather_pages(c[l], page_table) for c in (cache_a_k, cache_a_v, cache_b_k, cache_b_v))
        y_s, rows = _layer(l, y_s, p_sample[l], past, state_conv[l], past_len, w)
        sa.append(rows)
    st = lambda rows, i: jnp.stack([r[i] for r in rows])
    return (y_p, y_s, st(pr, 0), st(pr, 1), st(pr, 2), st(pr, 3), st(pr, 4),
            st(sa, 0), st(sa, 1), st(sa, 2), st(sa, 3), st(sa, 4))
```

```python
import functools
import math

import numpy as np
import jax
import jax.numpy as jnp
from jax import lax
from jax.experimental import pallas as pl
from jax.experimental.pallas import tpu as pltpu

F32 = jnp.float32
BF16 = jnp.bfloat16

DH = 64
MOBA_BLOCK = 256
MOBA_TOPK = 3
N_BUCKETS = 32
MAX_DISTANCE = 128
CONV_W = 3
LN_EPS = 1e-5
RMS_EPS = 1e-5
NEG_INF = -1e30
LANES = 128
VMEM_LIMIT = 56 * 1024 * 1024

ATT_TILE = MOBA_BLOCK
FFN_CHUNK = 256


def _cparams(sem):
    return pltpu.CompilerParams(dimension_semantics=sem, vmem_limit_bytes=VMEM_LIMIT)


def _dot(a, b):
    return jnp.dot(a, b, preferred_element_type=F32)


def _dot_nt(a, b):
    return lax.dot_general(a, b, (((1,), (1,)), ((), ())), preferred_element_type=F32)


def _resident(shape):
    return pl.BlockSpec(shape, lambda *_: (0,) * len(shape), pipeline_mode=pl.Buffered(1))


def _t5_bucket_np(dist):
    dist = np.maximum(dist, 0)
    exact = N_BUCKETS // 2
    logv = (np.log(np.maximum(dist, 1).astype(np.float32) / np.float32(exact))
            / np.float32(math.log(MAX_DISTANCE / exact))).astype(np.float32)
    large = np.minimum(exact + (logv * np.float32(N_BUCKETS - exact)).astype(np.int32), N_BUCKETS - 1)
    return np.where(dist < exact, dist, large).astype(np.int32)


def _prompt_bias_tiles(tab):
    t = ATT_TILE
    i = np.arange(t)[:, None]
    j = np.arange(t)[None, :]
    far = tab[:, N_BUCKETS - 1][:, None, None]
    diag = jnp.where((i >= j)[None], tab[:, _t5_bucket_np(i - j)] - far, NEG_INF)
    prev = tab[:, _t5_bucket_np(t + i - j)] - far
    return jnp.stack([diag, prev], axis=1).astype(F32)


def _sample_bias_tiles(tab_a, tab_b, n_new, past_len):
    h_a, h_b = tab_a.shape[0], tab_b.shape[0]
    rows = (h_a + 2 * h_b) * n_new
    r = np.arange(rows)
    qi = (r % n_new)[:, None]
    head = np.where(r < h_a * n_new, r // n_new, h_a + (r - h_a * n_new) // (2 * n_new))
    tab = jnp.concatenate([tab_a, tab_b], axis=0)
    far = tab[head, N_BUCKETS - 1][:, None]
    t = np.arange(MOBA_BLOCK)[None, :]
    dist_last = past_len + qi - (past_len - MOBA_BLOCK + t)
    last = tab[head[:, None], _t5_bucket_np(dist_last)] - far
    j = np.arange(LANES)[None, :]
    new = jnp.where((j <= qi) & (j < n_new), tab[head[:, None], _t5_bucket_np(qi - j)] - far, NEG_INF)
    return last.astype(F32), new.astype(F32)


def _proj_kernel(x_ref, w_ref, ka_ref, va_ref, kb_ref, vb_ref, hq_ref, *rest, seg, q_scale, n_mean):
    x = x_ref[...].astype(BF16)
    outs = (None, ka_ref, va_ref, None, kb_ref, vb_ref)
    for s in range(6):
        h = _dot(x, w_ref[:, s * seg:(s + 1) * seg])
        if outs[s] is not None:
            outs[s][...] = h
        if s == 1 and n_mean:
            km_ref = rest[0]
            for n in range(n_mean):
                blk = h[n * MOBA_BLOCK:(n + 1) * MOBA_BLOCK]
                km_ref[n] = jnp.sum(blk, axis=0, keepdims=True) * (1.0 / MOBA_BLOCK)
        if s in (0, 3):
            h = h * q_scale
        hq_ref[:, s * seg:(s + 1) * seg] = h.astype(hq_ref.dtype)


def _proj(x2, w_in_l, *, tm, hq_dtype, with_kmean):
    m, d = x2.shape
    d_in = w_in_l.shape[1]
    seg = d_in // 6
    n_mean = tm // MOBA_BLOCK if with_kmean else 0
    row = lambda w: pl.BlockSpec((tm, w), lambda i: (i, 0))
    out_shape = [jax.ShapeDtypeStruct((m, seg), F32)] * 4 + [jax.ShapeDtypeStruct((m, d_in), hq_dtype)]
    out_specs = [row(seg)] * 4 + [row(d_in)]
    if with_kmean:
        out_shape.append(jax.ShapeDtypeStruct((m // MOBA_BLOCK, 1, seg), F32))
        out_specs.append(pl.BlockSpec((n_mean, 1, seg), lambda i: (i, 0, 0)))
    return pl.pallas_call(
        functools.partial(_proj_kernel, seg=seg, q_scale=DH ** -0.5, n_mean=n_mean),
        grid=(m // tm,),
        in_specs=[row(d), _resident((d, d_in))],
        out_specs=out_specs,
        out_shape=out_shape,
        compiler_params=_cparams(("parallel",)),
    )(x2, w_in_l)


def _softmax_tile(s):
    m = jnp.max(s, axis=1, keepdims=True)
    p = jnp.exp(s - m)
    return m, p, jnp.sum(p, axis=1, keepdims=True)


def _moba_prompt_kernel(q_ref, k_ref, v_ref, kmean_ref, bias_ref, o_ref, m_sc, l_sc, acc_sc):
    t = ATT_TILE
    qi = pl.program_id(2)
    lane = lax.broadcasted_iota(jnp.int32, (t, LANES), 1)
    head0 = lane < DH
    q = q_ref[...]
    zero = jnp.zeros_like(q)
    q_heads = (jnp.where(head0, q, zero), jnp.where(head0, zero, q))
    kmean = kmean_ref[...].astype(BF16)

    def block_mask(qh):
        gate = jnp.where(lane < qi, _dot_nt(qh, kmean), NEG_INF)
        chosen = lane == qi
        for j in range(MOBA_TOPK):
            best = jnp.max(gate, axis=1, keepdims=True)
            idx = jnp.min(jnp.where(gate == best, lane, 2 * LANES), axis=1, keepdims=True)
            hit = lane == idx
            chosen = jnp.logical_or(chosen, jnp.logical_and(hit, qi > j))
            gate = jnp.where(hit, -3e38, gate)
        return jnp.where(chosen, 0.0, NEG_INF).astype(BF16)

    q_ext = tuple(jnp.concatenate([qh, block_mask(qh)], axis=1) for qh in q_heads)

    def scores(kj, bias_idx):
        rows = pl.ds(pl.multiple_of(kj * t, t), t)
        onehot = jnp.where(lane == kj, 1.0, 0.0).astype(BF16)
        k_ext = jnp.concatenate([k_ref[rows, :], onehot], axis=1)
        out = []
        for h in range(2):
            s = _dot_nt(q_ext[h], k_ext)
            if bias_idx is not None:
                s = s + bias_ref[h, bias_idx]
            out.append(s)
        return out, v_ref[rows, :]

    s_pair, v = scores(qi, 0)
    o_pair = []
    for h in range(2):
        m, p, l = _softmax_tile(s_pair[h])
        m_sc[h] = m
        l_sc[h] = l
        o_pair.append(_dot(p.astype(BF16), v))
    acc_sc[...] = jnp.where(head0, o_pair[0], o_pair[1])

    def update(kj, bias_idx):
        s_pair, v = scores(kj, bias_idx)
        o_pair, alphas = [], []
        for h in range(2):
            m_old = m_sc[h]
            m_new = jnp.maximum(m_old, jnp.max(s_pair[h], axis=1, keepdims=True))
            alpha = jnp.exp(m_old - m_new)
            p = jnp.exp(s_pair[h] - m_new)
            l_sc[h] = alpha * l_sc[h] + jnp.sum(p, axis=1, keepdims=True)
            m_sc[h] = m_new
            o_pair.append(_dot(p.astype(BF16), v))
            alphas.append(alpha)
        acc_sc[...] = (jnp.where(head0, alphas[0], alphas[1]) * acc_sc[...]
                       + jnp.where(head0, o_pair[0], o_pair[1]))

    @pl.when(qi >= 1)
    def _():
        update(qi - 1, 1)

    def far_body(kj, carry):
        update(kj, None)
        return carry

    lax.fori_loop(0, jnp.maximum(qi - 1, 0), far_body, 0)
    o_ref[...] = (acc_sc[...] / jnp.where(head0, l_sc[0], l_sc[1])).astype(o_ref.dtype)


def _moba_prompt(hq, kmean_pad, bias_a, *, batch, seq, h_a):
    t = ATT_TILE
    pairs = h_a * DH // LANES
    seg_blocks = hq.shape[1] // 6 // LANES
    nq = seq // t
    return pl.pallas_call(
        _moba_prompt_kernel,
        grid=(batch, pairs, nq),
        in_specs=[
            pl.BlockSpec((t, LANES), lambda b, j, i: (b * nq + i, j)),
            pl.BlockSpec((seq, LANES), lambda b, j, i: (b, seg_blocks + j)),
            pl.BlockSpec((seq, LANES), lambda b, j, i: (b, 2 * seg_blocks + j)),
            pl.BlockSpec((None, LANES, LANES), lambda b, j, i: (b, 0, j)),
            pl.BlockSpec((2, 2, t, t), lambda b, j, i: (j, 0, 0, 0)),
        ],
        out_specs=pl.BlockSpec((t, LANES), lambda b, j, i: (b * nq + i, j)),
        out_shape=jax.ShapeDtypeStruct((batch * seq, pairs * LANES), BF16),
        scratch_shapes=[pltpu.VMEM((2, t, 1), F32), pltpu.VMEM((2, t, 1), F32), pltpu.VMEM((t, LANES), F32)],
        compiler_params=_cparams(("parallel", "parallel", "arbitrary")),
    )(hq, hq, hq, kmean_pad, bias_a)


def _diff_prompt_kernel(lam_ref, q_ref, k_ref, v_ref, bias_ref, g_ref, o_ref, m_sc, l_sc, acc_sc, *, out_scale):
    t = ATT_TILE
    qi = pl.program_id(2)
    lane = lax.broadcasted_iota(jnp.int32, (t, LANES), 1)
    half0 = lane < DH
    q = q_ref[...]
    zero = jnp.zeros_like(q)
    q_halves = (jnp.where(half0, q, zero), jnp.where(half0, zero, q))

    def scores(kj, bias_idx):
        rows = pl.ds(pl.multiple_of(kj * t, t), t)
        k = k_ref[rows, :]
        out = []
        for h in range(2):
            s = _dot_nt(q_halves[h], k)
            if bias_idx is not None:
                s = s + bias_ref[bias_idx]
            out.append(s)
        return out, v_ref[rows, :]

    s_pair, v = scores(qi, 0)
    for h in range(2):
        m, p, l = _softmax_tile(s_pair[h])
        m_sc[h] = m
        l_sc[h] = l
        acc_sc[h] = _dot(p.astype(BF16), v)

    def update(kj, bias_idx):
        s_pair, v = scores(kj, bias_idx)
        for h in range(2):
            m_old = m_sc[h]
            m_new = jnp.maximum(m_old, jnp.max(s_pair[h], axis=1, keepdims=True))
            alpha = jnp.exp(m_old - m_new)
            p = jnp.exp(s_pair[h] - m_new)
            l_sc[h] = alpha * l_sc[h] + jnp.sum(p, axis=1, keepdims=True)
            m_sc[h] = m_new
            acc_sc[h] = alpha * acc_sc[h] + _dot(p.astype(BF16), v)

    @pl.when(qi >= 1)
    def _():
        update(qi - 1, 1)

    def far_body(kj, carry):
        update(kj, None)
        return carry

    lax.fori_loop(0, jnp.maximum(qi - 1, 0), far_body, 0)
    o = acc_sc[0] / l_sc[0] - lam_ref[0] * (acc_sc[1] / l_sc[1])
    rms = lax.rsqrt(jnp.mean(o * o, axis=1, keepdims=True) + RMS_EPS)
    o_ref[...] = (o * rms * g_ref[...] * out_scale).astype(o_ref.dtype)


def _diff_prompt(lam, hq, bias_b, subln_g, *, batch, seq, h_b, lam_init):
    t = ATT_TILE
    seg_blocks = hq.shape[1] // 6 // LANES
    nq = seq // t
    return pl.pallas_call(
        functools.partial(_diff_prompt_kernel, out_scale=1.0 - lam_init),
        grid=(batch, h_b, nq),
        in_specs=[
            pl.BlockSpec(memory_space=pltpu.SMEM),
            pl.BlockSpec((t, LANES), lambda b, h, i: (b * nq + i, 3 * seg_blocks + h)),
            pl.BlockSpec((seq, LANES), lambda b, h, i: (b, 4 * seg_blocks + h)),
            pl.BlockSpec((seq, LANES), lambda b, h, i: (b, 5 * seg_blocks + h)),
            pl.BlockSpec((None, 2, t, t), lambda b, h, i: (h, 0, 0, 0)),
            _resident((1, LANES)),
        ],
        out_specs=pl.BlockSpec((t, LANES), lambda b, h, i: (b * nq + i, h)),
        out_shape=jax.ShapeDtypeStruct((batch * seq, h_b * LANES), BF16),
        scratch_shapes=[pltpu.VMEM((2, t, 1), F32), pltpu.VMEM((2, t, 1), F32), pltpu.VMEM((2, t, LANES), F32)],
        compiler_params=_cparams(("parallel", "parallel", "arbitrary")),
    )(lam, hq, hq, hq, bias_b, subln_g)


def _sample_attn_kernel(pt_ref, lam_ref, hq_ref, ka_new_ref, va_new_ref, kb_new_ref, vb_new_ref,
                        bias_last_ref, bias_new_ref, g_ref, *refs,
                        n_steps, blocks_per_step, n_new, h_a, h_b, page, out_scale):
    del pt_ref
    n_pages = 2 * blocks_per_step
    page_refs = refs[:4 * n_pages]
    ya_ref, yb_ref = refs[4 * n_pages:4 * n_pages + 2]
    w_sc, gate_sc, m_sc, l_sc, o_sc = refs[4 * n_pages + 2:]
    g = pl.program_id(1)
    w_a = h_a * DH
    rows_a = h_a * n_new
    rows = rows_a + 2 * h_b * n_new
    n_past = n_steps * blocks_per_step
    lane = lax.broadcasted_iota(jnp.int32, (rows, LANES), 1)

    @pl.when(g == 0)
    def _():
        seg = hq_ref.shape[1] // 6
        qcat = jnp.concatenate([hq_ref[:, 0:seg], hq_ref[:, 3 * seg:4 * seg]], axis=1)
        tiled = jnp.concatenate([qcat] * (rows // n_new), axis=0)
        r = lax.broadcasted_iota(jnp.int32, tiled.shape, 0)
        c = lax.broadcasted_iota(jnp.int32, tiled.shape, 1)
        w_sc[...] = jnp.where(c // DH == r // n_new, tiled, 0.0).astype(BF16)
        gate_sc[...] = jnp.zeros_like(gate_sc)
        m_sc[...] = jnp.zeros_like(m_sc)
        l_sc[...] = jnp.zeros_like(l_sc)

    def partial_block(n, s, bias, pv_a, pv_b):
        gsum = jnp.sum(s, axis=1, keepdims=True)
        m, p, l = _softmax_tile(s + bias)
        p = p.astype(BF16)
        o_sc[n, 0:rows_a, :] = pv_a(p[0:rows_a])
        o_sc[n, rows_a:rows, :] = pv_b(p[rows_a:rows])
        here = lane == n
        gate_sc[...] = jnp.where(here, gsum, gate_sc[...])
        m_sc[...] = jnp.where(here, m, m_sc[...])
        l_sc[...] = jnp.where(here, l, l_sc[...])

    def heads_on_lanes(ref):
        return jnp.concatenate([ref[pl.ds(h, page, stride=h_b), :] for h in range(h_b)], axis=1).astype(BF16)

    for u in range(blocks_per_step):
        n = g * blocks_per_step + u
        pages = [page_refs[4 * (2 * u + v):4 * (2 * u + v) + 4] for v in range(2)]
        ka_t = [pg[0][...].reshape(w_a, page).astype(BF16) for pg in pages]
        va_t = [pg[1][...].reshape(w_a, page).astype(BF16) for pg in pages]
        kb = [heads_on_lanes(pg[2]) for pg in pages]
        vb = [heads_on_lanes(pg[3]) for pg in pages]
        q_a = w_sc[0:rows_a, 0:w_a]
        q_b = w_sc[rows_a:rows, w_a:2 * w_a]
        s = jnp.concatenate([
            jnp.concatenate([_dot(q_a, k) for k in ka_t], axis=1),
            jnp.concatenate([_dot_nt(q_b, k) for k in kb], axis=1)], axis=0)
        pv_a = lambda p: sum(_dot_nt(p[:, v * page:(v + 1) * page], va_t[v]) for v in range(2))
        pv_b = lambda p: sum(_dot(p[:, v * page:(v + 1) * page], vb[v]) for v in range(2))
        is_last = (n == n_past - 1).astype(F32)
        partial_block(n, s, bias_last_ref[...] * is_last, pv_a, pv_b)

    @pl.when(g == n_steps - 1)
    def _():
        pad = jnp.zeros((LANES - n_new, w_a), F32)
        padded = lambda ref: jnp.concatenate([ref[...], pad], axis=0).astype(BF16)
        k_cat = jnp.concatenate([padded(ka_new_ref), padded(kb_new_ref)], axis=1)
        va, vb = padded(va_new_ref), padded(vb_new_ref)
        partial_block(n_past, _dot_nt(w_sc[...], k_cat), bias_new_ref[...],
                      lambda p: _dot(p, va), lambda p: _dot(p, vb))

        row = lax.broadcasted_iota(jnp.int32, (rows, LANES), 0)
        gate = jnp.where(lane < n_past, gate_sc[...], NEG_INF)
        chosen = lane == n_past
        for _ in range(min(MOBA_TOPK, n_past)):
            best = jnp.max(gate, axis=1, keepdims=True)
            idx = jnp.min(jnp.where(gate == best, lane, 2 * LANES), axis=1, keepdims=True)
            hit = lane == idx
            chosen = jnp.logical_or(chosen, hit)
            gate = jnp.where(hit, -3e38, gate)
        chosen = jnp.logical_or(chosen, jnp.logical_and(row >= rows_a, lane <= n_past))
        m_fin = jnp.max(jnp.where(chosen, m_sc[...], NEG_INF), axis=1, keepdims=True)
        wgt = jnp.where(chosen, jnp.exp(m_sc[...] - m_fin), 0.0)
        l_fin = jnp.sum(wgt * l_sc[...], axis=1, keepdims=True)
        o = wgt[:, 0:1] * o_sc[0]
        for n in range(1, n_past + 1):
            o = o + wgt[:, n:n + 1] * o_sc[n]
        o = o / l_fin

        col = lax.broadcasted_iota(jnp.int32, (n_new, w_a), 1)
        ya = jnp.zeros((n_new, w_a), F32)
        for h in range(h_a):
            ya = ya + jnp.where(col // DH == h, o[h * n_new:(h + 1) * n_new], 0.0)
        ya_ref[...] = ya
        lam = lam_ref[0]
        dv = w_a // h_b
        outs = []
        for h in range(h_b):
            r0 = rows_a + 2 * h * n_new
            d = (o[r0:r0 + n_new, h * dv:(h + 1) * dv]
                 - lam * o[r0 + n_new:r0 + 2 * n_new, h * dv:(h + 1) * dv])
            rms = lax.rsqrt(jnp.mean(d * d, axis=1, keepdims=True) + RMS_EPS)
            outs.append(d * rms * g_ref[...] * out_scale)
        yb_ref[...] = jnp.concatenate(outs, axis=1)


def _sample_attn(page_table, lam, hq, ka, va, kb, vb, caches, layer, bias_last, bias_new, subln_g,
                 *, n_new, h_a, h_b, lam_init, blocks_per_step):
    dec_batch, n_pages = page_table.shape
    page = caches[0].shape[2]
    assert 2 * page == MOBA_BLOCK and n_pages % (2 * blocks_per_step) == 0
    w_a = h_a * DH
    rows = (h_a + 2 * h_b) * n_new
    n_steps = n_pages // (2 * blocks_per_step)
    pps = 2 * blocks_per_step
    n_blocks = n_pages // 2 + 1
    caches = [caches[0].transpose(0, 1, 3, 4, 2), caches[1].transpose(0, 1, 3, 4, 2),
              caches[2].reshape(caches[2].shape[:2] + (page * h_b, -1)),
              caches[3].reshape(caches[3].shape[:2] + (page * h_b, -1))]

    def page_spec(v, c):
        shape = (None, None) + c.shape[2:]
        zeros = (0,) * (c.ndim - 2)
        return pl.BlockSpec(shape, lambda b, g, pt: (layer, pt[b * n_pages + g * pps + v]) + zeros)

    new_spec = pl.BlockSpec((n_new, w_a), lambda b, g, pt: (b, 0))
    in_specs = [
        pl.BlockSpec(memory_space=pltpu.SMEM),
        pl.BlockSpec((n_new, hq.shape[1]), lambda b, g, pt: (b, 0)),
        new_spec, new_spec, new_spec, new_spec,
        pl.BlockSpec((rows, MOBA_BLOCK), lambda b, g, pt: (0, 0)),
        pl.BlockSpec((rows, LANES), lambda b, g, pt: (0, 0)),
        pl.BlockSpec((1, w_a // h_b), lambda b, g, pt: (0, 0)),
    ]
    operands = [lam, hq, ka, va, kb, vb, bias_last, bias_new, subln_g]
    for v in range(pps):
        for c in caches:
            in_specs.append(page_spec(v, c))
            operands.append(c)
    grid_spec = pltpu.PrefetchScalarGridSpec(
        num_scalar_prefetch=1,
        grid=(dec_batch, n_steps),
        in_specs=in_specs,
        out_specs=[new_spec, new_spec],
        scratch_shapes=[
            pltpu.VMEM((rows, 2 * w_a), BF16),
            pltpu.VMEM((rows, LANES), F32), pltpu.VMEM((rows, LANES), F32), pltpu.VMEM((rows, LANES), F32),
            pltpu.VMEM((n_blocks, rows, w_a), F32),
        ],
    )
    return pl.pallas_call(
        functools.partial(_sample_attn_kernel, n_steps=n_steps, blocks_per_step=blocks_per_step, n_new=n_new,
                          h_a=h_a, h_b=h_b, page=page, out_scale=1.0 - lam_init),
        grid_spec=grid_spec,
        out_shape=[jax.ShapeDtypeStruct((dec_batch * n_new, w_a), F32)] * 2,
        compiler_params=_cparams(("parallel", "arbitrary")),
    )(page_table.reshape(-1), *operands)


def _layernorm(x, g, b):
    mu = jnp.mean(x, axis=-1, keepdims=True)
    xc = x - mu
    var = jnp.mean(xc * xc, axis=-1, keepdims=True)
    return xc * lax.rsqrt(var + LN_EPS) * g + b


def _mix_kernel(x_ref, ya_ref, yb_ref, wg_ref, bg_ref, wa_ref, wb_ref, wo_ref, g_ref, b_ref, o_ref, *, alpha):
    x = x_ref[...]
    d = x.shape[1]
    gates = jax.nn.sigmoid(_dot(x.astype(BF16), wg_ref[...]) + bg_ref[...])
    z = (gates[:, :d] * _dot(ya_ref[...].astype(BF16), wa_ref[...])
         + gates[:, d:] * _dot(yb_ref[...].astype(BF16), wb_ref[...]))
    mix = _dot(z.astype(BF16), wo_ref[...])
    o_ref[...] = _layernorm(alpha * x + mix, g_ref[...], b_ref[...])


def _mix(x2, ya, yb, wg, bg, wa, wb, wo, g, b, *, tm, alpha):
    m, d = x2.shape
    w = ya.shape[1]
    row = lambda c: pl.BlockSpec((tm, c), lambda i: (i, 0))
    return pl.pallas_call(
        functools.partial(_mix_kernel, alpha=alpha),
        grid=(m // tm,),
        in_specs=[row(d), row(w), row(w), _resident(wg.shape), _resident(bg.shape), _resident(wa.shape),
                  _resident(wb.shape), _resident(wo.shape), _resident(g.shape), _resident(b.shape)],
        out_specs=row(d),
        out_shape=jax.ShapeDtypeStruct((m, d), F32),
        compiler_params=_cparams(("parallel",)),
    )(x2, ya, yb, wg, bg, wa, wb, wo, g, b)


def _gelu_tanh(x):
    return 0.5 * x * (1.0 + jnp.tanh(math.sqrt(2.0 / math.pi) * (x + 0.044715 * (x * x * x))))


def _ffn_kernel(x_ref, p_ref, prev_ref, wu_ref, cw_ref, cb_ref, wd_ref, g_ref, b_ref, wpg_ref, wp_ref,
                o_ref, st_ref, carry_sc, *, alpha, seq_rows, n_chunks):
    x = x_ref[...]
    tm, d = x.shape
    xb = x.astype(BF16)
    long_seq = seq_rows is None
    if long_seq:
        first = pl.program_id(1) == 0
        row = lax.broadcasted_iota(jnp.int32, (tm, FFN_CHUNK), 0)
    else:
        groups = tm // seq_rows
        sub = lax.broadcasted_iota(jnp.int32, (groups, seq_rows, FFN_CHUNK), 1)

    def conv(c, u):
        cw = cw_ref[c]
        if long_seq:
            @pl.when(first)
            def _():
                carry_sc[c] = prev_ref[0, :, c * FFN_CHUNK:(c + 1) * FFN_CHUNK]

            hist = carry_sc[c]
            u1 =jnp.where(row == 0, hist[1:2], pltpu.roll(u, 1, 0))
            u2 = jnp.where(row == 0, hist[0:1], jnp.where(row == 1, hist[1:2], pltpu.roll(u, 2, 0)))
            carry_sc[c] = u[tm - 2:tm]
            st_ref[0, :, c * FFN_CHUNK:(c + 1) * FFN_CHUNK] = u[tm - 2:tm]
            return cw[0:1] * u2 + cw[1:2] * u1 + cw[2:3] * u + cb_ref[c]
        u3 = u.reshape(groups, seq_rows, FFN_CHUNK)
        hist = prev_ref[:, :, c * FFN_CHUNK:(c + 1) * FFN_CHUNK]
        back = lambda k: pltpu.roll(u, k, 0).reshape(groups, seq_rows, FFN_CHUNK)
        u1 = jnp.where(sub == 0, hist[:, 1:2], back(1))
        u2 = jnp.where(sub == 0, hist[:, 0:1], jnp.where(sub == 1, hist[:, 1:2], back(2)))
        st_ref[:, :, c * FFN_CHUNK:(c + 1) * FFN_CHUNK] = u3[:, seq_rows - 2:seq_rows]
        out = cw[0:1] * u2 + cw[1:2] * u1 + cw[2:3] * u3 + cb_ref[c]
        return out.reshape(tm, FFN_CHUNK)

    f = jnp.zeros((tm, d), F32)
    for c in range(n_chunks):
        c_gate = conv(c, _dot(xb, wu_ref[c]))
        c_val = conv(n_chunks + c, _dot(xb, wu_ref[n_chunks + c]))
        f = f + _dot((_gelu_tanh(c_gate) * c_val).astype(BF16), wd_ref[c])
    y = _layernorm(alpha * x + f, g_ref[...], b_ref[...])
    gate = jax.nn.sigmoid(_dot(y.astype(BF16), wpg_ref[...]))
    o_ref[...] = y + gate * _dot(p_ref[...].astype(BF16), wp_ref[...])


def _ffn(x2, p2, prev, wu, cw, cb, wd, g, b, wpg, wp, *, tm, alpha, batch, seq_rows):
    m, d = x2.shape
    d_up = prev.shape[-1]
    n_chunks = wd.shape[0]
    d_ple = p2.shape[1]
    if seq_rows is None:
        tiles = m // batch // tm
        grid = (batch, tiles)
        row = lambda c: pl.BlockSpec((tm, c), lambda bb, i: (bb * tiles + i, 0))
        st_spec = pl.BlockSpec((1, CONV_W - 1, d_up), lambda bb, i: (bb, 0, 0))
        sem = ("parallel", "arbitrary")
    else:
        grid = (1, m // tm)
        row = lambda c: pl.BlockSpec((tm, c), lambda bb, i: (i, 0))
        st_spec = pl.BlockSpec((tm // seq_rows, CONV_W - 1, d_up), lambda bb, i: (i, 0, 0))
        sem = ("arbitrary", "arbitrary")
    n_seq = prev.shape[0]
    return pl.pallas_call(
        functools.partial(_ffn_kernel, alpha=alpha, seq_rows=seq_rows, n_chunks=n_chunks),
        grid=grid,
        in_specs=[row(d), row(d_ple), st_spec, _resident(wu.shape), _resident(cw.shape), _resident(cb.shape),
                  _resident(wd.shape), _resident(g.shape), _resident(b.shape), _resident(wpg.shape),
                  _resident(wp.shape)],
        out_specs=[row(d), st_spec],
        out_shape=[jax.ShapeDtypeStruct((m, d), F32), jax.ShapeDtypeStruct((n_seq, CONV_W - 1, d_up), F32)],
        scratch_shapes=[pltpu.VMEM((2 * n_chunks, CONV_W - 1, FFN_CHUNK), F32)],
        compiler_params=_cparams(sem),
    )(x2, p2, prev, wu, cw, cb, wd, g, b, wpg, wp)


def _layer_weights(l, w):
    d_ff = w['w_down'].shape[1]
    n_chunks = d_ff // FFN_CHUNK
    d = w['w_up'].shape[1]
    chunked = lambda a: a.reshape(a.shape[0], 2 * n_chunks, FFN_CHUNK).transpose(1, 0, 2)
    return {
        'w_in': w['w_in'][l].astype(BF16),
        'w_gate': w['w_gate'][l].astype(BF16),
        'b_gate': w['b_gate'][l][None, :],
        'w_branch_a': w['w_branch_a'][l].astype(BF16),
        'w_branch_b': w['w_branch_b'][l].astype(BF16),
        'w_out': w['w_out'][l].astype(BF16),
        'ln1_g': w['ln1_g'][l][None, :], 'ln1_b': w['ln1_b'][l][None, :],
        'w_up': chunked(w['w_up'][l]).astype(BF16),
        'conv_w': chunked(w['conv_w'][l]),
        'conv_b': chunked(w['conv_b'][l][None, :]),
        'w_down': w['w_down'][l].reshape(n_chunks, FFN_CHUNK, d).astype(BF16),
        'ln2_g': w['ln2_g'][l][None, :], 'ln2_b': w['ln2_b'][l][None, :],
        'w_ple_gate': w['w_ple_gate'][l].astype(BF16),
        'w_ple': w['w_ple'][l].astype(BF16),
        'subln_g': w['diff_subln_g'][l][None, :],
        'lam': (jnp.exp(jnp.sum(w['diff_lq1'][l] * w['diff_lk1'][l]))
                - jnp.exp(jnp.sum(w['diff_lq2'][l] * w['diff_lk2'][l]))
                + (0.8 - 0.6 * math.exp(-0.3 * l))).reshape(1).astype(F32),
    }


def _tail(x2, ya, yb, p2, prev, wl, *, alpha, tm_mix, tm_ffn, batch, seq_rows):
    x1 = _mix(x2, ya, yb, wl['w_gate'], wl['b_gate'], wl['w_branch_a'], wl['w_branch_b'], wl['w_out'],
              wl['ln1_g'], wl['ln1_b'], tm=tm_mix, alpha=alpha)
    return _ffn(x1, p2, prev, wl['w_up'], wl['conv_w'], wl['conv_b'], wl['w_down'], wl['ln2_g'], wl['ln2_b'],
                wl['w_ple_gate'], wl['w_ple'], tm=tm_ffn, alpha=alpha, batch=batch, seq_rows=seq_rows)


def _forward(x_prompt, x_sample, cache_a_k, cache_a_v, cache_b_k, cache_b_v, state_conv, page_table,
             p_prompt, p_sample, w, *, tm_prompt=256, tm_sample=256, blocks_per_step=1):
    depth = w['w_in'].shape[0]
    batch, seq, d = x_prompt.shape
    dec_batch, n_new, _ = x_sample.shape
    h_a, dh = cache_a_k.shape[3], cache_a_k.shape[4]
    h_b = cache_b_k.shape[3]
    assert dh == DH and cache_b_k.shape[4] == 2 * DH and cache_b_v.shape[4] == 2 * DH
    d_up = state_conv.shape[-1]
    past_len = page_table.shape[1] * cache_a_k.shape[2]
    alpha = (2 * depth) ** 0.25
    tab = w['rel_bias'].T
    bias_prompt = _prompt_bias_tiles(tab)
    bias_last, bias_new = _sample_bias_tiles(tab[:h_a], tab[h_a:], n_new, past_len)
    caches = (cache_a_k, cache_a_v, cache_b_k, cache_b_v)

    xp = x_prompt.reshape(batch * seq, d)
    xs = x_sample.reshape(dec_batch * n_new, d)
    conv0 = jnp.zeros((batch, CONV_W - 1, d_up), F32)
    rows_p, rows_s = [], []
    for l in range(depth):
        wl = _layer_weights(l, w)
        lam_init = 0.8 - 0.6 * math.exp(-0.3 * l)

        ka, va, kb, vb, hq, kmean = _proj(xp, wl['w_in'], tm=tm_prompt, hq_dtype=BF16, with_kmean=True)
        nb = seq // MOBA_BLOCK
        kmean_pad = jnp.pad(kmean.reshape(batch, nb, h_a * DH), ((0, 0), (0, LANES - nb), (0, 0)))
        ya = _moba_prompt(hq, kmean_pad, bias_prompt[:h_a], batch=batch, seq=seq, h_a=h_a)
        yb = _diff_prompt(wl['lam'], hq, bias_prompt[h_a:], wl['subln_g'], batch=batch, seq=seq, h_b=h_b,
                          lam_init=lam_init)
        xp, conv_p = _tail(xp, ya, yb, p_prompt[l].reshape(batch * seq, -1), conv0, wl, alpha=alpha,
                           tm_mix=tm_prompt, tm_ffn=tm_prompt, batch=batch, seq_rows=None)
        rows_p.append((ka, va, kb, vb, conv_p))

        ka, va, kb, vb, hq = _proj(xs, wl['w_in'], tm=tm_sample, hq_dtype=F32, with_kmean=False)
        ya, yb = _sample_attn(page_table, wl['lam'], hq, ka, va, kb, vb, caches, l, bias_last, bias_new,
                              wl['subln_g'], n_new=n_new, h_a=h_a, h_b=h_b, lam_init=lam_init,
                              blocks_per_step=blocks_per_step)
        xs, conv_s = _tail(xs, ya, yb, p_sample[l].reshape(dec_batch * n_new, -1), state_conv[l], wl,
                           alpha=alpha, tm_mix=tm_sample, tm_ffn=tm_sample, batch=1, seq_rows=n_new)
        rows_s.append((ka, va, kb, vb, conv_s))

    def stack(rows, i, shape):
        return jnp.stack([r[i] for r in rows]).reshape((depth,) + shape)

    dv = cache_b_v.shape[4]
    return (xp.reshape(batch, seq, d), xs.reshape(dec_batch, n_new, d),
            stack(rows_p, 0, (batch, seq, h_a, DH)), stack(rows_p, 1, (batch, seq, h_a, DH)),
            stack(rows_p, 2, (batch, seq, h_b, 2 * DH)), stack(rows_p, 3, (batch, seq, h_b, dv)),
            stack(rows_p, 4, (batch, CONV_W - 1, d_up)),
            stack(rows_s, 0, (dec_batch, n_new, h_a, DH)), stack(rows_s, 1, (dec_batch, n_new, h_a, DH)),
            stack(rows_s, 2, (dec_batch, n_new, h_b, 2 * DH)), stack(rows_s, 3, (dec_batch, n_new, h_b, dv)),
            stack(rows_s, 4, (dec_batch, CONV_W - 1, d_up)))


def kernel(x_prompt, x_sample, cache_a_k, cache_a_v, cache_b_k, cache_b_v, state_conv, page_table, p_prompt, p_sample, w_in, w_gate, b_gate, w_branch_a, w_branch_b, w_out, rel_bias, diff_lq1, diff_lk1, diff_lq2, diff_lk2, diff_subln_g, ln1_g, ln1_b, w_up, conv_w, conv_b, w_down, ln2_g, ln2_b, w_ple, w_ple_gate):
    w = {'w_in': w_in, 'w_gate': w_gate, 'b_gate': b_gate, 'w_branch_a': w_branch_a,
         'w_branch_b': w_branch_b, 'w_out': w_out, 'rel_bias': rel_bias, 'diff_lq1': diff_lq1,
         'diff_lk1': diff_lk1, 'diff_lq2': diff_lq2, 'diff_lk2': diff_lk2, 'diff_subln_g': diff_subln_g,
         'ln1_g': ln1_g, 'ln1_b': ln1_b, 'w_up': w_up, 'conv_w': conv_w, 'conv_b': conv_b,
         'w_down': w_down, 'ln2_g': ln2_g, 'ln2_b': ln2_b, 'w_ple': w_ple, 'w_ple_gate': w_ple_gate}
    return _forward(x_prompt, x_sample, cache_a_k, cache_a_v, cache_b_k, cache_b_v, state_conv, page_table,
                    p_prompt, p_sample, w)
```

```python
import functools
import math

import numpy as np
import jax
import jax.numpy as jnp
from jax import lax
from jax.experimental import pallas as pl
from jax.experimental.pallas import tpu as pltpu

F32 = jnp.float32
BF16 = jnp.bfloat16

DH = 64
MOBA_BLOCK = 256
MOBA_TOPK = 3
N_BUCKETS = 32
MAX_DISTANCE = 128
CONV_W = 3
LN_EPS = 1e-5
RMS_EPS = 1e-5
NEG_INF = -1e30
LOG2E = math.log2(math.e)
LANES = 128
BF16_ROWS = 16
VMEM_LIMIT = 56 * 1024 * 1024

ATT_TILE = 2 * MOBA_BLOCK
BLOCKS_PER_TILE = ATT_TILE // MOBA_BLOCK
FFN_CHUNK = 256


def _cparams(sem):
    return pltpu.CompilerParams(dimension_semantics=sem, vmem_limit_bytes=VMEM_LIMIT)


def _dot(a, b):
    return jnp.dot(a, b, preferred_element_type=F32)


def _dot_nt(a, b):
    return lax.dot_general(a, b, (((1,), (1,)), ((), ())), preferred_element_type=F32)


def _resident(shape):
    return pl.BlockSpec(shape, lambda *_: (0,) * len(shape), pipeline_mode=pl.Buffered(1))


def _t5_bucket_np(dist):
    dist = np.maximum(dist, 0)
    exact = N_BUCKETS // 2
    logv = (np.log(np.maximum(dist, 1).astype(np.float32) / np.float32(exact))
            / np.float32(math.log(MAX_DISTANCE / exact))).astype(np.float32)
    large = np.minimum(exact + (logv * np.float32(N_BUCKETS - exact)).astype(np.int32), N_BUCKETS - 1)
    return np.where(dist < exact, dist, large).astype(np.int32)


def _bias_lookup(tab_rows, buckets, per_row=False):
    shape = buckets.shape if per_row else (tab_rows.shape[0],) + buckets.shape
    out = jnp.zeros(shape, F32)
    for n in np.unique(buckets):
        col = tab_rows[:, n].reshape((-1,) + (1,) * (len(shape) - 1))
        out = jnp.where(jnp.asarray(buckets == n), col, out)
    return out


def _prompt_bias_tiles(tab):
    t = ATT_TILE
    j = np.arange(t)[:, None]
    i = np.arange(t)[None, :]
    far = tab[:, N_BUCKETS - 1][:, None, None]
    diag = jnp.where(jnp.asarray(i >= j)[None], (_bias_lookup(tab, _t5_bucket_np(i - j)) - far) * LOG2E, NEG_INF)
    prev = (_bias_lookup(tab, _t5_bucket_np(t + i - j)) - far) * LOG2E
    return jnp.stack([diag, prev], axis=1).astype(F32)


def _sample_bias_tiles(tab, h_a, n_new, past_len):
    h_b = tab.shape[0] - h_a
    rows = (h_a + 2 * h_b) * n_new
    r = np.arange(rows)
    qi = (r % n_new)[:, None]
    head = np.where(r < h_a * n_new, r // n_new, h_a + (r - h_a * n_new) // (2 * n_new))
    onehot = jnp.asarray(head[:, None] == np.arange(tab.shape[0])[None, :], F32)
    tab_rows = jnp.dot(onehot, tab, precision=lax.Precision.HIGHEST)
    far = tab_rows[:, N_BUCKETS - 1:]
    t = np.arange(MOBA_BLOCK)[None, :]
    dist_last = past_len + qi - (past_len - MOBA_BLOCK + t)
    last = (_bias_lookup(tab_rows, _t5_bucket_np(dist_last), per_row=True) - far) * LOG2E
    j = np.arange(LANES)[None, :]
    new = jnp.where(jnp.asarray((j <= qi) & (j < n_new)),
                    (_bias_lookup(tab_rows, _t5_bucket_np(qi - j), per_row=True) - far) * LOG2E, NEG_INF)
    return last.astype(F32), new.astype(F32)


def _proj_prompt_kernel(x_ref, w_ref, hq_ref, kat_ref, vat_ref, kb_ref, vb_ref, vat16_ref, vbt16_ref, kmean_ref,
                        *, seg, q_scale, h_b):
    x = x_ref[...].astype(BF16)
    dv = seg // h_b
    for s in range(6):
        h = _dot(x, w_ref[:, s * seg:(s + 1) * seg])
        if s == 0:
            hq_ref[:, 0:seg] = (h * q_scale).astype(BF16)
        elif s == 1:
            hq_ref[:, seg:2 * seg] = h.astype(BF16)
            for n in range(BLOCKS_PER_TILE):
                blk = h[n * MOBA_BLOCK:(n + 1) * MOBA_BLOCK]
                kmean_ref[n] = jnp.sum(blk, axis=0, keepdims=True) * (1.0 / MOBA_BLOCK)
            kat_ref[...] = h.T
        elif s == 2:
            ht = h.T
            vat_ref[...] = ht
            vat16_ref[...] = ht.astype(BF16)
        elif s == 3:
            hq_ref[:, 2 * seg:3 * seg] = (h * q_scale).astype(BF16)
        elif s == 4:
            hq_ref[:, 3 * seg:4 * seg] = h.astype(BF16)
            for hh in range(h_b):
                kb_ref[:, hh, :] = h[:, hh * dv:(hh + 1) * dv]
        else:
            for hh in range(h_b):
                vb_ref[:, hh, :] = h[:, hh * dv:(hh + 1) * dv]
            vbt16_ref[...] = h.T.astype(BF16)


def _proj_prompt(x2, w_in_l, *, batch, seq, h_b):
    tm = ATT_TILE
    m, d = x2.shape
    seg = w_in_l.shape[1] // 6
    nt = seq // tm
    dv = seg // h_b
    row = lambda w: pl.BlockSpec((tm, w), lambda b, i: (b * nt + i, 0))
    feat = pl.BlockSpec((None, seg, tm), lambda b, i: (b, 0, i))
    heads = pl.BlockSpec((tm, h_b, dv), lambda b, i: (b * nt + i, 0, 0))
    tiles = pl.BlockSpec((None, None, seg, tm), lambda b, i: (b, i, 0, 0))
    return pl.pallas_call(
        functools.partial(_proj_prompt_kernel, seg=seg, q_scale=DH ** -0.5 * LOG2E, h_b=h_b),
        grid=(batch, nt),
        in_specs=[row(d), _resident(w_in_l.shape)],
        out_specs=[row(4 * seg), feat, feat, heads, heads, tiles, tiles,
                   pl.BlockSpec((BLOCKS_PER_TILE, 1, seg), lambda b, i: (b * nt + i, 0, 0))],
        out_shape=[jax.ShapeDtypeStruct((m, 4 * seg), BF16),
                   jax.ShapeDtypeStruct((batch, seg, seq), F32), jax.ShapeDtypeStruct((batch, seg, seq), F32),
                   jax.ShapeDtypeStruct((m, h_b, dv), F32), jax.ShapeDtypeStruct((m, h_b, dv), F32),
                   jax.ShapeDtypeStruct((batch, nt, seg, tm), BF16), jax.ShapeDtypeStruct((batch, nt, seg, tm), BF16),
                   jax.ShapeDtypeStruct((m // MOBA_BLOCK, 1, seg), F32)],
        compiler_params=_cparams(("parallel", "parallel")),
    )(x2, w_in_l)


def _proj_sample_kernel(x_ref, w_ref, hq_ref, ka_ref, va_ref, kb_ref, vb_ref, *, seg, q_scale):
    x = x_ref[...].astype(BF16)
    outs = (None, ka_ref, va_ref, None, kb_ref, vb_ref)
    for s in range(6):
        h = _dot(x, w_ref[:, s * seg:(s + 1) * seg])
        if s == 0:
            hq_ref[:, 0:seg] = h * q_scale
        elif s == 3:
            hq_ref[:, seg:2 * seg] = h * q_scale
        else:
            outs[s][...] = h


def _proj_sample(x2, w_in_l, *, tm):
    m, d = x2.shape
    seg = w_in_l.shape[1] // 6
    row = lambda w: pl.BlockSpec((tm, w), lambda i: (i, 0))
    return pl.pallas_call(
        functools.partial(_proj_sample_kernel, seg=seg, q_scale=DH ** -0.5 * LOG2E),
        grid=(m // tm,),
        in_specs=[row(d), _resident(w_in_l.shape)],
        out_specs=[row(2 * seg)] + [row(seg)] * 4,
        out_shape=[jax.ShapeDtypeStruct((m, 2 * seg), F32)] + [jax.ShapeDtypeStruct((m, seg), F32)] * 4,
        compiler_params=_cparams(("parallel",)),
    )(x2, w_in_l)


def _flash_step(s, vt_ext, m_sc, acc_sc, h, first):
    m_tile = jnp.max(s, axis=0, keepdims=True)
    if first:
        m_new = m_tile
    else:
        m_old = m_sc[h]
        m_new = jnp.maximum(m_old, m_tile)
    pv = _dot(vt_ext, jnp.exp2(s - m_new).astype(BF16))
    if first:
        acc_sc[h] = pv
    else:
        acc_sc[h] = jnp.exp2(m_old - m_new) * acc_sc[h] + pv
    m_sc[h] = m_new


def _moba_prompt_kernel(q_ref, k_ref, vt_ref, kmean_ref, bias_ref, o_ref, m_sc, acc_sc):
    t = ATT_TILE
    qi = pl.program_id(2)
    qt = q_ref[...].astype(F32).T
    row = lax.broadcasted_iota(jnp.int32, (LANES, t), 0)
    head0 = row < DH
    q_heads = (jnp.where(head0, qt, 0.0).astype(BF16), jnp.where(head0, 0.0, qt).astype(BF16))
    kmean = kmean_ref[...].astype(BF16)
    cur = qi * BLOCKS_PER_TILE + lax.broadcasted_iota(jnp.int32, (LANES, t), 1) // MOBA_BLOCK

    def block_mask(qh):
        gate = jnp.where(row < cur, _dot(kmean, qh), NEG_INF)
        chosen = row == cur
        for j in range(MOBA_TOPK):
            best = jnp.max(gate, axis=0, keepdims=True)
            idx = jnp.min(jnp.where(gate == best, row, 2 * LANES), axis=0, keepdims=True)
            hit = row == idx
            chosen = jnp.logical_or(chosen, jnp.logical_and(hit, cur > j))
            gate = jnp.where(hit, -3e38, gate)
        return jnp.where(chosen, 0.0, NEG_INF).astype(BF16)

    q_ext = tuple(jnp.concatenate([qh, block_mask(qh)], axis=0) for qh in q_heads)
    lane = lax.broadcasted_iota(jnp.int32, (t, LANES), 1)
    key_block = lax.broadcasted_iota(jnp.int32, (t, LANES), 0) // MOBA_BLOCK
    ones = jnp.ones((BF16_ROWS, t), BF16)

    def step(kj, bias_idx, first):
        rows = pl.ds(pl.multiple_of(kj * t, t), t)
        onehot = jnp.where(lane == kj * BLOCKS_PER_TILE + key_block, 1.0, 0.0).astype(BF16)
        k_ext = jnp.concatenate([k_ref[rows, :], onehot], axis=1)
        vt = vt_ref[kj]
        for h in range(2):
            s = _dot(k_ext, q_ext[h])
            if bias_idx is not None:
                s = s + bias_ref[h, bias_idx]
            vt_ext = jnp.concatenate([vt[h * DH:(h + 1) * DH], ones], axis=0)
            _flash_step(s, vt_ext, m_sc, acc_sc, h, first)

    step(qi, 0, True)

    @pl.when(qi >= 1)
    def _():
        step(qi - 1, 1, False)

    def far_body(kj, carry):
        step(kj, None, False)
        return carry

    lax.fori_loop(0, jnp.maximum(qi - 1, 0), far_body, 0)
    out = [acc_sc[h, 0:DH] / acc_sc[h, DH:DH + 1] for h in range(2)]
    o_ref[...] = jnp.concatenate(out, axis=0).T.astype(o_ref.dtype)


def _moba_prompt(hq, vat16, kmean_pad, bias_a, *, batch, seq, h_a):
    t = ATT_TILE
    pairs = h_a * DH // LANES
    seg_blocks = hq.shape[1] // 4 // LANES
    nq = seq // t
    return pl.pallas_call(
        _moba_prompt_kernel,
        grid=(batch, pairs, nq),
        in_specs=[
            pl.BlockSpec((t, LANES), lambda b, j, i: (b * nq + i, j)),
            pl.BlockSpec((seq, LANES), lambda b, j, i: (b, seg_blocks + j)),
            pl.BlockSpec((None, nq, LANES, t), lambda b, j, i: (b, 0, j, 0)),
            pl.BlockSpec((None, LANES, LANES), lambda b, j, i: (b, 0, j)),
            pl.BlockSpec((2, 2, t, t), lambda b, j, i: (j, 0, 0, 0)),
        ],
        out_specs=pl.BlockSpec((t, LANES), lambda b, j, i: (b * nq + i, j)),
        out_shape=jax.ShapeDtypeStruct((batch * seq, pairs * LANES), BF16),
        scratch_shapes=[pltpu.VMEM((2, 1, t), F32), pltpu.VMEM((2, DH + BF16_ROWS, t), F32)],
        compiler_params=_cparams(("parallel", "parallel", "arbitrary")),
    )(hq, hq, vat16, kmean_pad, bias_a)


def _diff_prompt_kernel(lam_ref, q_ref, k_ref, vt_ref, bias_ref, g_ref, o_ref, m_sc, acc_sc, *, out_scale):
    t = ATT_TILE
    dv = LANES
    qi = pl.program_id(2)
    qt = q_ref[...].astype(F32).T
    row = lax.broadcasted_iota(jnp.int32, (LANES, t), 0)
    half0 = row < DH
    q_halves = (jnp.where(half0, qt, 0.0).astype(BF16), jnp.where(half0, 0.0, qt).astype(BF16))
    ones = jnp.ones((BF16_ROWS, t), BF16)

    def step(kj, bias_idx, first):
        k = k_ref[pl.ds(pl.multiple_of(kj * t, t), t), :]
        vt_ext = jnp.concatenate([vt_ref[kj], ones], axis=0)
        for h in range(2):
            s = _dot(k, q_halves[h])
            if bias_idx is not None:
                s = s + bias_ref[bias_idx]
            _flash_step(s, vt_ext, m_sc, acc_sc, h, first)

    step(qi, 0, True)

    @pl.when(qi >= 1)
    def _():
        step(qi - 1, 1, False)

    def far_body(kj, carry):
        step(kj, None, False)
        return carry

    lax.fori_loop(0, jnp.maximum(qi - 1, 0), far_body, 0)
    o = (acc_sc[0, 0:dv] / acc_sc[0, dv:dv + 1]
         - lam_ref[0] * (acc_sc[1, 0:dv] / acc_sc[1, dv:dv + 1]))
    rms = lax.rsqrt(jnp.mean(o * o, axis=0, keepdims=True) + RMS_EPS)
    o_ref[...] = (o * rms * g_ref[...] * out_scale).T.astype(o_ref.dtype)


def _diff_prompt(lam, hq, vbt16, bias_b, subln_g, *, batch, seq, h_b, lam_init):
    t = ATT_TILE
    seg_blocks = hq.shape[1] // 4 // LANES
    nq = seq // t
    g_spread = jnp.broadcast_to(subln_g.reshape(LANES, 1), (LANES, t))
    return pl.pallas_call(
        functools.partial(_diff_prompt_kernel, out_scale=1.0 - lam_init),
        grid=(batch, h_b, nq),
        in_specs=[
            pl.BlockSpec(memory_space=pltpu.SMEM),
            pl.BlockSpec((t, LANES), lambda b, h, i: (b * nq + i, 2 * seg_blocks + h)),
            pl.BlockSpec((seq, LANES), lambda b, h, i: (b, 3 * seg_blocks + h)),
            pl.BlockSpec((None, nq, LANES, t), lambda b, h, i: (b, 0, h, 0)),
            pl.BlockSpec((None, 2, t, t), lambda b, h, i: (h, 0, 0, 0)),
            pl.BlockSpec((LANES, t), lambda b, h, i: (0, 0)),
        ],
        out_specs=pl.BlockSpec((t, LANES), lambda b, h, i: (b * nq + i, h)),
        out_shape=jax.ShapeDtypeStruct((batch * seq, h_b * LANES), BF16),
        scratch_shapes=[pltpu.VMEM((2, 1, t), F32), pltpu.VMEM((2, LANES + BF16_ROWS, t), F32)],
        compiler_params=_cparams(("parallel", "parallel", "arbitrary")),
    )(lam, hq, hq, vbt16, bias_b, g_spread)


def _softmax_tile(s):
    m = jnp.max(s, axis=1, keepdims=True)
    p = jnp.exp2(s - m)
    return m, p, jnp.sum(p, axis=1, keepdims=True)


def _sample_attn_kernel(pt_ref, lam_ref, hq_ref, ka_new_ref, va_new_ref, kb_new_ref, vb_new_ref,
                        bias_last_ref, bias_new_ref, g_ref, *refs,
                        n_steps, blocks_per_step, n_new, h_a, h_b, page, out_scale):
    del pt_ref
    n_pages = 2 * blocks_per_step
    page_refs = refs[:4 * n_pages]
    ya_ref, yb_ref = refs[4 * n_pages:4 * n_pages + 2]
    w_sc, gate_sc, m_sc, l_sc, o_sc = refs[4 * n_pages + 2:]
    g = pl.program_id(1)
    w_a = h_a * DH
    rows_a = h_a * n_new
    rows = rows_a + 2 * h_b * n_new
    n_past = n_steps * blocks_per_step
    lane = lax.broadcasted_iota(jnp.int32, (rows, LANES), 1)

    @pl.when(g == 0)
    def _():
        tiled = jnp.concatenate([hq_ref[...]] * (rows // n_new), axis=0)
        r = lax.broadcasted_iota(jnp.int32, tiled.shape, 0)
        c = lax.broadcasted_iota(jnp.int32, tiled.shape, 1)
        w_sc[...] = jnp.where(c // DH == r // n_new, tiled, 0.0).astype(BF16)
        gate_sc[...] = jnp.zeros_like(gate_sc)
        m_sc[...] = jnp.zeros_like(m_sc)
        l_sc[...] = jnp.zeros_like(l_sc)

    def partial_block(n, s, bias, pv_a, pv_b):
        gsum = jnp.sum(s, axis=1, keepdims=True)
        m, p, l = _softmax_tile(s + bias)
        p = p.astype(BF16)
        o_sc[n, 0:rows_a, :] = pv_a(p[0:rows_a])
        o_sc[n, rows_a:rows, :] = pv_b(p[rows_a:rows])
        here = lane == n
        gate_sc[...] = jnp.where(here, gsum, gate_sc[...])
        m_sc[...] = jnp.where(here, m, m_sc[...])
        l_sc[...] = jnp.where(here, l, l_sc[...])

    def heads_on_lanes(ref):
        return jnp.concatenate([ref[pl.ds(h, page, stride=h_b), :] for h in range(h_b)], axis=1).astype(BF16)

    for u in range(blocks_per_step):
        n = g * blocks_per_step + u
        pages = [page_refs[4 * (2 * u + v):4 * (2 * u + v) + 4] for v in range(2)]
        ka_t = [pg[0][...].reshape(w_a, page).astype(BF16) for pg in pages]
        va_t = [pg[1][...].reshape(w_a, page).astype(BF16) for pg in pages]
        kb = [heads_on_lanes(pg[2]) for pg in pages]
        vb = [heads_on_lanes(pg[3]) for pg in pages]
        q_a = w_sc[0:rows_a, 0:w_a]
        q_b = w_sc[rows_a:rows, w_a:2 * w_a]
        s = jnp.concatenate([
            jnp.concatenate([_dot(q_a, k) for k in ka_t], axis=1),
            jnp.concatenate([_dot_nt(q_b, k) for k in kb], axis=1)], axis=0)
        pv_a = lambda p: sum(_dot_nt(p[:, v * page:(v + 1) * page], va_t[v]) for v in range(2))
        pv_b = lambda p: sum(_dot(p[:, v * page:(v + 1) * page], vb[v]) for v in range(2))
        is_last = (n == n_past - 1).astype(F32)
        partial_block(n, s, bias_last_ref[...] * is_last, pv_a, pv_b)

    @pl.when(g == n_steps - 1)
    def _():
        pad = jnp.zeros((LANES - n_new, w_a), F32)
        padded = lambda ref: jnp.concatenate([ref[...], pad], axis=0).astype(BF16)
        k_cat = jnp.concatenate([padded(ka_new_ref), padded(kb_new_ref)], axis=1)
        va, vb = padded(va_new_ref), padded(vb_new_ref)
        partial_block(n_past, _dot_nt(w_sc[...], k_cat), bias_new_ref[...],
                      lambda p: _dot(p, va), lambda p: _dot(p, vb))

        row = lax.broadcasted_iota(jnp.int32, (rows, LANES), 0)
        gate = jnp.where(lane < n_past, gate_sc[...], NEG_INF)
        chosen = lane == n_past
        for _ in range(min(MOBA_TOPK, n_past)):
            best = jnp.max(gate, axis=1, keepdims=True)
            idx = jnp.min(jnp.where(gate == best, lane, 2 * LANES), axis=1, keepdims=True)
            hit = lane == idx
            chosen = jnp.logical_or(chosen, hit)
            gate = jnp.where(hit, -3e38, gate)
        chosen = jnp.logical_or(chosen, jnp.logical_and(row >= rows_a, lane <= n_past))
        m_fin = jnp.max(jnp.where(chosen, m_sc[...], NEG_INF), axis=1, keepdims=True)
        wgt = jnp.where(chosen, jnp.exp2(m_sc[...] - m_fin), 0.0)
        l_fin = jnp.sum(wgt * l_sc[...], axis=1, keepdims=True)
        o = wgt[:, 0:1] * o_sc[0]
        for n in range(1, n_past + 1):
            o = o + wgt[:, n:n + 1] * o_sc[n]
        o = o / l_fin

        col = lax.broadcasted_iota(jnp.int32, (n_new, w_a), 1)
        ya = jnp.zeros((n_new, w_a), F32)
        for h in range(h_a):
            ya = ya + jnp.where(col // DH == h, o[h * n_new:(h + 1) * n_new], 0.0)
        ya_ref[...] = ya
        lam = lam_ref[0]
        dv = w_a // h_b
        outs = []
        for h in range(h_b):
            r0 = rows_a + 2 * h * n_new
            d = (o[r0:r0 + n_new, h * dv:(h + 1) * dv]
                 - lam * o[r0 + n_new:r0 + 2 * n_new, h * dv:(h + 1) * dv])
            rms = lax.rsqrt(jnp.mean(d * d, axis=1, keepdims=True) + RMS_EPS)
            outs.append(d * rms * g_ref[...] * out_scale)
        yb_ref[...] = jnp.concatenate(outs, axis=1)


def _sample_attn(page_table, lam, hq, ka, va, kb, vb, caches, layer, bias_last, bias_new, subln_g,
                 *, n_new, h_a, h_b, lam_init, blocks_per_step):
    dec_batch, n_pages = page_table.shape
    page = caches[0].shape[2]
    assert 2 * page == MOBA_BLOCK and n_pages % (2 * blocks_per_step) == 0
    w_a = h_a * DH
    rows = (h_a + 2 * h_b) * n_new
    n_steps = n_pages // (2 * blocks_per_step)
    pps = 2 * blocks_per_step
    n_blocks = n_pages // 2 + 1
    caches = [caches[0].transpose(0, 1, 3, 4, 2), caches[1].transpose(0, 1, 3, 4, 2),
              caches[2].reshape(caches[2].shape[:2] + (page * h_b, -1)),
              caches[3].reshape(caches[3].shape[:2] + (page * h_b, -1))]

    def page_spec(v, c):
        shape = (None, None) + c.shape[2:]
        zeros = (0,) * (c.ndim - 2)
        return pl.BlockSpec(shape, lambda b, g, pt: (layer, pt[b * n_pages + g * pps + v]) + zeros)

    new_spec = pl.BlockSpec((n_new, w_a), lambda b, g, pt: (b, 0))
    in_specs = [
        pl.BlockSpec(memory_space=pltpu.SMEM),
        pl.BlockSpec((n_new, hq.shape[1]), lambda b, g, pt: (b, 0)),
        new_spec, new_spec, new_spec, new_spec,
        pl.BlockSpec((rows, MOBA_BLOCK), lambda b, g, pt: (0, 0)),
        pl.BlockSpec((rows, LANES), lambda b, g, pt: (0, 0)),
        pl.BlockSpec((1, w_a // h_b), lambda b, g, pt: (0, 0)),
    ]
    operands = [lam, hq, ka, va, kb, vb, bias_last, bias_new, subln_g]
    for v in range(pps):
        for c in caches:
            in_specs.append(page_spec(v, c))
            operands.append(c)
    grid_spec = pltpu.PrefetchScalarGridSpec(
        num_scalar_prefetch=1,
        grid=(dec_batch, n_steps),
        in_specs=in_specs,
        out_specs=[new_spec, new_spec],
        scratch_shapes=[
            pltpu.VMEM((rows, 2 * w_a), BF16),
            pltpu.VMEM((rows, LANES), F32), pltpu.VMEM((rows, LANES), F32), pltpu.VMEM((rows, LANES), F32),
            pltpu.VMEM((n_blocks, rows, w_a), F32),
        ],
    )
    return pl.pallas_call(
        functools.partial(_sample_attn_kernel, n_steps=n_steps, blocks_per_step=blocks_per_step, n_new=n_new,
                          h_a=h_a, h_b=h_b, page=page, out_scale=1.0 - lam_init),
        grid_spec=grid_spec,
        out_shape=[jax.ShapeDtypeStruct((dec_batch * n_new, w_a), F32)] * 2,
        compiler_params=_cparams(("parallel", "arbitrary")),
    )(page_table.reshape(-1), *operands)


def _layernorm(x, g, b):
    mu = jnp.mean(x, axis=-1, keepdims=True)
    xc = x - mu
    var = jnp.mean(xc * xc, axis=-1, keepdims=True)
    return xc * lax.rsqrt(var + LN_EPS) * g + b


def _mix_kernel(x_ref, ya_ref, yb_ref, wg_ref, bg_ref, wa_ref, wb_ref, wo_ref, g_ref, b_ref, o_ref, *, alpha):
    x = x_ref[...]
    d = x.shape[1]
    gates = jax.nn.sigmoid(_dot(x.astype(BF16), wg_ref[...]) + bg_ref[...])
    z = (gates[:, :d] * _dot(ya_ref[...].astype(BF16), wa_ref[...])
         + gates[:, d:] * _dot(yb_ref[...].astype(BF16), wb_ref[...]))
    mix = _dot(z.astype(BF16), wo_ref[...])
    o_ref[...] = _layernorm(alpha * x + mix, g_ref[...], b_ref[...])


def _mix(x2, ya, yb, wg, bg, wa, wb, wo, g, b, *, tm, alpha):
    m, d = x2.shape
    w = ya.shape[1]
    row = lambda c: pl.BlockSpec((tm, c), lambda i: (i, 0))
    return pl.pallas_call(
        functools.partial(_mix_kernel, alpha=alpha),
        grid=(m // tm,),
        in_specs=[row(d), row(w), row(w), _resident(wg.shape), _resident(bg.shape), _resident(wa.shape),
                  _resident(wb.shape), _resident(wo.shape), _resident(g.shape), _resident(b.shape)],
        out_specs=row(d),
        out_shape=jax.ShapeDtypeStruct((m, d), F32),
        compiler_params=_cparams(("parallel",)),
    )(x2, ya, yb, wg, bg, wa, wb, wo, g, b)


def _gelu_tanh(x):
    return 0.5 * x * (1.0 + jnp.tanh(math.sqrt(2.0 / math.pi) * (x + 0.044715 * (x * x * x))))


def _ffn_kernel(x_ref, p_ref, prev_ref, wu_ref, cw_ref, cb_ref, wd_ref, g_ref, b_ref, wpg_ref, wp_ref,
                o_ref, st_ref, carry_sc, *, alpha, seq_rows, n_chunks):
    x = x_ref[...]
    tm, d = x.shape
    xb = x.astype(BF16)
    long_seq = seq_rows is None
    if long_seq:
        first = pl.program_id(1) == 0
        row = lax.broadcasted_iota(jnp.int32, (tm, FFN_CHUNK), 0)
    else:
        groups = tm // seq_rows
        sub = lax.broadcasted_iota(jnp.int32, (groups, seq_rows, FFN_CHUNK), 1)

    def conv(c, u):
        cw = cw_ref[c]
        if long_seq:
            @pl.when(first)
            def _():
                carry_sc[c] = prev_ref[0, :, c * FFN_CHUNK:(c + 1) * FFN_CHUNK]

            hist = carry_sc[c]
            u1 = jnp.where(row == 0, hist[1:2], pltpu.roll(u, 1, 0))
            u2 = jnp.where(row == 0, hist[0:1], jnp.where(row == 1, hist[1:2], pltpu.roll(u, 2, 0)))
            carry_sc[c] = u[tm - 2:tm]
            st_ref[0, :, c * FFN_CHUNK:(c + 1) * FFN_CHUNK] = u[tm - 2:tm]
            return cw[0:1] * u2 + cw[1:2] * u1 + cw[2:3] * u + cb_ref[c]
        u3 = u.reshape(groups, seq_rows, FFN_CHUNK)
        hist = prev_ref[:, :, c * FFN_CHUNK:(c + 1) * FFN_CHUNK]
        back = lambda k: pltpu.roll(u, k, 0).reshape(groups, seq_rows, FFN_CHUNK)
        u1 = jnp.where(sub == 0, hist[:, 1:2], back(1))
        u2 = jnp.where(sub == 0, hist[:, 0:1], jnp.where(sub == 1, hist[:, 1:2], back(2)))
        st_ref[:, :, c * FFN_CHUNK:(c + 1) * FFN_CHUNK] = u3[:, seq_rows - 2:seq_rows]
        out = cw[0:1] * u2 + cw[1:2] * u1 + cw[2:3] * u3 + cb_ref[c]
        return out.reshape(tm, FFN_CHUNK)

    f = jnp.zeros((tm, d), F32)
    for c in range(n_chunks):
        c_gate = conv(c, _dot(xb, wu_ref[c]))
        c_val = conv(n_chunks + c, _dot(xb, wu_ref[n_chunks + c]))
        f = f + _dot((_gelu_tanh(c_gate) * c_val).astype(BF16), wd_ref[c])
    y = _layernorm(alpha * x + f, g_ref[...], b_ref[...])
    gate = jax.nn.sigmoid(_dot(y.astype(BF16), wpg_ref[...]))
    o_ref[...] = y + gate * _dot(p_ref[...].astype(BF16), wp_ref[...])


def _ffn(x2, p2, prev, wu, cw, cb, wd, g, b, wpg, wp, *, tm, alpha, batch, seq_rows):
    m, d = x2.shape
    d_up = prev.shape[-1]
    n_chunks = wd.shape[0]
    d_ple = p2.shape[1]
    if seq_rows is None:
        tiles = m // batch // tm
        grid = (batch, tiles)
        row = lambda c: pl.BlockSpec((tm, c), lambda bb, i: (bb * tiles + i, 0))
        st_spec = pl.BlockSpec((1, CONV_W - 1, d_up), lambda bb, i: (bb, 0, 0))
        sem = ("parallel", "arbitrary")
    else:
        grid = (1, m // tm)
        row = lambda c: pl.BlockSpec((tm, c), lambda bb, i: (i, 0))
        st_spec = pl.BlockSpec((tm // seq_rows, CONV_W - 1, d_up), lambda bb, i: (i, 0, 0))
        sem = ("arbitrary", "arbitrary")
    n_seq = prev.shape[0]
    return pl.pallas_call(
        functools.partial(_ffn_kernel, alpha=alpha, seq_rows=seq_rows, n_chunks=n_chunks),
        grid=grid,
        in_specs=[row(d), row(d_ple), st_spec, _resident(wu.shape), _resident(cw.shape), _resident(cb.shape),
                  _resident(wd.shape), _resident(g.shape), _resident(b.shape), _resident(wpg.shape),
                  _resident(wp.shape)],
        out_specs=[row(d), st_spec],
        out_shape=[jax.ShapeDtypeStruct((m, d), F32), jax.ShapeDtypeStruct((n_seq, CONV_W - 1, d_up), F32)],
        scratch_shapes=[pltpu.VMEM((2 * n_chunks, CONV_W - 1, FFN_CHUNK), F32)],
        compiler_params=_cparams(sem),
    )(x2, p2, prev, wu, cw, cb, wd, g, b, wpg, wp)


def _layer_weights(l, w):
    d_ff = w['w_down'].shape[1]
    n_chunks = d_ff // FFN_CHUNK
    d = w['w_up'].shape[1]
    chunked = lambda a: a.reshape(a.shape[0], 2 * n_chunks, FFN_CHUNK).transpose(1, 0, 2)
    return {
        'w_in': w['w_in'][l].astype(BF16),
        'w_gate': w['w_gate'][l].astype(BF16),
        'b_gate': w['b_gate'][l][None, :],
        'w_branch_a': w['w_branch_a'][l].astype(BF16),
        'w_branch_b': w['w_branch_b'][l].astype(BF16),
        'w_out': w['w_out'][l].astype(BF16),
        'ln1_g': w['ln1_g'][l][None, :], 'ln1_b': w['ln1_b'][l][None, :],
        'w_up': chunked(w['w_up'][l]).astype(BF16),
        'conv_w': chunked(w['conv_w'][l]),
        'conv_b': chunked(w['conv_b'][l][None, :]),
        'w_down': w['w_down'][l].reshape(n_chunks, FFN_CHUNK, d).astype(BF16),
        'ln2_g': w['ln2_g'][l][None, :], 'ln2_b': w['ln2_b'][l][None, :],
        'w_ple_gate': w['w_ple_gate'][l].astype(BF16),
        'w_ple': w['w_ple'][l].astype(BF16),
        'subln_g': w['diff_subln_g'][l][None, :],
        'lam': (jnp.exp(jnp.sum(w['diff_lq1'][l] * w['diff_lk1'][l]))
                - jnp.exp(jnp.sum(w['diff_lq2'][l] * w['diff_lk2'][l]))
                + (0.8 - 0.6 * math.exp(-0.3 * l))).reshape(1).astype(F32),
    }


def _tail(x2, ya, yb, p2, prev, wl, *, alpha, tm_mix, tm_ffn, batch, seq_rows):
    x1 = _mix(x2, ya, yb, wl['w_gate'], wl['b_gate'], wl['w_branch_a'], wl['w_branch_b'], wl['w_out'],
              wl['ln1_g'], wl['ln1_b'], tm=tm_mix, alpha=alpha)
    return _ffn(x1, p2, prev, wl['w_up'], wl['conv_w'], wl['conv_b'], wl['w_down'], wl['ln2_g'], wl['ln2_b'],
                wl['w_ple_gate'], wl['w_ple'], tm=tm_ffn, alpha=alpha, batch=batch, seq_rows=seq_rows)


def _forward(x_prompt, x_sample, cache_a_k, cache_a_v, cache_b_k, cache_b_v, state_conv, page_table,
             p_prompt, p_sample, w, *, tm_prompt=256, tm_sample=256, blocks_per_step=4):
    depth = w['w_in'].shape[0]
    batch, seq, d = x_prompt.shape
    dec_batch, n_new, _ = x_sample.shape
    h_a, dh = cache_a_k.shape[3], cache_a_k.shape[4]
    h_b, dv = cache_b_v.shape[3], cache_b_v.shape[4]
    assert dh == DH and cache_b_k.shape[4] == 2 * DH and dv == LANES and seq % ATT_TILE == 0
    d_up = state_conv.shape[-1]
    past_len = page_table.shape[1] * cache_a_k.shape[2]
    alpha = (2 * depth) ** 0.25
    tab = w['rel_bias'].T
    bias_prompt = _prompt_bias_tiles(tab)
    bias_last, bias_new = _sample_bias_tiles(tab, h_a, n_new, past_len)
    caches = (cache_a_k, cache_a_v, cache_b_k, cache_b_v)

    xp = x_prompt.reshape(batch * seq, d)
    xs = x_sample.reshape(dec_batch * n_new, d)
    conv0 = jnp.zeros((batch, CONV_W - 1, d_up), F32)
    rows_p, rows_s = [], []
    for l in range(depth):
        wl = _layer_weights(l, w)
        lam_init = 0.8 - 0.6 * math.exp(-0.3 * l)

        hq, kat, vat, kb, vb, vat16, vbt16, kmean = _proj_prompt(xp, wl['w_in'], batch=batch, seq=seq, h_b=h_b)
        nb = seq // MOBA_BLOCK
        kmean_pad = jnp.pad(kmean.reshape(batch, nb, h_a * DH), ((0, 0), (0, LANES - nb), (0, 0)))
        ya = _moba_prompt(hq, vat16, kmean_pad, bias_prompt[:h_a], batch=batch, seq=seq, h_a=h_a)
        yb = _diff_prompt(wl['lam'], hq, vbt16, bias_prompt[h_a:], wl['subln_g'], batch=batch, seq=seq, h_b=h_b,
                          lam_init=lam_init)
        xp, conv_p = _tail(xp, ya, yb, p_prompt[l].reshape(batch * seq, -1), conv0, wl, alpha=alpha,
                           tm_mix=tm_prompt, tm_ffn=tm_prompt, batch=batch, seq_rows=None)
        token_major = lambda a: a.reshape(batch, h_a, DH, seq).transpose(0, 3, 1, 2)
        rows_p.append((token_major(kat), token_major(vat), kb.reshape(batch, seq, h_b, 2 * DH),
                       vb.reshape(batch, seq, h_b, dv), conv_p))

        hq, ka, va, kb, vb = _proj_sample(xs, wl['w_in'], tm=tm_sample)
        ya, yb = _sample_attn(page_table, wl['lam'], hq, ka, va, kb, vb, caches, l, bias_last, bias_new,
                              wl['subln_g'], n_new=n_new, h_a=h_a, h_b=h_b, lam_init=lam_init,
                              blocks_per_step=blocks_per_step)
        xs, conv_s = _tail(xs, ya, yb, p_sample[l].reshape(dec_batch * n_new, -1), state_conv[l], wl,
                           alpha=alpha, tm_mix=tm_sample, tm_ffn=tm_sample, batch=1, seq_rows=n_new)
        rows_s.append((ka.reshape(dec_batch, n_new, h_a, DH), va.reshape(dec_batch, n_new, h_a, DH),
                       kb.reshape(dec_batch, n_new, h_b, 2 * DH), vb.reshape(dec_batch, n_new, h_b, dv), conv_s))

    stack = lambda rows, i: jnp.stack([r[i] for r in rows])
    return ((xp.reshape(batch, seq, d), xs.reshape(dec_batch, n_new, d))
            + tuple(stack(rows_p, i) for i in range(5)) + tuple(stack(rows_s, i) for i in range(5)))


def kernel(x_prompt, x_sample, cache_a_k, cache_a_v, cache_b_k, cache_b_v, state_conv, page_table, p_prompt, p_sample, w_in, w_gate, b_gate, w_branch_a, w_branch_b, w_out, rel_bias, diff_lq1, diff_lk1, diff_lq2, diff_lk2, diff_subln_g, ln1_g, ln1_b, w_up, conv_w, conv_b, w_down, ln2_g, ln2_b, w_ple, w_ple_gate):
    w = {'w_in': w_in, 'w_gate': w_gate, 'b_gate': b_gate, 'w_branch_a': w_branch_a,
         'w_branch_b': w_branch_b, 'w_out': w_out, 'rel_bias': rel_bias, 'diff_lq1': diff_lq1,
         'diff_lk1': diff_lk1, 'diff_lq2': diff_lq2, 'diff_lk2': diff_lk2, 'diff_subln_g': diff_subln_g,
         'ln1_g': ln1_g, 'ln1_b': ln1_b, 'w_up': w_up, 'conv_w': conv_w, 'conv_b': conv_b,
         'w_down': w_down, 'ln2_g': ln2_g, 'ln2_b': ln2_b, 'w_ple': w_ple, 'w_ple_gate': w_ple_gate}
    return _forward(x_prompt, x_sample, cache_a_k, cache_a_v, cache_b_k, cache_b_v, state_conv, page_table,
                    p_prompt, p_sample, w)
```

```python
import functools
import math

import numpy as np
import jax
import jax.numpy as jnp
from jax import lax
from jax.experimental import pallas as pl
from jax.experimental.pallas import tpu as pltpu

F32 = jnp.float32
BF16 = jnp.bfloat16

DH = 64
MOBA_BLOCK = 256
MOBA_TOPK = 3
N_BUCKETS = 32
MAX_DISTANCE = 128
CONV_W = 3
LN_EPS = 1e-5
RMS_EPS = 1e-5
NEG_INF = -1e30
LOG2E = math.log2(math.e)
LANES = 128
BF16_ROWS = 16
VMEM_LIMIT = 56 * 1024 * 1024

ATT_TILE = 2 * MOBA_BLOCK
BLOCKS_PER_TILE = ATT_TILE // MOBA_BLOCK
FFN_CHUNK = 256
FFN_ROWS = 64
CONV_HIST = 8


def _cparams(sem):
    return pltpu.CompilerParams(dimension_semantics=sem, vmem_limit_bytes=VMEM_LIMIT)


def _dot(a, b):
    return jnp.dot(a, b, preferred_element_type=F32)


def _dot_nt(a, b):
    return lax.dot_general(a, b, (((1,), (1,)), ((), ())), preferred_element_type=F32)


def _resident(shape):
    return pl.BlockSpec(shape, lambda *_: (0,) * len(shape), pipeline_mode=pl.Buffered(1))


def _t5_bucket_np(dist):
    dist = np.maximum(dist, 0)
    exact = N_BUCKETS // 2
    logv = (np.log(np.maximum(dist, 1).astype(np.float32) / np.float32(exact))
            / np.float32(math.log(MAX_DISTANCE / exact))).astype(np.float32)
    large = np.minimum(exact + (logv * np.float32(N_BUCKETS - exact)).astype(np.int32), N_BUCKETS - 1)
    return np.where(dist < exact, dist, large).astype(np.int32)


def _bias_lookup(tab_rows, buckets, per_row=False):
    shape = buckets.shape if per_row else (tab_rows.shape[0],) + buckets.shape
    out = jnp.zeros(shape, F32)
    for n in np.unique(buckets):
        col = tab_rows[:, n].reshape((-1,) + (1,) * (len(shape) - 1))
        out = jnp.where(jnp.asarray(buckets == n), col, out)
    return out


def _prompt_bias_tiles(tab):
    t = ATT_TILE
    j = np.arange(t)[:, None]
    i = np.arange(t)[None, :]
    far = tab[:, N_BUCKETS - 1][:, None, None]
    diag = jnp.where(jnp.asarray(i >= j)[None], (_bias_lookup(tab, _t5_bucket_np(i - j)) - far) * LOG2E, NEG_INF)
    prev = (_bias_lookup(tab, _t5_bucket_np(t + i - j)) - far) * LOG2E
    return jnp.stack([diag, prev], axis=1).astype(F32)


def _sample_bias_tiles(tab, h_a, n_new, past_len):
    h_b = tab.shape[0] - h_a
    rows = (h_a + 2 * h_b) * n_new
    r = np.arange(rows)
    qi = (r % n_new)[:, None]
    head = np.where(r < h_a * n_new, r // n_new, h_a + (r - h_a * n_new) // (2 * n_new))
    onehot = jnp.asarray(head[:, None] == np.arange(tab.shape[0])[None, :], F32)
    tab_rows = jnp.dot(onehot, tab, precision=lax.Precision.HIGHEST)
    far = tab_rows[:, N_BUCKETS - 1:]
    t = np.arange(MOBA_BLOCK)[None, :]
    dist_last = past_len + qi - (past_len - MOBA_BLOCK + t)
    last = (_bias_lookup(tab_rows, _t5_bucket_np(dist_last), per_row=True) - far) * LOG2E
    j = np.arange(LANES)[None, :]
    new = jnp.where(jnp.asarray((j <= qi) & (j < n_new)),
                    (_bias_lookup(tab_rows, _t5_bucket_np(qi - j), per_row=True) - far) * LOG2E, NEG_INF)
    return last.astype(F32), new.astype(F32)


def _proj_prompt_kernel(x_ref, w_ref, hq_ref, kat_ref, vat_ref, kb_ref, vb_ref, vat16_ref, vbt16_ref, kmean_ref,
                        *, seg, q_scale, h_b):
    x = x_ref[...].astype(BF16)
    dv = seg // h_b
    for s in range(6):
        h = _dot(x, w_ref[:, s * seg:(s + 1) * seg])
        if s == 0:
            hq_ref[:, 0:seg] = (h * q_scale).astype(BF16)
        elif s == 1:
            hq_ref[:, seg:2 * seg] = h.astype(BF16)
            for n in range(BLOCKS_PER_TILE):
                blk = h[n * MOBA_BLOCK:(n + 1) * MOBA_BLOCK]
                kmean_ref[n] = jnp.sum(blk, axis=0, keepdims=True) * (1.0 / MOBA_BLOCK)
            kat_ref[...] = h.T
        elif s == 2:
            ht = h.T
            vat_ref[...] = ht
            vat16_ref[...] = ht.astype(BF16)
        elif s == 3:
            hq_ref[:, 2 * seg:3 * seg] = (h * q_scale).astype(BF16)
        elif s == 4:
            hq_ref[:, 3 * seg:4 * seg] = h.astype(BF16)
            for hh in range(h_b):
                kb_ref[:, hh, :] = h[:, hh * dv:(hh + 1) * dv]
        else:
            for hh in range(h_b):
                vb_ref[:, hh, :] = h[:, hh * dv:(hh + 1) * dv]
            vbt16_ref[...] = h.T.astype(BF16)


def _proj_prompt(x2, w_in_l, *, batch, seq, h_b):
    tm = ATT_TILE
    m, d = x2.shape
    seg = w_in_l.shape[1] // 6
    nt = seq // tm
    dv = seg // h_b
    row = lambda w: pl.BlockSpec((tm, w), lambda b, i: (b * nt + i, 0))
    feat = pl.BlockSpec((None, seg, tm), lambda b, i: (b, 0, i))
    heads = pl.BlockSpec((tm, h_b, dv), lambda b, i: (b * nt + i, 0, 0))
    tiles = pl.BlockSpec((None, None, seg, tm), lambda b, i: (b, i, 0, 0))
    return pl.pallas_call(
        functools.partial(_proj_prompt_kernel, seg=seg, q_scale=DH ** -0.5 * LOG2E, h_b=h_b),
        grid=(batch, nt),
        in_specs=[row(d), _resident(w_in_l.shape)],
        out_specs=[row(4 * seg), feat, feat, heads, heads, tiles, tiles,
                   pl.BlockSpec((BLOCKS_PER_TILE, 1, seg), lambda b, i: (b * nt + i, 0, 0))],
        out_shape=[jax.ShapeDtypeStruct((m, 4 * seg), BF16),
                   jax.ShapeDtypeStruct((batch, seg, seq), F32), jax.ShapeDtypeStruct((batch, seg, seq), F32),
                   jax.ShapeDtypeStruct((m, h_b, dv), F32), jax.ShapeDtypeStruct((m, h_b, dv), F32),
                   jax.ShapeDtypeStruct((batch, nt, seg, tm), BF16), jax.ShapeDtypeStruct((batch, nt, seg, tm), BF16),
                   jax.ShapeDtypeStruct((m // MOBA_BLOCK, 1, seg), F32)],
        compiler_params=_cparams(("parallel", "parallel")),
    )(x2, w_in_l)


def _proj_sample_kernel(x_ref, w_ref, hq_ref, ka_ref, va_ref, kb_ref, vb_ref, *, seg, q_scale):
    x = x_ref[...].astype(BF16)
    outs = (None, ka_ref, va_ref, None, kb_ref, vb_ref)
    for s in range(6):
        h = _dot(x, w_ref[:, s * seg:(s + 1) * seg])
        if s == 0:
            hq_ref[:, 0:seg] = h * q_scale
        elif s == 3:
            hq_ref[:, seg:2 * seg] = h * q_scale
        else:
            outs[s][...] = h


def _proj_sample(x2, w_in_l, *, tm):
    m, d = x2.shape
    seg = w_in_l.shape[1] // 6
    row = lambda w: pl.BlockSpec((tm, w), lambda i: (i, 0))
    return pl.pallas_call(
        functools.partial(_proj_sample_kernel, seg=seg, q_scale=DH ** -0.5 * LOG2E),
        grid=(m // tm,),
        in_specs=[row(d), _resident(w_in_l.shape)],
        out_specs=[row(2 * seg)] + [row(seg)] * 4,
        out_shape=[jax.ShapeDtypeStruct((m, 2 * seg), F32)] + [jax.ShapeDtypeStruct((m, seg), F32)] * 4,
        compiler_params=_cparams(("parallel",)),
    )(x2, w_in_l)


SOFTMAX_ROWS = 64


def _flash_consume(s_ref, p_ref, bias_ref, bias_shift, vt_ext, m_sc, acc_sc, h, first):
    t_k, t_q = s_ref.shape
    sub = 8

    def chunk(c):
        rows = slice(c * SOFTMAX_ROWS, (c + 1) * SOFTMAX_ROWS)
        s = s_ref[rows, :]
        if bias_ref is not None:
            s = s + bias_ref[rows, :]
        return s if bias_shift is None else s + bias_shift

    m8 = None
    for c in range(t_k // SOFTMAX_ROWS):
        part = jnp.max(chunk(c).reshape(SOFTMAX_ROWS // sub, sub, t_q), axis=0)
        m8 = part if m8 is None else jnp.maximum(m8, part)
    m_tile = jnp.max(m8, axis=0, keepdims=True)
    if first:
        m_new = m_tile
    else:
        m_old = m_sc[h]
        m_new = jnp.maximum(m_old, m_tile)
    for c in range(t_k // SOFTMAX_ROWS):
        p_ref[c * SOFTMAX_ROWS:(c + 1) * SOFTMAX_ROWS, :] = jnp.exp2(chunk(c) - m_new).astype(BF16)
    pv = _dot(vt_ext, p_ref[...])
    if first:
        acc_sc[h] = pv
    else:
        acc_sc[h] = jnp.exp2(m_old - m_new) * acc_sc[h] + pv
    m_sc[h] = m_new


def _score_buffers(t):
    return [pltpu.VMEM((2, t, t), F32)] * 2 + [pltpu.VMEM((2, t, t), BF16)] * 2


def _flash_sweep(qi, scores, consume):
    near = jnp.maximum(qi - 1, 0)
    n_far = near
    last = jnp.maximum(n_far - 1, 0)
    scores(qi, 0)
    scores(near, 1)
    consume(qi, 0, 0, None, True)
    scores(0, 0)
    consume(near, 1, 1, jnp.where(qi == 0, NEG_INF, 0.0), False)

    def pair(p, carry):
        j = 2 * p
        scores(jnp.minimum(j + 1, last), 1)
        consume(j, 0, None, None, False)
        scores(jnp.minimum(j + 2, last), 0)
        consume(j + 1, 1, None, None, False)
        return carry

    lax.fori_loop(0, n_far // 2, pair, 0)

    @pl.when(n_far % 2 == 1)
    def _():
        consume(last, 0, None, None, False)


def _moba_prompt_kernel(q_ref, k_ref, vt_ref, kmean_ref, bias_ref, o_ref, m_sc, acc_sc, s0_sc, s1_sc, p0_sc, p1_sc):
    t = ATT_TILE
    qi = pl.program_id(2)
    qt = q_ref[...].astype(F32).T
    row = lax.broadcasted_iota(jnp.int32, (LANES, t), 0)
    head0 = row < DH
    q_heads = (jnp.where(head0, qt, 0.0).astype(BF16), jnp.where(head0, 0.0, qt).astype(BF16))
    kmean = kmean_ref[...].astype(BF16)
    cur = qi * BLOCKS_PER_TILE + lax.broadcasted_iota(jnp.int32, (LANES, t), 1) // MOBA_BLOCK

    def block_mask(qh):
        gate = jnp.where(row < cur, _dot(kmean, qh), NEG_INF)
        chosen = row == cur
        for j in range(MOBA_TOPK):
            best = jnp.max(gate, axis=0, keepdims=True)
            idx = jnp.min(jnp.where(gate == best, row, 2 * LANES), axis=0, keepdims=True)
            hit = row == idx
            chosen = jnp.logical_or(chosen, jnp.logical_and(hit, cur > j))
            gate = jnp.where(hit, -3e38, gate)
        return jnp.where(chosen, 0.0, NEG_INF).astype(BF16)

    q_ext = tuple(jnp.concatenate([qh, block_mask(qh)], axis=0) for qh in q_heads)
    lane = lax.broadcasted_iota(jnp.int32, (t, LANES), 1)
    key_block = lax.broadcasted_iota(jnp.int32, (t, LANES), 0) // MOBA_BLOCK
    ones = jnp.ones((BF16_ROWS, t), BF16)

    s_bufs, p_bufs = (s0_sc, s1_sc), (p0_sc, p1_sc)

    def scores(kj, slot):
        rows = pl.ds(pl.multiple_of(kj * t, t), t)
        onehot = jnp.where(lane == kj * BLOCKS_PER_TILE + key_block, 1.0, 0.0).astype(BF16)
        k_ext = jnp.concatenate([k_ref[rows, :], onehot], axis=1)
        for h in range(2):
            s_bufs[slot][h] = _dot(k_ext, q_ext[h])

    def consume(kj, slot, bias_idx, bias_shift, first):
        vt = vt_ref[kj]
        for h in range(2):
            vt_ext = jnp.concatenate([vt[h * DH:(h + 1) * DH], ones], axis=0)
            bias = None if bias_idx is None else bias_ref.at[h, bias_idx]
            _flash_consume(s_bufs[slot].at[h], p_bufs[slot].at[h], bias, bias_shift, vt_ext, m_sc, acc_sc, h,
                           first)

    _flash_sweep(qi, scores, consume)
    out = [acc_sc[h, 0:DH] / acc_sc[h, DH:DH + 1] for h in range(2)]
    o_ref[...] = jnp.concatenate(out, axis=0).T.astype(o_ref.dtype)


def _moba_prompt(hq, vat16, kmean_pad, bias_a, *, batch, seq, h_a):
    t = ATT_TILE
    pairs = h_a * DH // LANES
    seg_blocks = hq.shape[1] // 4 // LANES
    nq = seq // t
    return pl.pallas_call(
        _moba_prompt_kernel,
        grid=(batch, pairs, nq),
        in_specs=[
            pl.BlockSpec((t, LANES), lambda b, j, i: (b * nq + i, j)),
            pl.BlockSpec((seq, LANES), lambda b, j, i: (b, seg_blocks + j)),
            pl.BlockSpec((None, nq, LANES, t), lambda b, j, i: (b, 0, j, 0)),
            pl.BlockSpec((None, LANES, LANES), lambda b, j, i: (b, 0, j)),
            pl.BlockSpec((2, 2, t, t), lambda b, j, i: (j, 0, 0, 0)),
        ],
        out_specs=pl.BlockSpec((t, LANES), lambda b, j, i: (b * nq + i, j)),
        out_shape=jax.ShapeDtypeStruct((batch * seq, pairs * LANES), BF16),
        scratch_shapes=[pltpu.VMEM((2, 1, t), F32), pltpu.VMEM((2, DH + BF16_ROWS, t), F32)] + _score_buffers(t),
        compiler_params=_cparams(("parallel", "parallel", "arbitrary")),
    )(hq, hq, vat16, kmean_pad, bias_a)


def _diff_prompt_kernel(lam_ref, q_ref, k_ref, vt_ref, bias_ref, g_ref, o_ref, m_sc, acc_sc,
                        s0_sc, s1_sc, p0_sc, p1_sc, *, out_scale):
    t = ATT_TILE
    dv = LANES
    qi = pl.program_id(2)
    qt = q_ref[...].astype(F32).T
    row = lax.broadcasted_iota(jnp.int32, (LANES, t), 0)
    half0 = row < DH
    q_halves = (jnp.where(half0, qt, 0.0).astype(BF16), jnp.where(half0, 0.0, qt).astype(BF16))
    ones = jnp.ones((BF16_ROWS, t), BF16)

    s_bufs, p_bufs = (s0_sc, s1_sc), (p0_sc, p1_sc)

    def scores(kj, slot):
        k = k_ref[pl.ds(pl.multiple_of(kj * t, t), t), :]
        for h in range(2):
            s_bufs[slot][h] = _dot(k, q_halves[h])

    def consume(kj, slot, bias_idx, bias_shift, first):
        vt_ext = jnp.concatenate([vt_ref[kj], ones], axis=0)
        bias = None if bias_idx is None else bias_ref.at[bias_idx]
        for h in range(2):
            _flash_consume(s_bufs[slot].at[h], p_bufs[slot].at[h], bias, bias_shift, vt_ext, m_sc, acc_sc, h,
                           first)

    _flash_sweep(qi, scores, consume)
    o = (acc_sc[0, 0:dv] / acc_sc[0, dv:dv + 1]
         - lam_ref[0] * (acc_sc[1, 0:dv] / acc_sc[1, dv:dv + 1]))
    rms = lax.rsqrt(jnp.mean(o * o, axis=0, keepdims=True) + RMS_EPS)
    o_ref[...] = (o * rms * g_ref[...] * out_scale).T.astype(o_ref.dtype)


def _diff_prompt(lam, hq, vbt16, bias_b, subln_g, *, batch, seq, h_b, lam_init):
    t = ATT_TILE
    seg_blocks = hq.shape[1] // 4 // LANES
    nq = seq // t
    g_spread = jnp.broadcast_to(subln_g.reshape(LANES, 1), (LANES, t))
    return pl.pallas_call(
        functools.partial(_diff_prompt_kernel, out_scale=1.0 - lam_init),
        grid=(batch, h_b, nq),
        in_specs=[
            pl.BlockSpec(memory_space=pltpu.SMEM),
            pl.BlockSpec((t, LANES), lambda b, h, i: (b * nq + i, 2 * seg_blocks + h)),
            pl.BlockSpec((seq, LANES), lambda b, h, i: (b, 3 * seg_blocks + h)),
            pl.BlockSpec((None, nq, LANES, t), lambda b, h, i: (b, 0, h, 0)),
            pl.BlockSpec((None, 2, t, t), lambda b, h, i: (h, 0, 0, 0)),
            pl.BlockSpec((LANES, t), lambda b, h, i: (0, 0)),
        ],
        out_specs=pl.BlockSpec((t, LANES), lambda b, h, i: (b * nq + i, h)),
        out_shape=jax.ShapeDtypeStruct((batch * seq, h_b * LANES), BF16),
        scratch_shapes=[pltpu.VMEM((2, 1, t), F32), pltpu.VMEM((2, LANES + BF16_ROWS, t), F32)] + _score_buffers(t),
        compiler_params=_cparams(("parallel", "parallel", "arbitrary")),
    )(lam, hq, hq, vbt16, bias_b, g_spread)


def _softmax_tile(s):
    m = jnp.max(s, axis=1, keepdims=True)
    p = jnp.exp2(s - m)
    return m, p, jnp.sum(p, axis=1, keepdims=True)


def _sample_attn_kernel(pt_ref, lam_ref, hq_ref, ka_new_ref, va_new_ref, kb_new_ref, vb_new_ref,
                        bias_last_ref, bias_new_ref, g_ref, *refs,
                        n_steps, blocks_per_step, n_new, h_a, h_b, page, out_scale):
    del pt_ref
    n_pages = 2 * blocks_per_step
    page_refs = refs[:4 * n_pages]
    ya_ref, yb_ref = refs[4 * n_pages:4 * n_pages + 2]
    w_sc, gate_sc, m_sc, l_sc, o_sc = refs[4 * n_pages + 2:]
    g = pl.program_id(1)
    w_a = h_a * DH
    rows_a = h_a * n_new
    rows = rows_a + 2 * h_b * n_new
    n_past = n_steps * blocks_per_step
    lane = lax.broadcasted_iota(jnp.int32, (rows, LANES), 1)

    @pl.when(g == 0)
    def _():
        tiled = jnp.concatenate([hq_ref[...]] * (rows // n_new), axis=0)
        r = lax.broadcasted_iota(jnp.int32, tiled.shape, 0)
        c = lax.broadcasted_iota(jnp.int32, tiled.shape, 1)
        w_sc[...] = jnp.where(c // DH == r // n_new, tiled, 0.0).astype(BF16)
        gate_sc[...] = jnp.zeros_like(gate_sc)
        m_sc[...] = jnp.zeros_like(m_sc)
        l_sc[...] = jnp.zeros_like(l_sc)

    def partial_block(n, s, bias, pv_a, pv_b):
        gsum = jnp.sum(s, axis=1, keepdims=True)
        m, p, l = _softmax_tile(s + bias)
        p = p.astype(BF16)
        o_sc[n, 0:rows_a, :] = pv_a(p[0:rows_a])
        o_sc[n, rows_a:rows, :] = pv_b(p[rows_a:rows])
        here = lane == n
        gate_sc[...] = jnp.where(here, gsum, gate_sc[...])
        m_sc[...] = jnp.where(here, m, m_sc[...])
        l_sc[...] = jnp.where(here, l, l_sc[...])

    def heads_on_lanes(ref):
        return jnp.concatenate([ref[pl.ds(h, page, stride=h_b), :] for h in range(h_b)], axis=1).astype(BF16)

    for u in range(blocks_per_step):
        n = g * blocks_per_step + u
        pages = [page_refs[4 * (2 * u + v):4 * (2 * u + v) + 4] for v in range(2)]
        ka_t = [pg[0][...].reshape(w_a, page).astype(BF16) for pg in pages]
        va_t = [pg[1][...].reshape(w_a, page).astype(BF16) for pg in pages]
        kb = [heads_on_lanes(pg[2]) for pg in pages]
        vb = [heads_on_lanes(pg[3]) for pg in pages]
        q_a = w_sc[0:rows_a, 0:w_a]
        q_b = w_sc[rows_a:rows, w_a:2 * w_a]
        s = jnp.concatenate([
            jnp.concatenate([_dot(q_a, k) for k in ka_t], axis=1),
            jnp.concatenate([_dot_nt(q_b, k) for k in kb], axis=1)], axis=0)
        pv_a = lambda p: sum(_dot_nt(p[:, v * page:(v + 1) * page], va_t[v]) for v in range(2))
        pv_b = lambda p: sum(_dot(p[:, v * page:(v + 1) * page], vb[v]) for v in range(2))
        is_last = (n == n_past - 1).astype(F32)
        partial_block(n, s, bias_last_ref[...] * is_last, pv_a, pv_b)

    @pl.when(g == n_steps - 1)
    def _():
        pad = jnp.zeros((LANES - n_new, w_a), F32)
        padded = lambda ref: jnp.concatenate([ref[...], pad], axis=0).astype(BF16)
        k_cat = jnp.concatenate([padded(ka_new_ref), padded(kb_new_ref)], axis=1)
        va, vb = padded(va_new_ref), padded(vb_new_ref)
        partial_block(n_past, _dot_nt(w_sc[...], k_cat), bias_new_ref[...],
                      lambda p: _dot(p, va), lambda p: _dot(p, vb))

        row = lax.broadcasted_iota(jnp.int32, (rows, LANES), 0)
        gate = jnp.where(lane < n_past, gate_sc[...], NEG_INF)
        chosen = lane == n_past
        for _ in range(min(MOBA_TOPK, n_past)):
            best = jnp.max(gate, axis=1, keepdims=True)
            idx = jnp.min(jnp.where(gate == best, lane, 2 * LANES), axis=1, keepdims=True)
            hit = lane == idx
            chosen = jnp.logical_or(chosen, hit)
            gate = jnp.where(hit, -3e38, gate)
        chosen = jnp.logical_or(chosen, jnp.logical_and(row >= rows_a, lane <= n_past))
        m_fin = jnp.max(jnp.where(chosen, m_sc[...], NEG_INF), axis=1, keepdims=True)
        wgt = jnp.where(chosen, jnp.exp2(m_sc[...] - m_fin), 0.0)
        l_fin = jnp.sum(wgt * l_sc[...], axis=1, keepdims=True)
        o = wgt[:, 0:1] * o_sc[0]
        for n in range(1, n_past + 1):
            o = o + wgt[:, n:n + 1] * o_sc[n]
        o = o / l_fin

        col = lax.broadcasted_iota(jnp.int32, (n_new, w_a), 1)
        ya = jnp.zeros((n_new, w_a), F32)
        for h in range(h_a):
            ya = ya + jnp.where(col // DH == h, o[h * n_new:(h + 1) * n_new], 0.0)
        ya_ref[...] = ya
        lam = lam_ref[0]
        dv = w_a // h_b
        outs = []
        for h in range(h_b):
            r0 = rows_a + 2 * h * n_new
            d = (o[r0:r0 + n_new, h * dv:(h + 1) * dv]
                 - lam * o[r0 + n_new:r0 + 2 * n_new, h * dv:(h + 1) * dv])
            rms = lax.rsqrt(jnp.mean(d * d, axis=1, keepdims=True) + RMS_EPS)
            outs.append(d * rms * g_ref[...] * out_scale)
        yb_ref[...] = jnp.concatenate(outs, axis=1)


def _sample_attn(page_table, lam, hq, ka, va, kb, vb, caches, layer, bias_last, bias_new, subln_g,
                 *, n_new, h_a, h_b, lam_init, blocks_per_step):
    dec_batch, n_pages = page_table.shape
    page = caches[0].shape[2]
    assert 2 * page == MOBA_BLOCK and n_pages % (2 * blocks_per_step) == 0
    w_a = h_a * DH
    rows = (h_a + 2 * h_b) * n_new
    n_steps = n_pages // (2 * blocks_per_step)
    pps = 2 * blocks_per_step
    n_blocks = n_pages // 2 + 1
    caches = [caches[0].transpose(0, 1, 3, 4, 2), caches[1].transpose(0, 1, 3, 4, 2),
              caches[2].reshape(caches[2].shape[:2] + (page * h_b, -1)),
              caches[3].reshape(caches[3].shape[:2] + (page * h_b, -1))]

    def page_spec(v, c):
        shape = (None, None) + c.shape[2:]
        zeros = (0,) * (c.ndim - 2)
        return pl.BlockSpec(shape, lambda b, g, pt: (layer, pt[b * n_pages + g * pps + v]) + zeros)

    new_spec = pl.BlockSpec((n_new, w_a), lambda b, g, pt: (b, 0))
    in_specs = [
        pl.BlockSpec(memory_space=pltpu.SMEM),
        pl.BlockSpec((n_new, hq.shape[1]), lambda b, g, pt: (b, 0)),
        new_spec, new_spec, new_spec, new_spec,
        pl.BlockSpec((rows, MOBA_BLOCK), lambda b, g, pt: (0, 0)),
        pl.BlockSpec((rows, LANES), lambda b, g, pt: (0, 0)),
        pl.BlockSpec((1, w_a // h_b), lambda b, g, pt: (0, 0)),
    ]
    operands = [lam, hq, ka, va, kb, vb, bias_last, bias_new, subln_g]
    for v in range(pps):
        for c in caches:
            in_specs.append(page_spec(v, c))
            operands.append(c)
    grid_spec = pltpu.PrefetchScalarGridSpec(
        num_scalar_prefetch=1,
        grid=(dec_batch, n_steps),
        in_specs=in_specs,
        out_specs=[new_spec, new_spec],
        scratch_shapes=[
            pltpu.VMEM((rows, 2 * w_a), BF16),
            pltpu.VMEM((rows, LANES), F32), pltpu.VMEM((rows, LANES), F32), pltpu.VMEM((rows, LANES), F32),
            pltpu.VMEM((n_blocks, rows, w_a), F32),
        ],
    )
    return pl.pallas_call(
        functools.partial(_sample_attn_kernel, n_steps=n_steps, blocks_per_step=blocks_per_step, n_new=n_new,
                          h_a=h_a, h_b=h_b, page=page, out_scale=1.0 - lam_init),
        grid_spec=grid_spec,
        out_shape=[jax.ShapeDtypeStruct((dec_batch * n_new, w_a), F32)] * 2,
        compiler_params=_cparams(("parallel", "arbitrary")),
    )(page_table.reshape(-1), *operands)


def _layernorm(x, g, b):
    mu = jnp.mean(x, axis=-1, keepdims=True)
    xc = x - mu
    var = jnp.mean(xc * xc, axis=-1, keepdims=True)
    return xc * lax.rsqrt(var + LN_EPS) * g + b


def _mix_kernel(x_ref, ya_ref, yb_ref, wg_ref, bg_ref, wa_ref, wb_ref, wo_ref, g_ref, b_ref, o_ref, *, alpha):
    x = x_ref[...]
    d = x.shape[1]
    gates = jax.nn.sigmoid(_dot(x.astype(BF16), wg_ref[...]) + bg_ref[...])
    z = (gates[:, :d] * _dot(ya_ref[...].astype(BF16), wa_ref[...])
         + gates[:, d:] * _dot(yb_ref[...].astype(BF16), wb_ref[...]))
    mix = _dot(z.astype(BF16), wo_ref[...])
    o_ref[...] = _layernorm(alpha * x + mix, g_ref[...], b_ref[...])


def _mix(x2, ya, yb, wg, bg, wa, wb, wo, g, b, *, tm, alpha):
    m, d = x2.shape
    w = ya.shape[1]
    row = lambda c: pl.BlockSpec((tm, c), lambda i: (i, 0))
    return pl.pallas_call(
        functools.partial(_mix_kernel, alpha=alpha),
        grid=(m // tm,),
        in_specs=[row(d), row(w), row(w), _resident(wg.shape), _resident(bg.shape), _resident(wa.shape),
                  _resident(wb.shape), _resident(wo.shape), _resident(g.shape), _resident(b.shape)],
        out_specs=row(d),
        out_shape=jax.ShapeDtypeStruct((m, d), F32),
        compiler_params=_cparams(("parallel",)),
    )(x2, ya, yb, wg, bg, wa, wb, wo, g, b)


def _gelu_tanh(x):
    return 0.5 * x * (1.0 + jnp.tanh(math.sqrt(2.0 / math.pi) * (x + 0.044715 * (x * x * x))))


def _ffn_kernel(x_ref, p_ref, prev_ref, wu_ref, cw_ref, cb_ref, wd_ref, g_ref, b_ref, wpg_ref, wp_ref,
                o_ref, st_ref, carry_sc, act_sc, u_sc, *, alpha, seq_rows, n_chunks):
    x = x_ref[...]
    tm, d = x.shape
    xb = x.astype(BF16)
    long_seq = seq_rows is None
    if long_seq:
        first = pl.program_id(1) == 0
    else:
        groups = tm // seq_rows
        sub = lax.broadcasted_iota(jnp.int32, (groups, seq_rows, FFN_CHUNK), 1)

    def conv(c, u):
        cw = cw_ref[c]
        u3 = u.reshape(groups, seq_rows, FFN_CHUNK)
        hist = prev_ref[:, :, c * FFN_CHUNK:(c + 1) * FFN_CHUNK]
        back = lambda k: pltpu.roll(u, k, 0).reshape(groups, seq_rows, FFN_CHUNK)
        u1 = jnp.where(sub == 0, hist[:, 1:2], back(1))
        u2 = jnp.where(sub == 0, hist[:, 0:1], jnp.where(sub == 1, hist[:, 1:2], back(2)))
        st_ref[:, :, c * FFN_CHUNK:(c + 1) * FFN_CHUNK] = u3[:, seq_rows - 2:seq_rows]
        out = cw[0:1] * u2 + cw[1:2] * u1 + cw[2:3] * u3 + cb_ref[c]
        return out.reshape(tm, FFN_CHUNK)

    def up_long(c, slot):
        for half in range(2):
            cc = half * n_chunks + c
            u = _dot(xb, wu_ref[cc])
            u_sc[slot, half, CONV_HIST - (CONV_W - 1):CONV_HIST, :] = carry_sc[cc]
            u_sc[slot, half, CONV_HIST:CONV_HIST + tm, :] = u
            carry_sc[cc] = u[tm - (CONV_W - 1):tm]
            st_ref[0, :, cc * FFN_CHUNK:(cc + 1) * FFN_CHUNK] = u[tm - (CONV_W - 1):tm]

    def act_long(c, slot):
        for r0 in range(0, tm, FFN_ROWS):
            halves = []
            for half in range(2):
                cc = half * n_chunks + c
                cw = cw_ref[cc]
                out = cb_ref[cc]
                for j in range(CONV_W):
                    start = CONV_HIST + r0 - (CONV_W - 1 - j)
                    out = out + cw[j:j + 1] * u_sc[slot, half, start:start + FFN_ROWS, :]
                halves.append(out)
            act_sc[r0:r0 + FFN_ROWS, c * FFN_CHUNK:(c + 1) * FFN_CHUNK] = (
                _gelu_tanh(halves[0]) * halves[1]).astype(BF16)

    if long_seq:
        @pl.when(first)
        def _():
            for cc in range(2 * n_chunks):
                carry_sc[cc] = prev_ref[0, :, cc * FFN_CHUNK:(cc + 1) * FFN_CHUNK]

        up_long(0, 0)
        for c in range(n_chunks):
            if c + 1 < n_chunks:
                up_long(c + 1, (c + 1) % 2)
            act_long(c, c % 2)
    else:
        for c in range(n_chunks):
            c_gate = conv(c, _dot(xb, wu_ref[c]))
            c_val = conv(n_chunks + c, _dot(xb, wu_ref[n_chunks + c]))
            act_sc[:, c * FFN_CHUNK:(c + 1) * FFN_CHUNK] = (_gelu_tanh(c_gate) * c_val).astype(BF16)
    f = _dot(act_sc[...], wd_ref[...])
    y = _layernorm(alpha * x + f, g_ref[...], b_ref[...])
    gate = jax.nn.sigmoid(_dot(y.astype(BF16), wpg_ref[...]))
    o_ref[...] = y + gate * _dot(p_ref[...].astype(BF16), wp_ref[...])


def _ffn(x2, p2, prev, wu, cw, cb, wd, g, b, wpg, wp, *, tm, alpha, batch, seq_rows):
    m, d = x2.shape
    d_up = prev.shape[-1]
    n_chunks = wu.shape[0] // 2
    d_ple = p2.shape[1]
    if seq_rows is None:
        tiles = m // batch // tm
        grid = (batch, tiles)
        row = lambda c: pl.BlockSpec((tm, c), lambda bb, i: (bb * tiles + i, 0))
        st_spec = pl.BlockSpec((1, CONV_W - 1, d_up), lambda bb, i: (bb, 0, 0))
        sem = ("parallel", "arbitrary")
    else:
        grid = (1, m // tm)
        row = lambda c: pl.BlockSpec((tm, c), lambda bb, i: (i, 0))
        st_spec = pl.BlockSpec((tm // seq_rows, CONV_W - 1, d_up), lambda bb, i: (i, 0, 0))
        sem = ("arbitrary", "arbitrary")
    n_seq = prev.shape[0]
    return pl.pallas_call(
        functools.partial(_ffn_kernel, alpha=alpha, seq_rows=seq_rows, n_chunks=n_chunks),
        grid=grid,
        in_specs=[row(d), row(d_ple), st_spec, _resident(wu.shape), _resident(cw.shape), _resident(cb.shape),
                  _resident(wd.shape), _resident(g.shape), _resident(b.shape), _resident(wpg.shape),
                  _resident(wp.shape)],
        out_specs=[row(d), st_spec],
        out_shape=[jax.ShapeDtypeStruct((m, d), F32), jax.ShapeDtypeStruct((n_seq, CONV_W - 1, d_up), F32)],
        scratch_shapes=[pltpu.VMEM((2 * n_chunks, CONV_W - 1, FFN_CHUNK), F32),
                        pltpu.VMEM((tm, n_chunks * FFN_CHUNK), BF16),
                        pltpu.VMEM((2, 2, CONV_HIST + tm, FFN_CHUNK), F32)],
        compiler_params=_cparams(sem),
    )(x2, p2, prev, wu, cw, cb, wd, g, b, wpg, wp)


def _layer_weights(l, w):
    d_ff = w['w_down'].shape[1]
    n_chunks = d_ff // FFN_CHUNK
    d = w['w_up'].shape[1]
    chunked = lambda a: a.reshape(a.shape[0], 2 * n_chunks, FFN_CHUNK).transpose(1, 0, 2)
    return {
        'w_in': w['w_in'][l].astype(BF16),
        'w_gate': w['w_gate'][l].astype(BF16),
        'b_gate': w['b_gate'][l][None, :],
        'w_branch_a': w['w_branch_a'][l].astype(BF16),
        'w_branch_b': w['w_branch_b'][l].astype(BF16),
        'w_out': w['w_out'][l].astype(BF16),
        'ln1_g': w['ln1_g'][l][None, :], 'ln1_b': w['ln1_b'][l][None, :],
        'w_up': chunked(w['w_up'][l]).astype(BF16),
        'conv_w': chunked(w['conv_w'][l]),
        'conv_b': chunked(w['conv_b'][l][None, :]),
        'w_down': w['w_down'][l].astype(BF16),
        'ln2_g': w['ln2_g'][l][None, :], 'ln2_b': w['ln2_b'][l][None, :],
        'w_ple_gate': w['w_ple_gate'][l].astype(BF16),
        'w_ple': w['w_ple'][l].astype(BF16),
        'subln_g': w['diff_subln_g'][l][None, :],
        'lam': (jnp.exp(jnp.sum(w['diff_lq1'][l] * w['diff_lk1'][l]))
                - jnp.exp(jnp.sum(w['diff_lq2'][l] * w['diff_lk2'][l]))
                + (0.8 - 0.6 * math.exp(-0.3 * l))).reshape(1).astype(F32),
    }


def _tail(x2, ya, yb, p2, prev, wl, *, alpha, tm_mix, tm_ffn, batch, seq_rows):
    x1 = _mix(x2, ya, yb, wl['w_gate'], wl['b_gate'], wl['w_branch_a'], wl['w_branch_b'], wl['w_out'],
              wl['ln1_g'], wl['ln1_b'], tm=tm_mix, alpha=alpha)
    return _ffn(x1, p2, prev, wl['w_up'], wl['conv_w'], wl['conv_b'], wl['w_down'], wl['ln2_g'], wl['ln2_b'],
                wl['w_ple_gate'], wl['w_ple'], tm=tm_ffn, alpha=alpha, batch=batch, seq_rows=seq_rows)


def _forward(x_prompt, x_sample, cache_a_k, cache_a_v, cache_b_k, cache_b_v, state_conv, page_table,
             p_prompt, p_sample, w, *, tm_prompt=256, tm_sample=256, blocks_per_step=4):
    depth = w['w_in'].shape[0]
    batch, seq, d = x_prompt.shape
    dec_batch, n_new, _ = x_sample.shape
    h_a, dh = cache_a_k.shape[3], cache_a_k.shape[4]
    h_b, dv = cache_b_v.shape[3], cache_b_v.shape[4]
    assert dh == DH and cache_b_k.shape[4] == 2 * DH and dv == LANES and seq % ATT_TILE == 0
    d_up = state_conv.shape[-1]
    past_len = page_table.shape[1] * cache_a_k.shape[2]
    alpha = (2 * depth) ** 0.25
    tab = w['rel_bias'].T
    bias_prompt = _prompt_bias_tiles(tab)
    bias_last, bias_new = _sample_bias_tiles(tab, h_a, n_new, past_len)
    caches = (cache_a_k, cache_a_v, cache_b_k, cache_b_v)

    xp = x_prompt.reshape(batch * seq, d)
    xs = x_sample.reshape(dec_batch * n_new, d)
    conv0 = jnp.zeros((batch, CONV_W - 1, d_up), F32)
    rows_p, rows_s = [], []
    for l in range(depth):
        wl = _layer_weights(l, w)
        lam_init = 0.8 - 0.6 * math.exp(-0.3 * l)

        hq, kat, vat, kb, vb, vat16, vbt16, kmean = _proj_prompt(xp, wl['w_in'], batch=batch, seq=seq, h_b=h_b)
        nb = seq // MOBA_BLOCK
        kmean_pad = jnp.pad(kmean.reshape(batch, nb, h_a * DH), ((0, 0), (0, LANES - nb), (0, 0)))
        ya = _moba_prompt(hq, vat16, kmean_pad, bias_prompt[:h_a], batch=batch, seq=seq, h_a=h_a)
        yb = _diff_prompt(wl['lam'], hq, vbt16, bias_prompt[h_a:], wl['subln_g'], batch=batch, seq=seq, h_b=h_b,
                          lam_init=lam_init)
        xp, conv_p = _tail(xp, ya, yb, p_prompt[l].reshape(batch * seq, -1), conv0, wl, alpha=alpha,
                           tm_mix=tm_prompt, tm_ffn=tm_prompt, batch=batch, seq_rows=None)
        token_major = lambda a: a.reshape(batch, h_a, DH, seq).transpose(0, 3, 1, 2)
        rows_p.append((token_major(kat), token_major(vat), kb.reshape(batch, seq, h_b, 2 * DH),
                       vb.reshape(batch, seq, h_b, dv), conv_p))

        hq, ka, va, kb, vb = _proj_sample(xs, wl['w_in'], tm=tm_sample)
        ya, yb = _sample_attn(page_table, wl['lam'], hq, ka, va, kb, vb, caches, l, bias_last, bias_new,
                              wl['subln_g'], n_new=n_new, h_a=h_a, h_b=h_b, lam_init=lam_init,
                              blocks_per_step=blocks_per_step)
        xs, conv_s = _tail(xs, ya, yb, p_sample[l].reshape(dec_batch * n_new, -1), state_conv[l], wl,
                           alpha=alpha, tm_mix=tm_sample, tm_ffn=tm_sample, batch=1, seq_rows=n_new)
        rows_s.append((ka.reshape(dec_batch, n_new, h_a, DH), va.reshape(dec_batch, n_new, h_a, DH),
                       kb.reshape(dec_batch, n_new, h_b, 2 * DH), vb.reshape(dec_batch, n_new, h_b, dv), conv_s))

    stack = lambda rows, i: jnp.stack([r[i] for r in rows])
    return ((xp.reshape(batch, seq, d), xs.reshape(dec_batch, n_new, d))
            + tuple(stack(rows_p, i) for i in range(5)) + tuple(stack(rows_s, i) for i in range(5)))


def kernel(x_prompt, x_sample, cache_a_k, cache_a_v, cache_b_k, cache_b_v, state_conv, page_table, p_prompt, p_sample, w_in, w_gate, b_gate, w_branch_a, w_branch_b, w_out, rel_bias, diff_lq1, diff_lk1, diff_lq2, diff_lk2, diff_subln_g, ln1_g, ln1_b, w_up, conv_w, conv_b, w_down, ln2_g, ln2_b, w_ple, w_ple_gate):
    w = {'w_in': w_in, 'w_gate': w_gate, 'b_gate': b_gate, 'w_branch_a': w_branch_a,
         'w_branch_b': w_branch_b, 'w_out': w_out, 'rel_bias': rel_bias, 'diff_lq1': diff_lq1,
         'diff_lk1': diff_lk1, 'diff_lq2': diff_lq2, 'diff_lk2': diff_lk2, 'diff_subln_g': diff_subln_g,
         'ln1_g': ln1_g, 'ln1_b': ln1_b, 'w_up': w_up, 'conv_w': conv_w, 'conv_b': conv_b,
         'w_down': w_down, 'ln2_g': ln2_g, 'ln2_b': ln2_b, 'w_ple': w_ple, 'w_ple_gate': w_ple_gate}
    return _forward(x_prompt, x_sample, cache_a_k, cache_a_v, cache_b_k, cache_b_v, state_conv, page_table,
                    p_prompt, p_sample, w)
```

```python
import functools
import math

import numpy as np
import jax
import jax.numpy as jnp
from jax import lax
from jax.experimental import pallas as pl
from jax.experimental.pallas import tpu as pltpu

F32 = jnp.float32
BF16 = jnp.bfloat16

DH = 64
MOBA_BLOCK = 256
MOBA_TOPK = 3
N_BUCKETS = 32
MAX_DISTANCE = 128
CONV_W = 3
LN_EPS = 1e-5
RMS_EPS = 1e-5
NEG_INF = -1e30
LOG2E = math.log2(math.e)
LANES = 128
BF16_ROWS = 16
VMEM_LIMIT = 56 * 1024 * 1024

ATT_TILE = 2 * MOBA_BLOCK
BLOCKS_PER_TILE = ATT_TILE // MOBA_BLOCK
FFN_CHUNK = 256
FFN_ROWS = 64
CONV_HIST = 8


def _cparams(sem):
    return pltpu.CompilerParams(dimension_semantics=sem, vmem_limit_bytes=VMEM_LIMIT)


def _dot(a, b):
    return jnp.dot(a, b, preferred_element_type=F32)


def _dot_nt(a, b):
    return lax.dot_general(a, b, (((1,), (1,)), ((), ())), preferred_element_type=F32)


def _resident(shape):
    return pl.BlockSpec(shape, lambda *_: (0,) * len(shape), pipeline_mode=pl.Buffered(1))


def _t5_bucket_np(dist):
    dist = np.maximum(dist, 0)
    exact = N_BUCKETS // 2
    logv = (np.log(np.maximum(dist, 1).astype(np.float32) / np.float32(exact))
            / np.float32(math.log(MAX_DISTANCE / exact))).astype(np.float32)
    large = np.minimum(exact + (logv * np.float32(N_BUCKETS - exact)).astype(np.int32), N_BUCKETS - 1)
    return np.where(dist < exact, dist, large).astype(np.int32)


def _bias_lookup(tab_rows, buckets, per_row=False):
    shape = buckets.shape if per_row else (tab_rows.shape[0],) + buckets.shape
    out = jnp.zeros(shape, F32)
    for n in np.unique(buckets):
        col = tab_rows[:, n].reshape((-1,) + (1,) * (len(shape) - 1))
        out = jnp.where(jnp.asarray(buckets == n), col, out)
    return out


def _prompt_bias_tiles(tab):
    t = ATT_TILE
    j = np.arange(t)[:, None]
    i = np.arange(t)[None, :]
    far = tab[:, N_BUCKETS - 1][:, None, None]
    diag = jnp.where(jnp.asarray(i >= j)[None], (_bias_lookup(tab, _t5_bucket_np(i - j)) - far) * LOG2E, NEG_INF)
    prev = (_bias_lookup(tab, _t5_bucket_np(t + i - j)) - far) * LOG2E
    return jnp.stack([diag, prev], axis=1).astype(F32)


def _sample_bias_tiles(tab, h_a, n_new, past_len):
    h_b = tab.shape[0] - h_a
    rows = (h_a + 2 * h_b) * n_new
    r = np.arange(rows)
    qi = (r % n_new)[:, None]
    head = np.where(r < h_a * n_new, r // n_new, h_a + (r - h_a * n_new) // (2 * n_new))
    onehot = jnp.asarray(head[:, None] == np.arange(tab.shape[0])[None, :], F32)
    tab_rows = jnp.dot(onehot, tab, precision=lax.Precision.HIGHEST)
    far = tab_rows[:, N_BUCKETS - 1:]
    t = np.arange(MOBA_BLOCK)[None, :]
    dist_last = past_len + qi - (past_len - MOBA_BLOCK + t)
    last = (_bias_lookup(tab_rows, _t5_bucket_np(dist_last), per_row=True) - far) * LOG2E
    j = np.arange(LANES)[None, :]
    new = jnp.where(jnp.asarray((j <= qi) & (j < n_new)),
                    (_bias_lookup(tab_rows, _t5_bucket_np(qi - j), per_row=True) - far) * LOG2E, NEG_INF)
    return last.astype(F32), new.astype(F32)


def _proj_prompt_kernel(x_ref, w_ref, hq_ref, kat_ref, vat_ref, kb_ref, vb_ref, vat16_ref, vbt16_ref, kmean_ref,
                        *, seg, q_scale, h_b):
    x = x_ref[...].astype(BF16)
    dv = seg // h_b
    for s in range(6):
        h = _dot(x, w_ref[:, s * seg:(s + 1) * seg])
        if s == 0:
            hq_ref[:, 0:seg] = (h * q_scale).astype(BF16)
        elif s == 1:
            hq_ref[:, seg:2 * seg] = h.astype(BF16)
            for n in range(BLOCKS_PER_TILE):
                blk = h[n * MOBA_BLOCK:(n + 1) * MOBA_BLOCK]
                kmean_ref[n] = jnp.sum(blk, axis=0, keepdims=True) * (1.0 / MOBA_BLOCK)
            kat_ref[...] = h.T
        elif s == 2:
            ht = h.T
            vat_ref[...] = ht
            vat16_ref[...] = ht.astype(BF16)
        elif s == 3:
            hq_ref[:, 2 * seg:3 * seg] = (h * q_scale).astype(BF16)
        elif s == 4:
            hq_ref[:, 3 * seg:4 * seg] = h.astype(BF16)
            for hh in range(h_b):
                kb_ref[:, hh, :] = h[:, hh * dv:(hh + 1) * dv]
        else:
            for hh in range(h_b):
                vb_ref[:, hh, :] = h[:, hh * dv:(hh + 1) * dv]
            vbt16_ref[...] = h.T.astype(BF16)


def _proj_prompt(x2, w_in_l, *, batch, seq, h_b):
    tm = ATT_TILE
    m, d = x2.shape
    seg = w_in_l.shape[1] // 6
    nt = seq // tm
    dv = seg // h_b
    row = lambda w: pl.BlockSpec((tm, w), lambda b, i: (b * nt + i, 0))
    feat = pl.BlockSpec((None, seg, tm), lambda b, i: (b, 0, i))
    heads = pl.BlockSpec((tm, h_b, dv), lambda b, i: (b * nt + i, 0, 0))
    tiles = pl.BlockSpec((None, None, seg, tm), lambda b, i: (b, i, 0, 0))
    return pl.pallas_call(
        functools.partial(_proj_prompt_kernel, seg=seg, q_scale=DH ** -0.5 * LOG2E, h_b=h_b),
        grid=(batch, nt),
        in_specs=[row(d), _resident(w_in_l.shape)],
        out_specs=[row(4 * seg), feat, feat, heads, heads, tiles, tiles,
                   pl.BlockSpec((BLOCKS_PER_TILE, 1, seg), lambda b, i: (b * nt + i, 0, 0))],
        out_shape=[jax.ShapeDtypeStruct((m, 4 * seg), BF16),
                   jax.ShapeDtypeStruct((batch, seg, seq), F32), jax.ShapeDtypeStruct((batch, seg, seq), F32),
                   jax.ShapeDtypeStruct((m, h_b, dv), F32), jax.ShapeDtypeStruct((m, h_b, dv), F32),
                   jax.ShapeDtypeStruct((batch, nt, seg, tm), BF16), jax.ShapeDtypeStruct((batch, nt, seg, tm), BF16),
                   jax.ShapeDtypeStruct((m // MOBA_BLOCK, 1, seg), F32)],
        compiler_params=_cparams(("parallel", "parallel")),
    )(x2, w_in_l)


def _proj_sample_kernel(x_ref, w_ref, hq_ref, ka_ref, va_ref, kb_ref, vb_ref, *, seg, q_scale):
    x = x_ref[...].astype(BF16)
    outs = (None, ka_ref, va_ref, None, kb_ref, vb_ref)
    for s in range(6):
        h = _dot(x, w_ref[:, s * seg:(s + 1) * seg])
        if s == 0:
            hq_ref[:, 0:seg] = h * q_scale
        elif s == 3:
            hq_ref[:, seg:2 * seg] = h * q_scale
        else:
            outs[s][...] = h


def _proj_sample(x2, w_in_l, *, tm):
    m, d = x2.shape
    seg = w_in_l.shape[1] // 6
    row = lambda w: pl.BlockSpec((tm, w), lambda i: (i, 0))
    return pl.pallas_call(
        functools.partial(_proj_sample_kernel, seg=seg, q_scale=DH ** -0.5 * LOG2E),
        grid=(m // tm,),
        in_specs=[row(d), _resident(w_in_l.shape)],
        out_specs=[row(2 * seg)] + [row(seg)] * 4,
        out_shape=[jax.ShapeDtypeStruct((m, 2 * seg), F32)] + [jax.ShapeDtypeStruct((m, seg), F32)] * 4,
        compiler_params=_cparams(("parallel",)),
    )(x2, w_in_l)


PREV_TILE_BIAS_FROM = ATT_TILE - MAX_DISTANCE
SCORE_COLS = 256
SOFTMAX_ROWS = 32


def _flash_stages(s_ref, p_ref, bias_ref, bias_shift, vt_ext, m_sc, acc_sc, h, first, bias_from=0):
    t_k, t_q = s_ref.shape
    sub = 8
    state = {}

    def chunk(c):
        rows = slice(c * SOFTMAX_ROWS, (c + 1) * SOFTMAX_ROWS)
        s = s_ref[rows, :]
        if bias_ref is not None and rows.stop > bias_from:
            s = s + bias_ref[rows, :]
        return s if bias_shift is None else s + bias_shift

    def column_max():
        m8 = None
        for c in range(t_k // SOFTMAX_ROWS):
            part = jnp.max(chunk(c).reshape(SOFTMAX_ROWS // sub, sub, t_q), axis=0)
            m8 = part if m8 is None else jnp.maximum(m8, part)
        m_new = jnp.max(m8, axis=0, keepdims=True)
        if not first:
            state['m_old'] = m_sc[h]
            m_new = jnp.maximum(state['m_old'], m_new)
        state['m_new'] = m_new

    def exponentials():
        for c in range(t_k // SOFTMAX_ROWS):
            p_ref[c * SOFTMAX_ROWS:(c + 1) * SOFTMAX_ROWS, :] = jnp.exp2(chunk(c) - state['m_new']).astype(BF16)

    def values():
        pv = _dot(vt_ext(), p_ref[...])
        if first:
            acc_sc[h] = pv
        else:
            acc_sc[h] = jnp.exp2(state['m_old'] - state['m_new']) * acc_sc[h] + pv
        m_sc[h] = state['m_new']

    return [column_max, exponentials, values]


def _interleave(score_parts, softmax_stages):
    for i in range(max(len(score_parts), len(softmax_stages))):
        if i < len(score_parts):
            score_parts[i]()
        if i < len(softmax_stages):
            softmax_stages[i]()


def _score_buffers(t):
    return [pltpu.VMEM((2, t, t), F32)] * 2 + [pltpu.VMEM((2, t, t), BF16)] * 2


def _flash_sweep(qi, scores, consume):
    near = jnp.maximum(qi - 1, 0)
    n_far = near
    last = jnp.maximum(n_far - 1, 0)
    _interleave(scores(qi, 0), [])
    _interleave(scores(near, 1), consume(qi, 0, 0, None, True))
    _interleave(scores(0, 0), consume(near, 1, 1, jnp.where(qi == 0, NEG_INF, 0.0), False))

    def pair(p, carry):
        j = 2 * p
        _interleave(scores(jnp.minimum(j + 1, last), 1), consume(j, 0, None, None, False))
        _interleave(scores(jnp.minimum(j + 2, last), 0), consume(j + 1, 1, None, None, False))
        return carry

    lax.fori_loop(0, n_far // 2, pair, 0)

    @pl.when(n_far % 2 == 1)
    def _():
        _interleave([], consume(last, 0, None, None, False))


def _moba_prompt_kernel(q_ref, k_ref, vt_ref, kmean_ref, bias_ref, o_ref, m_sc, acc_sc, s0_sc, s1_sc, p0_sc, p1_sc,
                        qx_sc):
    t = ATT_TILE
    qi = pl.program_id(2)
    qt = q_ref[...].astype(F32).T
    row = lax.broadcasted_iota(jnp.int32, (LANES, t), 0)
    head0 = row < DH
    q_heads = (jnp.where(head0, qt, 0.0).astype(BF16), jnp.where(head0, 0.0, qt).astype(BF16))
    kmean = kmean_ref[...].astype(BF16)
    cur = qi * BLOCKS_PER_TILE + lax.broadcasted_iota(jnp.int32, (LANES, t), 1) // MOBA_BLOCK

    def block_mask(qh):
        gate = jnp.where(row < cur, _dot(kmean, qh), NEG_INF)
        chosen = row == cur
        for j in range(MOBA_TOPK):
            best = jnp.max(gate, axis=0, keepdims=True)
            idx = jnp.min(jnp.where(gate == best, row, 2 * LANES), axis=0, keepdims=True)
            hit = row == idx
            chosen = jnp.logical_or(chosen, jnp.logical_and(hit, cur > j))
            gate = jnp.where(hit, -3e38, gate)
        return jnp.where(chosen, 0.0, NEG_INF).astype(BF16)

    for h in range(2):
        qx_sc[h] = jnp.concatenate([q_heads[h], block_mask(q_heads[h])], axis=0)
    lane = lax.broadcasted_iota(jnp.int32, (1, LANES), 1)
    ones = jnp.ones((BF16_ROWS, t), BF16)

    s_bufs, p_bufs = (s0_sc, s1_sc), (p0_sc, p1_sc)

    def scores(kj, slot):
        def part(h, c):
            rows = pl.ds(pl.multiple_of(kj * t, t), t)
            onehot = jnp.concatenate(
                [jnp.broadcast_to(jnp.where(lane == kj * BLOCKS_PER_TILE + n, 1.0, 0.0).astype(BF16),
                                  (MOBA_BLOCK, LANES)) for n in range(BLOCKS_PER_TILE)], axis=0)
            k_ext = jnp.concatenate([k_ref[rows, :], onehot], axis=1)
            cols = slice(c * SCORE_COLS, (c + 1) * SCORE_COLS)
            s_bufs[slot][h, :, cols] = _dot(k_ext, qx_sc[h, :, cols])
        return [functools.partial(part, h, c) for h in range(2) for c in range(t // SCORE_COLS)]

    def consume(kj, slot, bias_idx, bias_shift, first):
        stages = []
        for h in range(2):
            vt_ext = lambda h=h: jnp.concatenate([vt_ref[kj, h * DH:(h + 1) * DH, :], ones], axis=0)
            bias = None if bias_idx is None else bias_ref.at[h, bias_idx]
            stages += _flash_stages(s_bufs[slot].at[h], p_bufs[slot].at[h], bias, bias_shift, vt_ext, m_sc,
                                    acc_sc, h, first, bias_from=PREV_TILE_BIAS_FROM if bias_idx == 1 else 0)
        return stages

    _flash_sweep(qi, scores, consume)
    out = [acc_sc[h, 0:DH] / acc_sc[h, DH:DH + 1] for h in range(2)]
    o_ref[...] = jnp.concatenate(out, axis=0).T.astype(o_ref.dtype)


def _moba_prompt(hq, vat16, kmean_pad, bias_a, *, batch, seq, h_a):
    t = ATT_TILE
    pairs = h_a * DH // LANES
    seg_blocks = hq.shape[1] // 4 // LANES
    nq = seq // t
    return pl.pallas_call(
        _moba_prompt_kernel,
        grid=(batch, pairs, nq),
        in_specs=[
            pl.BlockSpec((t, LANES), lambda b, j, i: (b * nq + i, j)),
            pl.BlockSpec((seq, LANES), lambda b, j, i: (b, seg_blocks + j)),
            pl.BlockSpec((None, nq, LANES, t), lambda b, j, i: (b, 0, j, 0)),
            pl.BlockSpec((None, LANES, LANES), lambda b, j, i: (b, 0, j)),
            pl.BlockSpec((2, 2, t, t), lambda b, j, i: (j, 0, 0, 0)),
        ],
        out_specs=pl.BlockSpec((t, LANES), lambda b, j, i: (b * nq + i, j)),
        out_shape=jax.ShapeDtypeStruct((batch * seq, pairs * LANES), BF16),
        scratch_shapes=([pltpu.VMEM((2, 1, t), F32), pltpu.VMEM((2, DH + BF16_ROWS, t), F32)] + _score_buffers(t)
                        + [pltpu.VMEM((2, 2 * LANES, t), BF16)]),
        compiler_params=_cparams(("parallel", "parallel", "arbitrary")),
    )(hq, hq, vat16, kmean_pad, bias_a)


def _diff_prompt_kernel(lam_ref, q_ref, k_ref, vt_ref, bias_ref, g_ref, o_ref, m_sc, acc_sc,
                        s0_sc, s1_sc, p0_sc, p1_sc, qx_sc, *, out_scale):
    t = ATT_TILE
    dv = LANES
    qi = pl.program_id(2)
    qt = q_ref[...].astype(F32).T
    row = lax.broadcasted_iota(jnp.int32, (LANES, t), 0)
    half0 = row < DH
    qx_sc[0] = jnp.where(half0, qt, 0.0).astype(BF16)
    qx_sc[1] = jnp.where(half0, 0.0, qt).astype(BF16)
    ones = jnp.ones((BF16_ROWS, t), BF16)

    s_bufs, p_bufs = (s0_sc, s1_sc), (p0_sc, p1_sc)

    def scores(kj, slot):
        def part(h, c):
            k = k_ref[pl.ds(pl.multiple_of(kj * t, t), t), :]
            cols = slice(c * SCORE_COLS, (c + 1) * SCORE_COLS)
            s_bufs[slot][h, :, cols] = _dot(k, qx_sc[h, :, cols])
        return [functools.partial(part, h, c) for h in range(2) for c in range(t // SCORE_COLS)]

    def consume(kj, slot, bias_idx, bias_shift, first):
        vt_ext = lambda: jnp.concatenate([vt_ref[kj], ones], axis=0)
        bias = None if bias_idx is None else bias_ref.at[bias_idx]
        stages = []
        for h in range(2):
            stages += _flash_stages(s_bufs[slot].at[h], p_bufs[slot].at[h], bias, bias_shift, vt_ext, m_sc,
                                    acc_sc, h, first, bias_from=PREV_TILE_BIAS_FROM if bias_idx == 1 else 0)
        return stages

    _flash_sweep(qi, scores, consume)
    o = (acc_sc[0, 0:dv] / acc_sc[0, dv:dv + 1]
         - lam_ref[0] * (acc_sc[1, 0:dv] / acc_sc[1, dv:dv + 1]))
    rms = lax.rsqrt(jnp.mean(o * o, axis=0, keepdims=True) + RMS_EPS)
    o_ref[...] = (o * rms * g_ref[...] * out_scale).T.astype(o_ref.dtype)


def _diff_prompt(lam, hq, vbt16, bias_b, subln_g, *, batch, seq, h_b, lam_init):
    t = ATT_TILE
    seg_blocks = hq.shape[1] // 4 // LANES
    nq = seq // t
    g_spread = jnp.broadcast_to(subln_g.reshape(LANES, 1), (LANES, t))
    return pl.pallas_call(
        functools.partial(_diff_prompt_kernel, out_scale=1.0 - lam_init),
        grid=(batch, h_b, nq),
        in_specs=[
            pl.BlockSpec(memory_space=pltpu.SMEM),
            pl.BlockSpec((t, LANES), lambda b, h, i: (b * nq + i, 2 * seg_blocks + h)),
            pl.BlockSpec((seq, LANES), lambda b, h, i: (b, 3 * seg_blocks + h)),
            pl.BlockSpec((None, nq, LANES, t), lambda b, h, i: (b, 0, h, 0)),
            pl.BlockSpec((None, 2, t, t), lambda b, h, i: (h, 0, 0, 0)),
            pl.BlockSpec((LANES, t), lambda b, h, i: (0, 0)),
        ],
        out_specs=pl.BlockSpec((t, LANES), lambda b, h, i: (b * nq + i, h)),
        out_shape=jax.ShapeDtypeStruct((batch * seq, h_b * LANES), BF16),
        scratch_shapes=([pltpu.VMEM((2, 1, t), F32), pltpu.VMEM((2, LANES + BF16_ROWS, t), F32)] + _score_buffers(t)
                        + [pltpu.VMEM((2, LANES, t), BF16)]),
        compiler_params=_cparams(("parallel", "parallel", "arbitrary")),
    )(lam, hq, hq, vbt16, bias_b, g_spread)


def _softmax_tile(s):
    m = jnp.max(s, axis=1, keepdims=True)
    p = jnp.exp2(s - m)
    return m, p, jnp.sum(p, axis=1, keepdims=True)


def _sample_attn_kernel(pt_ref, lam_ref, hq_ref, ka_new_ref, va_new_ref, kb_new_ref, vb_new_ref,
                        bias_last_ref, bias_new_ref, g_ref, *refs,
                        n_steps, blocks_per_step, n_new, h_a, h_b, page, out_scale):
    del pt_ref
    n_pages = 2 * blocks_per_step
    page_refs = refs[:4 * n_pages]
    ya_ref, yb_ref = refs[4 * n_pages:4 * n_pages + 2]
    w_sc, gate_sc, m_sc, l_sc, o_sc = refs[4 * n_pages + 2:]
    g = pl.program_id(1)
    w_a = h_a * DH
    rows_a = h_a * n_new
    rows = rows_a + 2 * h_b * n_new
    n_past = n_steps * blocks_per_step
    lane = lax.broadcasted_iota(jnp.int32, (rows, LANES), 1)

    at_step = (lambda step: pl.when(g == step)) if n_steps > 1 else (lambda step: (lambda body: body()))

    @at_step(0)
    def _():
        tiled = jnp.concatenate([hq_ref[...]] * (rows // n_new), axis=0)
        r = lax.broadcasted_iota(jnp.int32, tiled.shape, 0)
        c = lax.broadcasted_iota(jnp.int32, tiled.shape, 1)
        w_sc[...] = jnp.where(c // DH == r // n_new, tiled, 0.0).astype(BF16)
        gate_sc[...] = jnp.zeros_like(gate_sc)
        m_sc[...] = jnp.zeros_like(m_sc)
        l_sc[...] = jnp.zeros_like(l_sc)

    def partial_block(n, s, bias, pv_a, pv_b):
        gsum = jnp.sum(s, axis=1, keepdims=True)
        m, p, l = _softmax_tile(s + bias)
        p = p.astype(BF16)
        o_sc[n, 0:rows_a, :] = pv_a(p[0:rows_a])
        o_sc[n, rows_a:rows, :] = pv_b(p[rows_a:rows])
        here = lane == n
        gate_sc[...] = jnp.where(here, gsum, gate_sc[...])
        m_sc[...] = jnp.where(here, m, m_sc[...])
        l_sc[...] = jnp.where(here, l, l_sc[...])

    def heads_on_lanes(ref):
        return jnp.concatenate([ref[pl.ds(h, page, stride=h_b), :] for h in range(h_b)], axis=1).astype(BF16)

    for u in range(blocks_per_step):
        n = g * blocks_per_step + u
        pages = [page_refs[4 * (2 * u + v):4 * (2 * u + v) + 4] for v in range(2)]
        ka_t = [pg[0][...].reshape(w_a, page).astype(BF16) for pg in pages]
        va_t = [pg[1][...].reshape(w_a, page).astype(BF16) for pg in pages]
        kb = [heads_on_lanes(pg[2]) for pg in pages]
        vb = [heads_on_lanes(pg[3]) for pg in pages]
        q_a = w_sc[0:rows_a, 0:w_a]
        q_b = w_sc[rows_a:rows, w_a:2 * w_a]
        s = jnp.concatenate([
            jnp.concatenate([_dot(q_a, k) for k in ka_t], axis=1),
            jnp.concatenate([_dot_nt(q_b, k) for k in kb], axis=1)], axis=0)
        pv_a = lambda p: sum(_dot_nt(p[:, v * page:(v + 1) * page], va_t[v]) for v in range(2))
        pv_b = lambda p: sum(_dot(p[:, v * page:(v + 1) * page], vb[v]) for v in range(2))
        is_last = (n == n_past - 1).astype(F32)
        partial_block(n, s, bias_last_ref[...] * is_last, pv_a, pv_b)

    @at_step(n_steps - 1)
    def _():
        pad =jnp.zeros((LANES - n_new, w_a), F32)
        padded = lambda ref: jnp.concatenate([ref[...], pad], axis=0).astype(BF16)
        k_cat = jnp.concatenate([padded(ka_new_ref), padded(kb_new_ref)], axis=1)
        va, vb = padded(va_new_ref), padded(vb_new_ref)
        partial_block(n_past, _dot_nt(w_sc[...], k_cat), bias_new_ref[...],
                      lambda p: _dot(p, va), lambda p: _dot(p, vb))

        row = lax.broadcasted_iota(jnp.int32, (rows, LANES), 0)
        gate = jnp.where(lane < n_past, gate_sc[...], NEG_INF)
        chosen = lane == n_past
        for _ in range(min(MOBA_TOPK, n_past)):
            best = jnp.max(gate, axis=1, keepdims=True)
            idx = jnp.min(jnp.where(gate == best, lane, 2 * LANES), axis=1, keepdims=True)
            hit = lane == idx
            chosen = jnp.logical_or(chosen, hit)
            gate = jnp.where(hit, -3e38, gate)
        chosen = jnp.logical_or(chosen, jnp.logical_and(row >= rows_a, lane <= n_past))
        m_fin = jnp.max(jnp.where(chosen, m_sc[...], NEG_INF), axis=1, keepdims=True)
        wgt = jnp.where(chosen, jnp.exp2(m_sc[...] - m_fin), 0.0)
        l_fin = jnp.sum(wgt * l_sc[...], axis=1, keepdims=True)
        o = wgt[:, 0:1] * o_sc[0]
        for n in range(1, n_past + 1):
            o = o + wgt[:, n:n + 1] * o_sc[n]
        o = o / l_fin

        col = lax.broadcasted_iota(jnp.int32, (n_new, w_a), 1)
        ya = jnp.zeros((n_new, w_a), F32)
        for h in range(h_a):
            ya = ya + jnp.where(col // DH == h, o[h * n_new:(h + 1) * n_new], 0.0)
        ya_ref[...] = ya
        lam = lam_ref[0]
        dv = w_a // h_b
        outs = []
        for h in range(h_b):
            r0 = rows_a + 2 * h * n_new
            d = (o[r0:r0 + n_new, h * dv:(h + 1) * dv]
                 - lam * o[r0 + n_new:r0 + 2 * n_new, h * dv:(h + 1) * dv])
            rms = lax.rsqrt(jnp.mean(d * d, axis=1, keepdims=True) + RMS_EPS)
            outs.append(d * rms * g_ref[...] * out_scale)
        yb_ref[...] = jnp.concatenate(outs, axis=1)


def _sample_attn(page_table, lam, hq, ka, va, kb, vb, caches, layer, bias_last, bias_new, subln_g,
                 *, n_new, h_a, h_b, lam_init, blocks_per_step):
    dec_batch, n_pages = page_table.shape
    page = caches[0].shape[2]
    assert 2 * page == MOBA_BLOCK and n_pages % (2 * blocks_per_step) == 0
    w_a = h_a * DH
    rows = (h_a + 2 * h_b) * n_new
    n_steps = n_pages // (2 * blocks_per_step)
    pps = 2 * blocks_per_step
    n_blocks = n_pages // 2 + 1
    caches = [caches[0].transpose(0, 1, 3, 4, 2), caches[1].transpose(0, 1, 3, 4, 2),
              caches[2].reshape(caches[2].shape[:2] + (page * h_b, -1)),
              caches[3].reshape(caches[3].shape[:2] + (page * h_b, -1))]

    def page_spec(v, c):
        shape = (None, None) + c.shape[2:]
        zeros = (0,) * (c.ndim - 2)
        return pl.BlockSpec(shape, lambda b, g, pt: (layer, pt[b * n_pages + g * pps + v]) + zeros)

    new_spec = pl.BlockSpec((n_new, w_a), lambda b, g, pt: (b, 0))
    in_specs = [
        pl.BlockSpec(memory_space=pltpu.SMEM),
        pl.BlockSpec((n_new, hq.shape[1]), lambda b, g, pt: (b, 0)),
        new_spec, new_spec, new_spec, new_spec,
        pl.BlockSpec((rows, MOBA_BLOCK), lambda b, g, pt: (0, 0)),
        pl.BlockSpec((rows, LANES), lambda b, g, pt: (0, 0)),
        pl.BlockSpec((1, w_a // h_b), lambda b, g, pt: (0, 0)),
    ]
    operands = [lam, hq, ka, va, kb, vb, bias_last, bias_new, subln_g]
    for v in range(pps):
        for c in caches:
            in_specs.append(page_spec(v, c))
            operands.append(c)
    grid_spec = pltpu.PrefetchScalarGridSpec(
        num_scalar_prefetch=1,
        grid=(dec_batch, n_steps),
        in_specs=in_specs,
        out_specs=[new_spec, new_spec],
        scratch_shapes=[
            pltpu.VMEM((rows, 2 * w_a), BF16),
            pltpu.VMEM((rows, LANES), F32), pltpu.VMEM((rows, LANES), F32), pltpu.VMEM((rows, LANES), F32),
            pltpu.VMEM((n_blocks, rows, w_a), F32),
        ],
    )
    return pl.pallas_call(
        functools.partial(_sample_attn_kernel, n_steps=n_steps, blocks_per_step=blocks_per_step, n_new=n_new,
                          h_a=h_a, h_b=h_b, page=page, out_scale=1.0 - lam_init),
        grid_spec=grid_spec,
        out_shape=[jax.ShapeDtypeStruct((dec_batch * n_new, w_a), F32)] * 2,
        compiler_params=_cparams(("parallel", "arbitrary")),
    )(page_table.reshape(-1), *operands)


def _layernorm(x, g, b):
    mu = jnp.mean(x, axis=-1, keepdims=True)
    xc = x - mu
    var = jnp.mean(xc * xc, axis=-1, keepdims=True)
    return xc * lax.rsqrt(var + LN_EPS) * g + b


def _mix_kernel(x_ref, ya_ref, yb_ref, wg_ref, bg_ref, wa_ref, wb_ref, wo_ref, g_ref, b_ref, o_ref, *, alpha):
    x = x_ref[...]
    d = x.shape[1]
    gates = jax.nn.sigmoid(_dot(x.astype(BF16), wg_ref[...]) + bg_ref[...])
    z = (gates[:, :d] * _dot(ya_ref[...].astype(BF16), wa_ref[...])
         + gates[:, d:] * _dot(yb_ref[...].astype(BF16), wb_ref[...]))
    mix = _dot(z.astype(BF16), wo_ref[...])
    o_ref[...] = _layernorm(alpha * x + mix, g_ref[...], b_ref[...])


def _mix(x2, ya, yb, wg, bg, wa, wb, wo, g, b, *, tm, alpha):
    m, d = x2.shape
    w = ya.shape[1]
    row = lambda c: pl.BlockSpec((tm, c), lambda i: (i, 0))
    return pl.pallas_call(
        functools.partial(_mix_kernel, alpha=alpha),
        grid=(m // tm,),
        in_specs=[row(d), row(w), row(w), _resident(wg.shape), _resident(bg.shape), _resident(wa.shape),
                  _resident(wb.shape), _resident(wo.shape), _resident(g.shape), _resident(b.shape)],
        out_specs=row(d),
        out_shape=jax.ShapeDtypeStruct((m, d), F32),
        compiler_params=_cparams(("parallel",)),
    )(x2, ya, yb, wg, bg, wa, wb, wo, g, b)


def _gelu_tanh(x):
    return 0.5 * x * (1.0 + jnp.tanh(math.sqrt(2.0 / math.pi) * (x + 0.044715 * (x * x * x))))


def _ffn_kernel(x_ref, p_ref, prev_ref, wu_ref, cw_ref, cb_ref, wd_ref, g_ref, b_ref, wpg_ref, wp_ref,
                o_ref, st_ref, carry_sc, act_sc, u_sc, *, alpha, seq_rows, n_chunks):
    x = x_ref[...]
    tm, d = x.shape
    xb = x.astype(BF16)
    cols = lambda c: slice(c * FFN_CHUNK, (c + 1) * FFN_CHUNK)
    long_seq = seq_rows is None
    if long_seq:
        first = pl.program_id(1) == 0
    else:
        groups = tm // seq_rows
        sub = lax.broadcasted_iota(jnp.int32, (groups, seq_rows, FFN_CHUNK), 1)

    def conv(c, u):
        cw = cw_ref[:, cols(c)]
        u3 = u.reshape(groups, seq_rows, FFN_CHUNK)
        hist = prev_ref[:, :, c * FFN_CHUNK:(c + 1) * FFN_CHUNK]
        back = lambda k: pltpu.roll(u, k, 0).reshape(groups, seq_rows, FFN_CHUNK)
        u1 = jnp.where(sub == 0, hist[:, 1:2], back(1))
        u2 = jnp.where(sub == 0, hist[:, 0:1], jnp.where(sub == 1, hist[:, 1:2], back(2)))
        st_ref[:, :, c * FFN_CHUNK:(c + 1) * FFN_CHUNK] = u3[:, seq_rows - 2:seq_rows]
        out = cw[0:1] * u2 + cw[1:2] * u1 + cw[2:3] * u3 + cb_ref[:, cols(c)]
        return out.reshape(tm, FFN_CHUNK)

    def up_long(c, slot):
        for half in range(2):
            cc = half * n_chunks + c
            u = _dot(xb, wu_ref[:, cols(cc)])
            u_sc[slot, half, CONV_HIST - (CONV_W - 1):CONV_HIST, :] = carry_sc[cc]
            u_sc[slot, half, CONV_HIST:CONV_HIST + tm, :] = u
            carry_sc[cc] = u[tm - (CONV_W - 1):tm]
            st_ref[0, :, cc * FFN_CHUNK:(cc + 1) * FFN_CHUNK] = u[tm - (CONV_W - 1):tm]

    def act_long(c, slot):
        for r0 in range(0, tm, FFN_ROWS):
            halves = []
            for half in range(2):
                cc = half * n_chunks + c
                cw = cw_ref[:, cols(cc)]
                out = cb_ref[:, cols(cc)]
                for j in range(CONV_W):
                    start = CONV_HIST + r0 - (CONV_W - 1 - j)
                    out = out + cw[j:j + 1] * u_sc[slot, half, start:start + FFN_ROWS, :]
                halves.append(out)
            act_sc[r0:r0 + FFN_ROWS, c * FFN_CHUNK:(c + 1) * FFN_CHUNK] = (
                _gelu_tanh(halves[0]) * halves[1]).astype(BF16)

    if long_seq:
        @pl.when(first)
        def _():
            for cc in range(2 * n_chunks):
                carry_sc[cc] = prev_ref[0, :, cc * FFN_CHUNK:(cc + 1) * FFN_CHUNK]

        up_long(0, 0)
        for c in range(n_chunks):
            if c + 1 < n_chunks:
                up_long(c + 1, (c + 1) % 2)
            act_long(c, c % 2)
    else:
        for c in range(n_chunks):
            c_gate = conv(c, _dot(xb, wu_ref[:, cols(c)]))
            c_val = conv(n_chunks + c, _dot(xb, wu_ref[:, cols(n_chunks + c)]))
            act_sc[:, c * FFN_CHUNK:(c + 1) * FFN_CHUNK] = (_gelu_tanh(c_gate) * c_val).astype(BF16)
    f = _dot(act_sc[...], wd_ref[...])
    y = _layernorm(alpha * x + f, g_ref[...], b_ref[...])
    gate = jax.nn.sigmoid(_dot(y.astype(BF16), wpg_ref[...]))
    o_ref[...] = y + gate * _dot(p_ref[...].astype(BF16), wp_ref[...])


def _ffn(x2, p2, prev, wu, cw, cb, wd, g, b, wpg, wp, *, tm, alpha, batch, seq_rows):
    m, d = x2.shape
    d_up = prev.shape[-1]
    n_chunks = wu.shape[1] // 2 // FFN_CHUNK
    d_ple = p2.shape[1]
    if seq_rows is None:
        tiles = m // batch // tm
        grid = (batch, tiles)
        row = lambda c: pl.BlockSpec((tm, c), lambda bb, i: (bb * tiles + i, 0))
        st_spec = pl.BlockSpec((1, CONV_W - 1, d_up), lambda bb, i: (bb, 0, 0))
        sem = ("parallel", "arbitrary")
    else:
        grid = (1, m // tm)
        row = lambda c: pl.BlockSpec((tm, c), lambda bb, i: (i, 0))
        st_spec = pl.BlockSpec((tm // seq_rows, CONV_W - 1, d_up), lambda bb, i: (i, 0, 0))
        sem = ("arbitrary", "arbitrary")
    n_seq = prev.shape[0]
    return pl.pallas_call(
        functools.partial(_ffn_kernel, alpha=alpha, seq_rows=seq_rows, n_chunks=n_chunks),
        grid=grid,
        in_specs=[row(d), row(d_ple), st_spec, _resident(wu.shape), _resident(cw.shape), _resident(cb.shape),
                  _resident(wd.shape), _resident(g.shape), _resident(b.shape), _resident(wpg.shape),
                  _resident(wp.shape)],
        out_specs=[row(d), st_spec],
        out_shape=[jax.ShapeDtypeStruct((m, d), F32), jax.ShapeDtypeStruct((n_seq, CONV_W - 1, d_up), F32)],
        scratch_shapes=[pltpu.VMEM((2 * n_chunks, CONV_W - 1, FFN_CHUNK), F32),
                        pltpu.VMEM((tm, n_chunks * FFN_CHUNK), BF16),
                        pltpu.VMEM((2, 2, CONV_HIST + tm, FFN_CHUNK), F32)],
        compiler_params=_cparams(sem),
    )(x2, p2, prev, wu, cw, cb, wd, g, b, wpg, wp)


def _layer_weights(l, w):
    return {
        'w_in': w['w_in'][l].astype(BF16),
        'w_gate': w['w_gate'][l].astype(BF16),
        'b_gate': w['b_gate'][l][None, :],
        'w_branch_a': w['w_branch_a'][l].astype(BF16),
        'w_branch_b': w['w_branch_b'][l].astype(BF16),
        'w_out': w['w_out'][l].astype(BF16),
        'ln1_g': w['ln1_g'][l][None, :], 'ln1_b': w['ln1_b'][l][None, :],
        'w_up': w['w_up'][l].astype(BF16),
        'conv_w': w['conv_w'][l],
        'conv_b': w['conv_b'][l][None, :],
        'w_down': w['w_down'][l].astype(BF16),
        'ln2_g': w['ln2_g'][l][None, :], 'ln2_b': w['ln2_b'][l][None, :],
        'w_ple_gate': w['w_ple_gate'][l].astype(BF16),
        'w_ple': w['w_ple'][l].astype(BF16),
        'subln_g': w['diff_subln_g'][l][None, :],
        'lam': (jnp.exp(jnp.sum(w['diff_lq1'][l] * w['diff_lk1'][l]))
                - jnp.exp(jnp.sum(w['diff_lq2'][l] * w['diff_lk2'][l]))
                + (0.8 - 0.6 * math.exp(-0.3 * l))).reshape(1).astype(F32),
    }


def _tail(x2, ya, yb, p2, prev, wl, *, alpha, tm_mix, tm_ffn, batch, seq_rows):
    x1 = _mix(x2, ya, yb, wl['w_gate'], wl['b_gate'], wl['w_branch_a'], wl['w_branch_b'], wl['w_out'],
              wl['ln1_g'], wl['ln1_b'], tm=tm_mix, alpha=alpha)
    return _ffn(x1, p2, prev, wl['w_up'], wl['conv_w'], wl['conv_b'], wl['w_down'], wl['ln2_g'], wl['ln2_b'],
                wl['w_ple_gate'], wl['w_ple'], tm=tm_ffn, alpha=alpha, batch=batch, seq_rows=seq_rows)


def _forward(x_prompt, x_sample, cache_a_k, cache_a_v, cache_b_k, cache_b_v, state_conv, page_table,
             p_prompt, p_sample, w, *, tm_prompt=256, tm_sample=256, blocks_per_step=8):
    depth = w['w_in'].shape[0]
    batch, seq, d = x_prompt.shape
    dec_batch, n_new, _ = x_sample.shape
    h_a, dh = cache_a_k.shape[3], cache_a_k.shape[4]
    h_b, dv = cache_b_v.shape[3], cache_b_v.shape[4]
    assert dh == DH and cache_b_k.shape[4] == 2 * DH and dv == LANES and seq % ATT_TILE == 0
    d_up = state_conv.shape[-1]
    past_len = page_table.shape[1] * cache_a_k.shape[2]
    alpha = (2 * depth) ** 0.25
    tab = w['rel_bias'].T
    bias_prompt = _prompt_bias_tiles(tab)
    bias_last, bias_new = _sample_bias_tiles(tab, h_a, n_new, past_len)
    caches = (cache_a_k, cache_a_v, cache_b_k, cache_b_v)

    xp = x_prompt.reshape(batch * seq, d)
    xs = x_sample.reshape(dec_batch * n_new, d)
    conv0 = jnp.zeros((batch, CONV_W - 1, d_up), F32)
    rows_p, rows_s = [], []
    for l in range(depth):
        wl = _layer_weights(l, w)
        lam_init = 0.8 - 0.6 * math.exp(-0.3 * l)

        hq, kat, vat, kb, vb, vat16, vbt16, kmean = _proj_prompt(xp, wl['w_in'], batch=batch, seq=seq, h_b=h_b)
        nb = seq // MOBA_BLOCK
        kmean_pad = jnp.pad(kmean.reshape(batch, nb, h_a * DH), ((0, 0), (0, LANES - nb), (0, 0)))
        ya = _moba_prompt(hq, vat16, kmean_pad, bias_prompt[:h_a], batch=batch, seq=seq, h_a=h_a)
        yb = _diff_prompt(wl['lam'], hq, vbt16, bias_prompt[h_a:], wl['subln_g'], batch=batch, seq=seq, h_b=h_b,
                          lam_init=lam_init)
        xp, conv_p = _tail(xp, ya, yb, p_prompt[l].reshape(batch * seq, -1), conv0, wl, alpha=alpha,
                           tm_mix=tm_prompt, tm_ffn=tm_prompt, batch=batch, seq_rows=None)
        token_major = lambda a: a.reshape(batch, h_a, DH, seq).transpose(0, 3, 1, 2)
        rows_p.append((token_major(kat), token_major(vat), kb.reshape(batch, seq, h_b, 2 * DH),
                       vb.reshape(batch, seq, h_b, dv), conv_p))

        hq, ka, va, kb, vb = _proj_sample(xs, wl['w_in'], tm=tm_sample)
        ya, yb = _sample_attn(page_table, wl['lam'], hq, ka, va, kb, vb, caches, l, bias_last, bias_new,
                              wl['subln_g'], n_new=n_new, h_a=h_a, h_b=h_b, lam_init=lam_init,
                              blocks_per_step=blocks_per_step)
        xs, conv_s = _tail(xs, ya, yb, p_sample[l].reshape(dec_batch * n_new, -1), state_conv[l], wl,
                           alpha=alpha, tm_mix=tm_sample, tm_ffn=tm_sample, batch=1, seq_rows=n_new)
        rows_s.append((ka.reshape(dec_batch, n_new, h_a, DH), va.reshape(dec_batch, n_new, h_a, DH),
                       kb.reshape(dec_batch, n_new, h_b, 2 * DH), vb.reshape(dec_batch, n_new, h_b, dv), conv_s))

    stack = lambda rows, i: jnp.stack([r[i] for r in rows])
    return ((xp.reshape(batch, seq, d), xs.reshape(dec_batch, n_new, d))
            + tuple(stack(rows_p, i) for i in range(5)) + tuple(stack(rows_s, i) for i in range(5)))


def kernel(x_prompt, x_sample, cache_a_k, cache_a_v, cache_b_k, cache_b_v, state_conv, page_table, p_prompt, p_sample, w_in, w_gate, b_gate, w_branch_a, w_branch_b, w_out, rel_bias, diff_lq1, diff_lk1, diff_lq2, diff_lk2, diff_subln_g, ln1_g, ln1_b, w_up, conv_w, conv_b, w_down, ln2_g, ln2_b, w_ple, w_ple_gate):
    w = {'w_in': w_in, 'w_gate': w_gate, 'b_gate': b_gate, 'w_branch_a': w_branch_a,
         'w_branch_b': w_branch_b, 'w_out': w_out, 'rel_bias': rel_bias, 'diff_lq1': diff_lq1,
         'diff_lk1': diff_lk1, 'diff_lq2': diff_lq2, 'diff_lk2': diff_lk2, 'diff_subln_g': diff_subln_g,
         'ln1_g': ln1_g, 'ln1_b': ln1_b, 'w_up': w_up, 'conv_w': conv_w, 'conv_b': conv_b,
         'w_down': w_down, 'ln2_g': ln2_g, 'ln2_b': ln2_b, 'w_ple': w_ple, 'w_ple_gate': w_ple_gate}
    return _forward(x_prompt, x_sample, cache_a_k, cache_a_v, cache_b_k, cache_b_v, state_conv, page_table,
                    p_prompt, p_sample, w)
```

```python
import functools
import math

import numpy as np
import jax
import jax.numpy as jnp
from jax import lax
from jax.experimental import pallas as pl
from jax.experimental.pallas import tpu as pltpu

F32 = jnp.float32
BF16 = jnp.bfloat16

DH = 64
MOBA_BLOCK = 256
MOBA_TOPK = 3
N_BUCKETS = 32
MAX_DISTANCE = 128
CONV_W = 3
LN_EPS = 1e-5
RMS_EPS = 1e-5
NEG_INF = -1e30
LOG2E = math.log2(math.e)
LANES = 128
BF16_ROWS = 16
VMEM_LIMIT = 56 * 1024 * 1024

ATT_TILE = 2 * MOBA_BLOCK
BLOCKS_PER_TILE = ATT_TILE // MOBA_BLOCK
FFN_CHUNK = 256
FFN_ROWS = 64
CONV_HIST = 8


def _cparams(sem):
    return pltpu.CompilerParams(dimension_semantics=sem, vmem_limit_bytes=VMEM_LIMIT)


def _dot(a, b):
    return jnp.dot(a, b, preferred_element_type=F32)


def _dot_nt(a, b):
    return lax.dot_general(a, b, (((1,), (1,)), ((), ())), preferred_element_type=F32)


def _resident(shape):
    return pl.BlockSpec(shape, lambda *_: (0,) * len(shape), pipeline_mode=pl.Buffered(1))


def _t5_bucket_np(dist):
    dist = np.maximum(dist, 0)
    exact = N_BUCKETS // 2
    logv = (np.log(np.maximum(dist, 1).astype(np.float32) / np.float32(exact))
            / np.float32(math.log(MAX_DISTANCE / exact))).astype(np.float32)
    large = np.minimum(exact + (logv * np.float32(N_BUCKETS - exact)).astype(np.int32), N_BUCKETS - 1)
    return np.where(dist < exact, dist, large).astype(np.int32)


def _bias_lookup(tab_rows, buckets, per_row=False):
    shape = buckets.shape if per_row else (tab_rows.shape[0],) + buckets.shape
    out = jnp.zeros(shape, F32)
    for n in np.unique(buckets):
        col = tab_rows[:, n].reshape((-1,) + (1,) * (len(shape) - 1))
        out = jnp.where(jnp.asarray(buckets == n), col, out)
    return out


def _prompt_bias_tiles(tab):
    t = ATT_TILE
    j = np.arange(t)[:, None]
    i = np.arange(t)[None, :]
    far = tab[:, N_BUCKETS - 1][:, None, None]
    diag = jnp.where(jnp.asarray(i >= j)[None], (_bias_lookup(tab, _t5_bucket_np(i - j)) - far) * LOG2E, NEG_INF)
    prev = (_bias_lookup(tab, _t5_bucket_np(t + i - j)) - far) * LOG2E
    return jnp.stack([diag, prev], axis=1).astype(F32)


def _sample_bias_tiles(tab, h_a, n_new, past_len):
    h_b = tab.shape[0] - h_a
    rows = (h_a + 2 * h_b) * n_new
    r = np.arange(rows)
    qi = (r % n_new)[:, None]
    head = np.where(r < h_a * n_new, r // n_new, h_a + (r - h_a * n_new) // (2 * n_new))
    onehot = jnp.asarray(head[:, None] == np.arange(tab.shape[0])[None, :], F32)
    tab_rows = jnp.dot(onehot, tab, precision=lax.Precision.HIGHEST)
    far = tab_rows[:, N_BUCKETS - 1:]
    t = np.arange(MOBA_BLOCK)[None, :]
    dist_last = past_len + qi - (past_len - MOBA_BLOCK + t)
    last = (_bias_lookup(tab_rows, _t5_bucket_np(dist_last), per_row=True) - far) * LOG2E
    j = np.arange(LANES)[None, :]
    new = jnp.where(jnp.asarray((j <= qi) & (j < n_new)),
                    (_bias_lookup(tab_rows, _t5_bucket_np(qi - j), per_row=True) - far) * LOG2E, NEG_INF)
    return last.astype(F32), new.astype(F32)


def _proj_prompt_kernel(x_ref, w_ref, hq_ref, kat_ref, vat_ref, kb_ref, vb_ref, vat16_ref, vbt16_ref, kmean_ref,
                        *, seg, q_scale, h_b):
    x = x_ref[...].astype(BF16)
    dv = seg // h_b
    for s in range(6):
        h = _dot(x, w_ref[:, s * seg:(s + 1) * seg])
        if s == 0:
            hq_ref[:, 0:seg] = (h * q_scale).astype(BF16)
        elif s == 1:
            hq_ref[:, seg:2 * seg] = h.astype(BF16)
            for n in range(BLOCKS_PER_TILE):
                blk = h[n * MOBA_BLOCK:(n + 1) * MOBA_BLOCK]
                kmean_ref[n] = jnp.sum(blk, axis=0, keepdims=True) * (1.0 / MOBA_BLOCK)
            kat_ref[...] = h.T
        elif s == 2:
            ht = h.T
            vat_ref[...] = ht
            vat16_ref[...] = ht.astype(BF16)
        elif s == 3:
            hq_ref[:, 2 * seg:3 * seg] = (h * q_scale).astype(BF16)
        elif s == 4:
            hq_ref[:, 3 * seg:4 * seg] = h.astype(BF16)
            for hh in range(h_b):
                kb_ref[:, hh, :] = h[:, hh * dv:(hh + 1) * dv]
        else:
            for hh in range(h_b):
                vb_ref[:, hh, :] = h[:, hh * dv:(hh + 1) * dv]
            vbt16_ref[...] = h.T.astype(BF16)


def _proj_prompt(x2, w_in_l, *, batch, seq, h_b):
    tm = ATT_TILE
    m, d = x2.shape
    seg = w_in_l.shape[1] // 6
    nt = seq // tm
    dv = seg // h_b
    row = lambda w: pl.BlockSpec((tm, w), lambda b, i: (b * nt + i, 0))
    feat = pl.BlockSpec((None, seg, tm), lambda b, i: (b, 0, i))
    heads = pl.BlockSpec((tm, h_b, dv), lambda b, i: (b * nt + i, 0, 0))
    tiles = pl.BlockSpec((None, None, seg, tm), lambda b, i: (b, i, 0, 0))
    return pl.pallas_call(
        functools.partial(_proj_prompt_kernel, seg=seg, q_scale=DH ** -0.5 * LOG2E, h_b=h_b),
        grid=(batch, nt),
        in_specs=[row(d), _resident(w_in_l.shape)],
        out_specs=[row(4 * seg), feat, feat, heads, heads, tiles, tiles,
                   pl.BlockSpec((BLOCKS_PER_TILE, 1, seg), lambda b, i: (b * nt + i, 0, 0))],
        out_shape=[jax.ShapeDtypeStruct((m, 4 * seg), BF16),
                   jax.ShapeDtypeStruct((batch, seg, seq), F32), jax.ShapeDtypeStruct((batch, seg, seq), F32),
                   jax.ShapeDtypeStruct((m, h_b, dv), F32), jax.ShapeDtypeStruct((m, h_b, dv), F32),
                   jax.ShapeDtypeStruct((batch, nt, seg, tm), BF16), jax.ShapeDtypeStruct((batch, nt, seg, tm), BF16),
                   jax.ShapeDtypeStruct((m // MOBA_BLOCK, 1, seg), F32)],
        compiler_params=_cparams(("parallel", "parallel")),
    )(x2, w_in_l)


def _proj_sample_kernel(x_ref, w_ref, hq_ref, ka_ref, va_ref, kb_ref, vb_ref, *, seg, q_scale):
    x = x_ref[...].astype(BF16)
    outs = (None, ka_ref, va_ref, None, kb_ref, vb_ref)
    for s in range(6):
        h = _dot(x, w_ref[:, s * seg:(s + 1) * seg])
        if s == 0:
            hq_ref[:, 0:seg] = h * q_scale
        elif s == 3:
            hq_ref[:, seg:2 * seg] = h * q_scale
        else:
            outs[s][...] = h


def _proj_sample(x2, w_in_l, *, tm):
    m, d = x2.shape
    seg = w_in_l.shape[1] // 6
    row = lambda w: pl.BlockSpec((tm, w), lambda i: (i, 0))
    return pl.pallas_call(
        functools.partial(_proj_sample_kernel, seg=seg, q_scale=DH ** -0.5 * LOG2E),
        grid=(m // tm,),
        in_specs=[row(d), _resident(w_in_l.shape)],
        out_specs=[row(2 * seg)] + [row(seg)] * 4,
        out_shape=[jax.ShapeDtypeStruct((m, 2 * seg), F32)] + [jax.ShapeDtypeStruct((m, seg), F32)] * 4,
        compiler_params=_cparams(("parallel",)),
    )(x2, w_in_l)


PREV_TILE_BIAS_FROM = ATT_TILE - MAX_DISTANCE
SCORE_COLS = 256
SOFTMAX_ROWS = 32


def _flash_stages(s_ref, p_ref, bias_ref, bias_shift, vt_ext, m_sc, acc_sc, h, first, bias_from=0):
    t_k, t_q = s_ref.shape
    sub = 8
    state = {}

    def chunk(c):
        rows = slice(c * SOFTMAX_ROWS, (c + 1) * SOFTMAX_ROWS)
        s = s_ref[rows, :]
        if bias_ref is not None and rows.stop > bias_from:
            s = s + bias_ref[rows, :]
        return s if bias_shift is None else s + bias_shift

    def column_max():
        m8 = None
        for c in range(t_k // SOFTMAX_ROWS):
            part = jnp.max(chunk(c).reshape(SOFTMAX_ROWS // sub, sub, t_q), axis=0)
            m8 = part if m8 is None else jnp.maximum(m8, part)
        m_new = jnp.max(m8, axis=0, keepdims=True)
        if not first:
            state['m_old'] = m_sc[h]
            m_new = jnp.maximum(state['m_old'], m_new)
        state['m_new'] = m_new

    def exponentials():
        for c in range(t_k // SOFTMAX_ROWS):
            p_ref[c * SOFTMAX_ROWS:(c + 1) * SOFTMAX_ROWS, :] = jnp.exp2(chunk(c) - state['m_new']).astype(BF16)

    def values():
        pv = _dot(vt_ext(), p_ref[...])
        if first:
            acc_sc[h] = pv
        else:
            acc_sc[h] = jnp.exp2(state['m_old'] - state['m_new']) * acc_sc[h] + pv
        m_sc[h] = state['m_new']

    return [column_max, exponentials, values]


def _interleave(score_parts, softmax_stages):
    for i in range(max(len(score_parts), len(softmax_stages))):
        if i < len(score_parts):
            score_parts[i]()
        if i < len(softmax_stages):
            softmax_stages[i]()


def _score_buffers(t):
    return [pltpu.VMEM((2, t, t), F32)] * 2 + [pltpu.VMEM((2, t, t), BF16)] * 2


def _flash_sweep(qi, scores, consume):
    near = jnp.maximum(qi - 1, 0)
    n_far = near
    last = jnp.maximum(n_far - 1, 0)
    _interleave(scores(qi, 0), [])
    _interleave(scores(near, 1), consume(qi, 0, 0, None, True))
    _interleave(scores(0, 0), consume(near, 1, 1, jnp.where(qi == 0, NEG_INF, 0.0), False))

    def pair(p, carry):
        j = 2 * p
        _interleave(scores(jnp.minimum(j + 1, last), 1), consume(j, 0, None, None, False))
        _interleave(scores(jnp.minimum(j + 2, last), 0), consume(j + 1, 1, None, None, False))
        return carry

    lax.fori_loop(0, n_far // 2, pair, 0)

    @pl.when(n_far % 2 == 1)
    def _():
        _interleave([], consume(last, 0, None, None, False))


def _moba_prompt_kernel(q_ref, k_ref, vt_ref, kmean_ref, bias_ref, o_ref, m_sc, acc_sc, s0_sc, s1_sc, p0_sc, p1_sc,
                        qx_sc):
    t = ATT_TILE
    qi = pl.program_id(2)
    qt = q_ref[...].astype(F32).T
    row = lax.broadcasted_iota(jnp.int32, (LANES, t), 0)
    head0 = row < DH
    q_heads = (jnp.where(head0, qt, 0.0).astype(BF16), jnp.where(head0, 0.0, qt).astype(BF16))
    kmean = kmean_ref[...].astype(BF16)
    cur = qi * BLOCKS_PER_TILE + lax.broadcasted_iota(jnp.int32, (LANES, t), 1) // MOBA_BLOCK

    def block_mask(qh):
        gate = jnp.where(row < cur, _dot(kmean, qh), NEG_INF)
        chosen = row == cur
        for j in range(MOBA_TOPK):
            best = jnp.max(gate, axis=0, keepdims=True)
            idx = jnp.min(jnp.where(gate == best, row, 2 * LANES), axis=0, keepdims=True)
            hit = row == idx
            chosen = jnp.logical_or(chosen, jnp.logical_and(hit, cur > j))
            gate = jnp.where(hit, -3e38, gate)
        return jnp.where(chosen, 0.0, NEG_INF).astype(BF16)

    for h in range(2):
        qx_sc[h] = jnp.concatenate([q_heads[h], block_mask(q_heads[h])], axis=0)
    lane = lax.broadcasted_iota(jnp.int32, (1, LANES), 1)
    ones = jnp.ones((BF16_ROWS, t), BF16)

    s_bufs, p_bufs = (s0_sc, s1_sc), (p0_sc, p1_sc)

    def scores(kj, slot):
        def part(h, c):
            rows = pl.ds(pl.multiple_of(kj * t, t), t)
            onehot = jnp.concatenate(
                [jnp.broadcast_to(jnp.where(lane == kj * BLOCKS_PER_TILE + n, 1.0, 0.0).astype(BF16),
                                  (MOBA_BLOCK, LANES)) for n in range(BLOCKS_PER_TILE)], axis=0)
            k_ext = jnp.concatenate([k_ref[rows, :], onehot], axis=1)
            cols = slice(c * SCORE_COLS, (c + 1) * SCORE_COLS)
            s_bufs[slot][h, :, cols] = _dot(k_ext, qx_sc[h, :, cols])
        return [functools.partial(part, h, c) for h in range(2) for c in range(t // SCORE_COLS)]

    def consume(kj, slot, bias_idx, bias_shift, first):
        stages = []
        for h in range(2):
            vt_ext = lambda h=h: jnp.concatenate([vt_ref[kj, h * DH:(h + 1) * DH, :], ones], axis=0)
            bias = None if bias_idx is None else bias_ref.at[h, bias_idx]
            stages += _flash_stages(s_bufs[slot].at[h], p_bufs[slot].at[h], bias, bias_shift, vt_ext, m_sc,
                                    acc_sc, h, first, bias_from=PREV_TILE_BIAS_FROM if bias_idx == 1 else 0)
        return stages

    _flash_sweep(qi, scores, consume)
    out = [acc_sc[h, 0:DH] / acc_sc[h, DH:DH + 1] for h in range(2)]
    o_ref[...] = jnp.concatenate(out, axis=0).T.astype(o_ref.dtype)


def _moba_prompt(hq, vat16, kmean_pad, bias_a, *, batch, seq, h_a):
    t = ATT_TILE
    pairs = h_a * DH // LANES
    seg_blocks = hq.shape[1] // 4 // LANES
    nq = seq // t
    return pl.pallas_call(
        _moba_prompt_kernel,
        grid=(batch, pairs, nq),
        in_specs=[
            pl.BlockSpec((t, LANES), lambda b, j, i: (b * nq + i, j)),
            pl.BlockSpec((seq, LANES), lambda b, j, i: (b, seg_blocks + j)),
            pl.BlockSpec((None, nq, LANES, t), lambda b, j, i: (b, 0, j, 0)),
            pl.BlockSpec((None, LANES, LANES), lambda b, j, i: (b, 0, j)),
            pl.BlockSpec((2, 2, t, t), lambda b, j, i: (j, 0, 0, 0)),
        ],
        out_specs=pl.BlockSpec((t, LANES), lambda b, j, i: (b * nq + i, j)),
        out_shape=jax.ShapeDtypeStruct((batch * seq, pairs * LANES), BF16),
        scratch_shapes=([pltpu.VMEM((2, 1, t), F32), pltpu.VMEM((2, DH + BF16_ROWS, t), F32)] + _score_buffers(t)
                        + [pltpu.VMEM((2, 2 * LANES, t), BF16)]),
        compiler_params=_cparams(("parallel", "parallel", "arbitrary")),
    )(hq, hq, vat16, kmean_pad, bias_a)


def _diff_prompt_kernel(lam_ref, q_ref, k_ref, vt_ref, bias_ref, g_ref, o_ref, m_sc, acc_sc,
                        s0_sc, s1_sc, p0_sc, p1_sc, qx_sc, *, out_scale):
    t = ATT_TILE
    dv = LANES
    qi = pl.program_id(2)
    qt = q_ref[...].astype(F32).T
    row = lax.broadcasted_iota(jnp.int32, (LANES, t), 0)
    half0 = row < DH
    qx_sc[0] = jnp.where(half0, qt, 0.0).astype(BF16)
    qx_sc[1] = jnp.where(half0, 0.0, qt).astype(BF16)
    ones = jnp.ones((BF16_ROWS, t), BF16)

    s_bufs, p_bufs = (s0_sc, s1_sc), (p0_sc, p1_sc)

    def scores(kj, slot):
        def part(h, c):
            k = k_ref[pl.ds(pl.multiple_of(kj * t, t), t), :]
            cols = slice(c * SCORE_COLS, (c + 1) * SCORE_COLS)
            s_bufs[slot][h, :, cols] = _dot(k, qx_sc[h, :, cols])
        return [functools.partial(part, h, c) for h in range(2) for c in range(t // SCORE_COLS)]

    def consume(kj, slot, bias_idx, bias_shift, first):
        vt_ext = lambda: jnp.concatenate([vt_ref[kj], ones], axis=0)
        bias = None if bias_idx is None else bias_ref.at[bias_idx]
        stages = []
        for h in range(2):
            stages += _flash_stages(s_bufs[slot].at[h], p_bufs[slot].at[h], bias, bias_shift, vt_ext, m_sc,
                                    acc_sc, h, first, bias_from=PREV_TILE_BIAS_FROM if bias_idx == 1 else 0)
        return stages

    _flash_sweep(qi, scores, consume)
    o = (acc_sc[0, 0:dv] / acc_sc[0, dv:dv + 1]
         - lam_ref[0] * (acc_sc[1, 0:dv] / acc_sc[1, dv:dv + 1]))
    rms = lax.rsqrt(jnp.mean(o * o, axis=0, keepdims=True) + RMS_EPS)
    o_ref[...] = (o * rms * g_ref[...] * out_scale).T.astype(o_ref.dtype)


def _diff_prompt(lam, hq, vbt16, bias_b, subln_g, *, batch, seq, h_b, lam_init):
    t = ATT_TILE
    seg_blocks = hq.shape[1] // 4 // LANES
    nq = seq // t
    g_spread = jnp.broadcast_to(subln_g.reshape(LANES, 1), (LANES, t))
    return pl.pallas_call(
        functools.partial(_diff_prompt_kernel, out_scale=1.0 - lam_init),
        grid=(batch, h_b, nq),
        in_specs=[
            pl.BlockSpec(memory_space=pltpu.SMEM),
            pl.BlockSpec((t, LANES), lambda b, h, i: (b * nq + i, 2 * seg_blocks + h)),
            pl.BlockSpec((seq, LANES), lambda b, h, i: (b, 3 * seg_blocks + h)),
            pl.BlockSpec((None, nq, LANES, t), lambda b, h, i: (b, 0, h, 0)),
            pl.BlockSpec((None, 2, t, t), lambda b, h, i: (h, 0, 0, 0)),
            pl.BlockSpec((LANES, t), lambda b, h, i: (0, 0)),
        ],
        out_specs=pl.BlockSpec((t, LANES), lambda b, h, i: (b * nq + i, h)),
        out_shape=jax.ShapeDtypeStruct((batch * seq, h_b * LANES), BF16),
        scratch_shapes=([pltpu.VMEM((2, 1, t), F32), pltpu.VMEM((2, LANES + BF16_ROWS, t), F32)] + _score_buffers(t)
                        + [pltpu.VMEM((2, LANES, t), BF16)]),
        compiler_params=_cparams(("parallel", "parallel", "arbitrary")),
    )(lam, hq, hq, vbt16, bias_b, g_spread)


def _softmax_tile(s):
    m = jnp.max(s, axis=1, keepdims=True)
    p = jnp.exp2(s - m)
    return m, p, jnp.sum(p, axis=1, keepdims=True)


def _sample_attn_kernel(pt_ref, lam_ref, hq_ref, ka_new_ref, va_new_ref, kb_new_ref, vb_new_ref,
                        bias_last_ref, bias_new_ref, g_ref, *refs,
                        n_steps, blocks_per_step, n_new, h_a, h_b, page, out_scale):
    del pt_ref
    n_pages = 2 * blocks_per_step
    page_refs = refs[:4 * n_pages]
    ya_ref, yb_ref = refs[4 * n_pages:4 * n_pages + 2]
    w_sc, gate_sc, m_sc, l_sc, o_sc = refs[4 * n_pages + 2:]
    g = pl.program_id(1)
    w_a = h_a * DH
    rows_a = h_a * n_new
    rows = rows_a + 2 * h_b * n_new
    n_past = n_steps * blocks_per_step

    at_step = (lambda step: pl.when(g == step)) if n_steps > 1 else (lambda step: (lambda body: body()))

    @at_step(0)
    def _():
        tiled = jnp.concatenate([hq_ref[...]] * (rows // n_new), axis=0)
        r = lax.broadcasted_iota(jnp.int32, tiled.shape, 0)
        c = lax.broadcasted_iota(jnp.int32, tiled.shape, 1)
        w_sc[...] = jnp.where(c // DH == r // n_new, tiled, 0.0).astype(BF16)

    def partial_block(n, s, bias, pv_a, pv_b):
        spread = lambda col: jnp.broadcast_to(col, (rows, LANES))
        gate_sc[n] = spread(jnp.sum(s, axis=1, keepdims=True))
        m, p, l = _softmax_tile(s + bias)
        p = p.astype(BF16)
        o_sc[n, 0:rows_a, :] = pv_a(p[0:rows_a])
        o_sc[n, rows_a:rows, :] = pv_b(p[rows_a:rows])
        m_sc[n] = spread(m)
        l_sc[n] = spread(l)

    def heads_on_lanes(ref):
        return jnp.concatenate([ref[pl.ds(h, page, stride=h_b), :] for h in range(h_b)], axis=1).astype(BF16)

    for u in range(blocks_per_step):
        n = g * blocks_per_step + u
        pages = [page_refs[4 * (2 * u + v):4 * (2 * u + v) + 4] for v in range(2)]
        ka_t = [pg[0][...].reshape(w_a, page).astype(BF16) for pg in pages]
        va_t = [pg[1][...].reshape(w_a, page).astype(BF16) for pg in pages]
        kb = [heads_on_lanes(pg[2]) for pg in pages]
        vb = [heads_on_lanes(pg[3]) for pg in pages]
        q_a = w_sc[0:rows_a, 0:w_a]
        q_b = w_sc[rows_a:rows, w_a:2 * w_a]
        s = jnp.concatenate([
            jnp.concatenate([_dot(q_a, k) for k in ka_t], axis=1),
            jnp.concatenate([_dot_nt(q_b, k) for k in kb], axis=1)], axis=0)
        pv_a = lambda p: sum(_dot_nt(p[:, v * page:(v + 1) * page], va_t[v]) for v in range(2))
        pv_b = lambda p: sum(_dot(p[:, v * page:(v + 1) * page], vb[v]) for v in range(2))
        is_last = (n == n_past - 1).astype(F32)
        partial_block(n, s, bias_last_ref[...] * is_last, pv_a, pv_b)

    @at_step(n_steps - 1)
    def _():
        pad =jnp.zeros((LANES - n_new, w_a), F32)
        padded = lambda ref: jnp.concatenate([ref[...], pad], axis=0).astype(BF16)
        k_cat = jnp.concatenate([padded(ka_new_ref), padded(kb_new_ref)], axis=1)
        va, vb = padded(va_new_ref), padded(vb_new_ref)
        partial_block(n_past, _dot_nt(w_sc[...], k_cat), bias_new_ref[...],
                      lambda p: _dot(p, va), lambda p: _dot(p, vb))

        rs_a = slice(0, rows_a)
        gates = [gate_sc[n, rs_a, :] for n in range(n_past)]
        kept = []
        for n in range(n_past):
            ahead = jnp.zeros((rows_a, LANES), F32)
            for j in range(n_past):
                if j != n:
                    ahead = ahead + jnp.where(gates[j] >= gates[n] if j < n else gates[j] > gates[n], 1.0, 0.0)
            kept.append(ahead < min(MOBA_TOPK, n_past))

        def combine(rs, kept):
            blocks = range(n_past + 1)
            on = lambda n: kept is None or n == n_past or kept[n]
            m_fin = None
            for n in blocks:
                m_n = m_sc[n, rs, :] if on(n) is True else jnp.where(on(n), m_sc[n, rs, :], NEG_INF)
                m_fin = m_n if m_fin is None else jnp.maximum(m_fin, m_n)
            l_fin, o = None, None
            for n in blocks:
                wgt = jnp.exp2(m_sc[n, rs, :] - m_fin)
                if on(n) is not True:
                    wgt = jnp.where(on(n), wgt, 0.0)
                l_n = wgt * l_sc[n, rs, :]
                o_n = jnp.concatenate([wgt] * (w_a // LANES), axis=1) * o_sc[n, rs, :]
                l_fin, o = (l_n, o_n) if o is None else (l_fin + l_n, o + o_n)
            return o / jnp.concatenate([l_fin] * (w_a // LANES), axis=1)

        o = jnp.concatenate([combine(rs_a, kept), combine(slice(rows_a, rows), None)], axis=0)

        col = lax.broadcasted_iota(jnp.int32, (n_new, w_a), 1)
        ya = jnp.zeros((n_new, w_a), F32)
        for h in range(h_a):
            ya = ya + jnp.where(col // DH == h, o[h * n_new:(h + 1) * n_new], 0.0)
        ya_ref[...] = ya
        lam = lam_ref[0]
        dv = w_a // h_b
        outs = []
        for h in range(h_b):
            r0 = rows_a + 2 * h * n_new
            d = (o[r0:r0 + n_new, h * dv:(h + 1) * dv]
                 - lam * o[r0 + n_new:r0 + 2 * n_new, h * dv:(h + 1) * dv])
            rms = lax.rsqrt(jnp.mean(d * d, axis=1, keepdims=True) + RMS_EPS)
            outs.append(d * rms * g_ref[...] * out_scale)
        yb_ref[...] = jnp.concatenate(outs, axis=1)


def _sample_attn(page_table, lam, hq, ka, va, kb, vb, caches, layer, bias_last, bias_new, subln_g,
                 *, n_new, h_a, h_b, lam_init, blocks_per_step):
    dec_batch, n_pages = page_table.shape
    page = caches[0].shape[2]
    assert 2 * page == MOBA_BLOCK and n_pages % (2 * blocks_per_step) == 0
    w_a = h_a * DH
    rows = (h_a + 2 * h_b) * n_new
    n_steps = n_pages // (2 * blocks_per_step)
    pps = 2 * blocks_per_step
    n_blocks = n_pages // 2 + 1
    caches = [caches[0].transpose(0, 1, 3, 4, 2), caches[1].transpose(0, 1, 3, 4, 2),
              caches[2].reshape(caches[2].shape[:2] + (page * h_b, -1)),
              caches[3].reshape(caches[3].shape[:2] + (page * h_b, -1))]

    def page_spec(v, c):
        shape = (None, None) + c.shape[2:]
        zeros = (0,) * (c.ndim - 2)
        return pl.BlockSpec(shape, lambda b, g, pt: (layer, pt[b * n_pages + g * pps + v]) + zeros)

    new_spec = pl.BlockSpec((n_new, w_a), lambda b, g, pt: (b, 0))
    in_specs = [
        pl.BlockSpec(memory_space=pltpu.SMEM),
        pl.BlockSpec((n_new, hq.shape[1]), lambda b, g, pt: (b, 0)),
        new_spec, new_spec, new_spec, new_spec,
        pl.BlockSpec((rows, MOBA_BLOCK), lambda b, g, pt: (0, 0)),
        pl.BlockSpec((rows, LANES), lambda b, g, pt: (0, 0)),
        pl.BlockSpec((1, w_a // h_b), lambda b, g, pt: (0, 0)),
    ]
    operands = [lam, hq, ka, va, kb, vb, bias_last, bias_new, subln_g]
    for v in range(pps):
        for c in caches:
            in_specs.append(page_spec(v, c))
            operands.append(c)
    grid_spec = pltpu.PrefetchScalarGridSpec(
        num_scalar_prefetch=1,
        grid=(dec_batch, n_steps),
        in_specs=in_specs,
        out_specs=[new_spec, new_spec],
        scratch_shapes=[
            pltpu.VMEM((rows, 2 * w_a), BF16),
            pltpu.VMEM((n_blocks, rows, LANES), F32), pltpu.VMEM((n_blocks, rows, LANES), F32),
            pltpu.VMEM((n_blocks, rows, LANES), F32),
            pltpu.VMEM((n_blocks, rows, w_a), F32),
        ],
    )
    return pl.pallas_call(
        functools.partial(_sample_attn_kernel, n_steps=n_steps, blocks_per_step=blocks_per_step, n_new=n_new,
                          h_a=h_a, h_b=h_b, page=page, out_scale=1.0 - lam_init),
        grid_spec=grid_spec,
        out_shape=[jax.ShapeDtypeStruct((dec_batch * n_new, w_a), F32)] * 2,
        compiler_params=_cparams(("parallel", "arbitrary")),
    )(page_table.reshape(-1), *operands)


def _layernorm(x, g, b):
    mu = jnp.mean(x, axis=-1, keepdims=True)
    xc = x - mu
    var = jnp.mean(xc * xc, axis=-1, keepdims=True)
    return xc * lax.rsqrt(var + LN_EPS) * g + b


def _mix_kernel(x_ref, ya_ref, yb_ref, wg_ref, bg_ref, wa_ref, wb_ref, wo_ref, g_ref, b_ref, o_ref, *, alpha):
    x = x_ref[...]
    d = x.shape[1]
    gates = jax.nn.sigmoid(_dot(x.astype(BF16), wg_ref[...]) + bg_ref[...])
    z = (gates[:, :d] * _dot(ya_ref[...].astype(BF16), wa_ref[...])
         + gates[:, d:] * _dot(yb_ref[...].astype(BF16), wb_ref[...]))
    mix = _dot(z.astype(BF16), wo_ref[...])
    o_ref[...] = _layernorm(alpha * x + mix, g_ref[...], b_ref[...])


def _mix(x2, ya, yb, wg, bg, wa, wb, wo, g, b, *, tm, alpha):
    m, d = x2.shape
    w = ya.shape[1]
    row = lambda c: pl.BlockSpec((tm, c), lambda i: (i, 0))
    return pl.pallas_call(
        functools.partial(_mix_kernel, alpha=alpha),
        grid=(m // tm,),
        in_specs=[row(d), row(w), row(w), _resident(wg.shape), _resident(bg.shape), _resident(wa.shape),
                  _resident(wb.shape), _resident(wo.shape), _resident(g.shape), _resident(b.shape)],
        out_specs=row(d),
        out_shape=jax.ShapeDtypeStruct((m, d), F32),
        compiler_params=_cparams(("parallel",)),
    )(x2, ya, yb, wg, bg, wa, wb, wo, g, b)


def _gelu_tanh(x):
    return 0.5 * x * (1.0 + jnp.tanh(math.sqrt(2.0 / math.pi) * (x + 0.044715 * (x * x * x))))


def _ffn_kernel(x_ref, p_ref, prev_ref, wu_ref, cw_ref, cb_ref, wd_ref, g_ref, b_ref, wpg_ref, wp_ref,
                o_ref, st_ref, carry_sc, act_sc, u_sc, *, alpha, seq_rows, n_chunks):
    x = x_ref[...]
    tm, d = x.shape
    xb = x.astype(BF16)
    cols = lambda c: slice(c * FFN_CHUNK, (c + 1) * FFN_CHUNK)
    long_seq = seq_rows is None
    if long_seq:
        first = pl.program_id(1) == 0
    else:
        groups = tm // seq_rows
        sub = lax.broadcasted_iota(jnp.int32, (groups, seq_rows, FFN_CHUNK), 1)

    def conv(c, u):
        cw = cw_ref[:, cols(c)]
        u3 = u.reshape(groups, seq_rows, FFN_CHUNK)
        hist = prev_ref[:, :, c * FFN_CHUNK:(c + 1) * FFN_CHUNK]
        back = lambda k: pltpu.roll(u, k, 0).reshape(groups, seq_rows, FFN_CHUNK)
        u1 = jnp.where(sub == 0, hist[:, 1:2], back(1))
        u2 = jnp.where(sub == 0, hist[:, 0:1], jnp.where(sub == 1, hist[:, 1:2], back(2)))
        st_ref[:, :, c * FFN_CHUNK:(c + 1) * FFN_CHUNK] = u3[:, seq_rows - 2:seq_rows]
        out = cw[0:1] * u2 + cw[1:2] * u1 + cw[2:3] * u3 + cb_ref[:, cols(c)]
        return out.reshape(tm, FFN_CHUNK)

    def up_long(c, slot):
        for half in range(2):
            cc = half * n_chunks + c
            u = _dot(xb, wu_ref[:, cols(cc)])
            u_sc[slot, half, CONV_HIST - (CONV_W - 1):CONV_HIST, :] = carry_sc[cc]
            u_sc[slot, half, CONV_HIST:CONV_HIST + tm, :] = u
            carry_sc[cc] = u[tm - (CONV_W - 1):tm]
            st_ref[0, :, cc * FFN_CHUNK:(cc + 1) * FFN_CHUNK] = u[tm - (CONV_W - 1):tm]

    def act_long(c, slot):
        for r0 in range(0, tm, FFN_ROWS):
            halves = []
            for half in range(2):
                cc = half * n_chunks + c
                cw = cw_ref[:, cols(cc)]
                out = cb_ref[:, cols(cc)]
                for j in range(CONV_W):
                    start = CONV_HIST + r0 - (CONV_W - 1 - j)
                    out = out + cw[j:j + 1] * u_sc[slot, half, start:start + FFN_ROWS, :]
                halves.append(out)
            act_sc[r0:r0 + FFN_ROWS, c * FFN_CHUNK:(c + 1) * FFN_CHUNK] = (
                _gelu_tanh(halves[0]) * halves[1]).astype(BF16)

    if long_seq:
        @pl.when(first)
        def _():
            for cc in range(2 * n_chunks):
                carry_sc[cc] = prev_ref[0, :, cc * FFN_CHUNK:(cc + 1) * FFN_CHUNK]

        up_long(0, 0)
        for c in range(n_chunks):
            if c + 1 < n_chunks:
                up_long(c + 1, (c + 1) % 2)
            act_long(c, c % 2)
    else:
        for c in range(n_chunks):
            c_gate = conv(c, _dot(xb, wu_ref[:, cols(c)]))
            c_val = conv(n_chunks + c, _dot(xb, wu_ref[:, cols(n_chunks + c)]))
            act_sc[:, c * FFN_CHUNK:(c + 1) * FFN_CHUNK] = (_gelu_tanh(c_gate) * c_val).astype(BF16)
    f = _dot(act_sc[...], wd_ref[...])
    y = _layernorm(alpha * x + f, g_ref[...], b_ref[...])
    gate = jax.nn.sigmoid(_dot(y.astype(BF16), wpg_ref[...]))
    o_ref[...] = y + gate * _dot(p_ref[...].astype(BF16), wp_ref[...])


def _ffn(x2, p2, prev, wu, cw, cb, wd, g, b, wpg, wp, *, tm, alpha, batch, seq_rows):
    m, d = x2.shape
    d_up = prev.shape[-1]
    n_chunks = wu.shape[1] // 2 // FFN_CHUNK
    d_ple = p2.shape[1]
    if seq_rows is None:
        tiles = m // batch // tm
        grid = (batch, tiles)
        row = lambda c: pl.BlockSpec((tm, c), lambda bb, i: (bb * tiles + i, 0))
        st_spec = pl.BlockSpec((1, CONV_W - 1, d_up), lambda bb, i: (bb, 0, 0))
        sem = ("parallel", "arbitrary")
    else:
        grid = (1, m // tm)
        row = lambda c: pl.BlockSpec((tm, c), lambda bb, i: (i, 0))
        st_spec = pl.BlockSpec((tm // seq_rows, CONV_W - 1, d_up), lambda bb, i: (i, 0, 0))
        sem = ("arbitrary", "arbitrary")
    n_seq = prev.shape[0]
    return pl.pallas_call(
        functools.partial(_ffn_kernel, alpha=alpha, seq_rows=seq_rows, n_chunks=n_chunks),
        grid=grid,
        in_specs=[row(d), row(d_ple), st_spec, _resident(wu.shape), _resident(cw.shape), _resident(cb.shape),
                  _resident(wd.shape), _resident(g.shape), _resident(b.shape), _resident(wpg.shape),
                  _resident(wp.shape)],
        out_specs=[row(d), st_spec],
        out_shape=[jax.ShapeDtypeStruct((m, d), F32), jax.ShapeDtypeStruct((n_seq, CONV_W - 1, d_up), F32)],
        scratch_shapes=[pltpu.VMEM((2 * n_chunks, CONV_W - 1, FFN_CHUNK), F32),
                        pltpu.VMEM((tm, n_chunks * FFN_CHUNK), BF16),
                        pltpu.VMEM((2, 2, CONV_HIST + tm, FFN_CHUNK), F32)],
        compiler_params=_cparams(sem),
    )(x2, p2, prev, wu, cw, cb, wd, g, b, wpg, wp)


def _layer_weights(l, w):
    return {
        'w_in': w['w_in'][l].astype(BF16),
        'w_gate': w['w_gate'][l].astype(BF16),
        'b_gate': w['b_gate'][l][None, :],
        'w_branch_a': w['w_branch_a'][l].astype(BF16),
        'w_branch_b': w['w_branch_b'][l].astype(BF16),
        'w_out': w['w_out'][l].astype(BF16),
        'ln1_g': w['ln1_g'][l][None, :], 'ln1_b': w['ln1_b'][l][None, :],
        'w_up': w['w_up'][l].astype(BF16),
        'conv_w': w['conv_w'][l],
        'conv_b': w['conv_b'][l][None, :],
        'w_down': w['w_down'][l].astype(BF16),
        'ln2_g': w['ln2_g'][l][None, :], 'ln2_b': w['ln2_b'][l][None, :],
        'w_ple_gate': w['w_ple_gate'][l].astype(BF16),
        'w_ple': w['w_ple'][l].astype(BF16),
        'subln_g': w['diff_subln_g'][l][None, :],
        'lam': (jnp.exp(jnp.sum(w['diff_lq1'][l] * w['diff_lk1'][l]))
                - jnp.exp(jnp.sum(w['diff_lq2'][l] * w['diff_lk2'][l]))
                + (0.8 - 0.6 * math.exp(-0.3 * l))).reshape(1).astype(F32),
    }


def _tail(x2, ya, yb, p2, prev, wl, *, alpha, tm_mix, tm_ffn, batch, seq_rows):
    x1 = _mix(x2, ya, yb, wl['w_gate'], wl['b_gate'], wl['w_branch_a'], wl['w_branch_b'], wl['w_out'],
              wl['ln1_g'], wl['ln1_b'], tm=tm_mix, alpha=alpha)
    return _ffn(x1, p2, prev, wl['w_up'], wl['conv_w'], wl['conv_b'], wl['w_down'], wl['ln2_g'], wl['ln2_b'],
                wl['w_ple_gate'], wl['w_ple'], tm=tm_ffn, alpha=alpha, batch=batch, seq_rows=seq_rows)


def _forward(x_prompt, x_sample, cache_a_k, cache_a_v, cache_b_k, cache_b_v, state_conv, page_table,
             p_prompt, p_sample, w, *, tm_prompt=256, tm_sample=256, blocks_per_step=8):
    depth = w['w_in'].shape[0]
    batch, seq, d = x_prompt.shape
    dec_batch, n_new, _ = x_sample.shape
    h_a, dh = cache_a_k.shape[3], cache_a_k.shape[4]
    h_b, dv = cache_b_v.shape[3], cache_b_v.shape[4]
    assert dh == DH and cache_b_k.shape[4] == 2 * DH and dv == LANES and seq % ATT_TILE == 0
    d_up = state_conv.shape[-1]
    past_len = page_table.shape[1] * cache_a_k.shape[2]
    alpha = (2 * depth) ** 0.25
    tab = w['rel_bias'].T
    bias_prompt = _prompt_bias_tiles(tab)
    bias_last, bias_new = _sample_bias_tiles(tab, h_a, n_new, past_len)
    caches = (cache_a_k, cache_a_v, cache_b_k, cache_b_v)

    xp = x_prompt.reshape(batch * seq, d)
    xs = x_sample.reshape(dec_batch * n_new, d)
    conv0 = jnp.zeros((batch, CONV_W - 1, d_up), F32)
    rows_p, rows_s = [], []
    for l in range(depth):
        wl = _layer_weights(l, w)
        lam_init = 0.8 - 0.6 * math.exp(-0.3 * l)

        hq, kat, vat, kb, vb, vat16, vbt16, kmean = _proj_prompt(xp, wl['w_in'], batch=batch, seq=seq, h_b=h_b)
        nb = seq // MOBA_BLOCK
        kmean_pad = jnp.pad(kmean.reshape(batch, nb, h_a * DH), ((0, 0), (0, LANES - nb), (0, 0)))
        ya = _moba_prompt(hq, vat16, kmean_pad, bias_prompt[:h_a], batch=batch, seq=seq, h_a=h_a)
        yb = _diff_prompt(wl['lam'], hq, vbt16, bias_prompt[h_a:], wl['subln_g'], batch=batch, seq=seq, h_b=h_b,
                          lam_init=lam_init)
        xp, conv_p = _tail(xp, ya, yb, p_prompt[l].reshape(batch * seq, -1), conv0, wl, alpha=alpha,
                           tm_mix=tm_prompt, tm_ffn=tm_prompt, batch=batch, seq_rows=None)
        token_major = lambda a: a.reshape(batch, h_a, DH, seq).transpose(0, 3, 1, 2)
        rows_p.append((token_major(kat), token_major(vat), kb.reshape(batch, seq, h_b, 2 * DH),
                       vb.reshape(batch, seq, h_b, dv), conv_p))

        hq, ka, va, kb, vb = _proj_sample(xs, wl['w_in'], tm=tm_sample)
        ya, yb = _sample_attn(page_table, wl['lam'], hq, ka, va, kb, vb, caches, l, bias_last, bias_new,
                              wl['subln_g'], n_new=n_new, h_a=h_a, h_b=h_b, lam_init=lam_init,
                              blocks_per_step=blocks_per_step)
        xs, conv_s = _tail(xs, ya, yb, p_sample[l].reshape(dec_batch * n_new, -1), state_conv[l], wl,
                           alpha=alpha, tm_mix=tm_sample, tm_ffn=tm_sample, batch=1, seq_rows=n_new)
        rows_s.append((ka.reshape(dec_batch, n_new, h_a, DH), va.reshape(dec_batch, n_new, h_a, DH),
                       kb.reshape(dec_batch, n_new, h_b, 2 * DH), vb.reshape(dec_batch, n_new, h_b, dv), conv_s))

    stack = lambda rows, i: jnp.stack([r[i] for r in rows])
    return ((xp.reshape(batch, seq, d), xs.reshape(dec_batch, n_new, d))
            + tuple(stack(rows_p, i) for i in range(5)) + tuple(stack(rows_s, i) for i in range(5)))


def kernel(x_prompt, x_sample, cache_a_k, cache_a_v, cache_b_k, cache_b_v, state_conv, page_table, p_prompt, p_sample, w_in, w_gate, b_gate, w_branch_a, w_branch_b, w_out, rel_bias, diff_lq1, diff_lk1, diff_lq2, diff_lk2, diff_subln_g, ln1_g, ln1_b, w_up, conv_w, conv_b, w_down, ln2_g, ln2_b, w_ple, w_ple_gate):
    w = {'w_in': w_in, 'w_gate': w_gate, 'b_gate': b_gate, 'w_branch_a': w_branch_a,
         'w_branch_b': w_branch_b, 'w_out': w_out, 'rel_bias': rel_bias, 'diff_lq1': diff_lq1,
         'diff_lk1': diff_lk1, 'diff_lq2': diff_lq2, 'diff_lk2': diff_lk2, 'diff_subln_g': diff_subln_g,
         'ln1_g': ln1_g, 'ln1_b': ln1_b, 'w_up': w_up, 'conv_w': conv_w, 'conv_b': conv_b,
         'w_down': w_down, 'ln2_g': ln2_g, 'ln2_b': ln2_b, 'w_ple': w_ple, 'w_ple_gate': w_ple_gate}
    return _forward(x_prompt, x_sample, cache_a_k, cache_a_v, cache_b_k, cache_b_v, state_conv, page_table,
                    p_prompt, p_sample, w)
```

```python
import functools
import math

import numpy as np
import jax
import jax.numpy as jnp
from jax import lax
from jax.experimental import pallas as pl
from jax.experimental.pallas import tpu as pltpu

F32 = jnp.float32
BF16 = jnp.bfloat16

DH = 64
MOBA_BLOCK = 256
MOBA_TOPK = 3
N_BUCKETS = 32
MAX_DISTANCE = 128
CONV_W = 3
LN_EPS = 1e-5
RMS_EPS = 1e-5
NEG_INF = -1e30
LOG2E = math.log2(math.e)
LANES = 128
SUBLANES = 8
BF16_ROWS = 16
VMEM_LIMIT = 56 * 1024 * 1024

ATT_TILE = 2 * MOBA_BLOCK
BLOCKS_PER_TILE = ATT_TILE // MOBA_BLOCK
FFN_CHUNK = 256
FFN_ROWS = 64
CONV_HIST = 8


def _cparams(sem):
    return pltpu.CompilerParams(dimension_semantics=sem, vmem_limit_bytes=VMEM_LIMIT)


def _dot(a, b):
    return jnp.dot(a, b, preferred_element_type=F32)


def _dot_nt(a, b):
    return lax.dot_general(a, b, (((1,), (1,)), ((), ())), preferred_element_type=F32)


def _resident(shape):
    return pl.BlockSpec(shape, lambda *_: (0,) * len(shape), pipeline_mode=pl.Buffered(1))


def _t5_bucket_np(dist):
    dist = np.maximum(dist, 0)
    exact = N_BUCKETS // 2
    logv = (np.log(np.maximum(dist, 1).astype(np.float32) / np.float32(exact))
            / np.float32(math.log(MAX_DISTANCE / exact))).astype(np.float32)
    large = np.minimum(exact + (logv * np.float32(N_BUCKETS - exact)).astype(np.int32), N_BUCKETS - 1)
    return np.where(dist < exact, dist, large).astype(np.int32)


def _bias_lookup(tab_rows, buckets, per_row=False):
    shape = buckets.shape if per_row else (tab_rows.shape[0],) + buckets.shape
    out = jnp.zeros(shape, F32)
    for n in np.unique(buckets):
        col = tab_rows[:, n].reshape((-1,) + (1,) * (len(shape) - 1))
        out = jnp.where(jnp.asarray(buckets == n), col, out)
    return out


def _prompt_bias_tiles(tab):
    t = ATT_TILE
    h = tab.shape[0]
    far = tab[:, N_BUCKETS - 1:]
    dist = np.arange(-(t - 1), 2 * t)
    by_dist = jnp.where(jnp.asarray(dist >= 0)[None], (_bias_lookup(tab, _t5_bucket_np(dist)) - far) * LOG2E,
                        NEG_INF)
    flat = jnp.tile(jnp.pad(by_dist, ((0, 0), (0, 1))), (1, t))[:, :t * (3 * t - 1)]
    both = flat.reshape(h, t, 3 * t - 1)[:, :, t - 1:]
    return jnp.stack([both[:, :, :t], both[:, :, t:]], axis=1).astype(F32)


def _sample_bias_tiles(tab, h_a, n_new, past_len):
    h_b = tab.shape[0] - h_a
    rows = (h_a + 2 * h_b) * n_new
    r = np.arange(rows)
    qi = (r % n_new)[:, None]
    head = np.where(r < h_a * n_new, r // n_new, h_a + (r - h_a * n_new) // (2 * n_new))
    onehot = jnp.asarray(head[:, None] == np.arange(tab.shape[0])[None, :], F32)
    tab_rows = jnp.dot(onehot, tab, precision=lax.Precision.HIGHEST)
    far = tab_rows[:, N_BUCKETS - 1:]
    t = np.arange(MOBA_BLOCK)[None, :]
    dist_last = past_len + qi - (past_len - MOBA_BLOCK + t)
    last = (_bias_lookup(tab_rows, _t5_bucket_np(dist_last), per_row=True) - far) * LOG2E
    j = np.arange(LANES)[None, :]
    new = jnp.where(jnp.asarray((j <= qi) & (j < n_new)),
                    (_bias_lookup(tab_rows, _t5_bucket_np(qi - j), per_row=True) - far) * LOG2E, NEG_INF)
    return last.astype(F32), new.astype(F32)


def _proj_prompt_kernel(x_ref, w_ref, hq_ref, kat_ref, vat_ref, kb_ref, vb_ref, vat16_ref, vbt16_ref, kmean_ref,
                        *, seg, q_scale, h_b):
    x = x_ref[...].astype(BF16)
    dv = seg // h_b
    for s in range(6):
        h = _dot(x, w_ref[:, s * seg:(s + 1) * seg])
        if s == 0:
            hq_ref[:, 0:seg] = (h * q_scale).astype(BF16)
        elif s == 1:
            hq_ref[:, seg:2 * seg] = h.astype(BF16)
            for n in range(BLOCKS_PER_TILE):
                blk = h[n * MOBA_BLOCK:(n + 1) * MOBA_BLOCK]
                kmean_ref[n] = jnp.sum(blk, axis=0, keepdims=True) * (1.0 / MOBA_BLOCK)
            kat_ref[...] = h.T
        elif s == 2:
            ht = h.T
            vat_ref[...] = ht
            vat16_ref[...] = ht.astype(BF16)
        elif s == 3:
            hq_ref[:, 2 * seg:3 * seg] = (h * q_scale).astype(BF16)
        elif s == 4:
            hq_ref[:, 3 * seg:4 * seg] = h.astype(BF16)
            for hh in range(h_b):
                kb_ref[:, hh, :] = h[:, hh * dv:(hh + 1) * dv]
        else:
            for hh in range(h_b):
                vb_ref[:, hh, :] = h[:, hh * dv:(hh + 1) * dv]
            vbt16_ref[...] = h.T.astype(BF16)


def _proj_prompt(x2, w_in_l, *, batch, seq, h_b):
    tm = ATT_TILE
    m, d = x2.shape
    seg = w_in_l.shape[1] // 6
    nt = seq // tm
    dv = seg // h_b
    row = lambda w: pl.BlockSpec((tm, w), lambda b, i: (b * nt + i, 0))
    feat = pl.BlockSpec((None, seg, tm), lambda b, i: (b, 0, i))
    heads = pl.BlockSpec((tm, h_b, dv), lambda b, i: (b * nt + i, 0, 0))
    tiles = pl.BlockSpec((None, None, seg, tm), lambda b, i: (b, i, 0, 0))
    return pl.pallas_call(
        functools.partial(_proj_prompt_kernel, seg=seg, q_scale=DH ** -0.5 * LOG2E, h_b=h_b),
        grid=(batch, nt),
        in_specs=[row(d), _resident(w_in_l.shape)],
        out_specs=[row(4 * seg), feat, feat, heads, heads, tiles, tiles,
                   pl.BlockSpec((BLOCKS_PER_TILE, 1, seg), lambda b, i: (b * nt + i, 0, 0))],
        out_shape=[jax.ShapeDtypeStruct((m, 4 * seg), BF16),
                   jax.ShapeDtypeStruct((batch, seg, seq), F32), jax.ShapeDtypeStruct((batch, seg, seq), F32),
                   jax.ShapeDtypeStruct((m, h_b, dv), F32), jax.ShapeDtypeStruct((m, h_b, dv), F32),
                   jax.ShapeDtypeStruct((batch, nt, seg, tm), BF16), jax.ShapeDtypeStruct((batch, nt, seg, tm), BF16),
                   jax.ShapeDtypeStruct((m // MOBA_BLOCK, 1, seg), F32)],
        compiler_params=_cparams(("parallel", "parallel")),
    )(x2, w_in_l)


def _proj_sample_kernel(x_ref, w_ref, hq_ref, ka_ref, va_ref, kb_ref, vb_ref, *, seg, q_scale):
    x = x_ref[...].astype(BF16)
    outs = (None, ka_ref, va_ref, None, kb_ref, vb_ref)
    for s in range(6):
        h = _dot(x, w_ref[:, s * seg:(s + 1) * seg])
        if s == 0:
            hq_ref[:, 0:seg] = h * q_scale
        elif s == 3:
            hq_ref[:, seg:2 * seg] = h * q_scale
        else:
            outs[s][...] = h


def _proj_sample(x2, w_in_l, *, tm):
    m, d = x2.shape
    seg = w_in_l.shape[1] // 6
    row = lambda w: pl.BlockSpec((tm, w), lambda i: (i, 0))
    return pl.pallas_call(
        functools.partial(_proj_sample_kernel, seg=seg, q_scale=DH ** -0.5 * LOG2E),
        grid=(m // tm,),
        in_specs=[row(d), _resident(w_in_l.shape)],
        out_specs=[row(2 * seg)] + [row(seg)] * 4,
        out_shape=[jax.ShapeDtypeStruct((m, 2 * seg), F32)] + [jax.ShapeDtypeStruct((m, seg), F32)] * 4,
        compiler_params=_cparams(("parallel",)),
    )(x2, w_in_l)


PREV_TILE_BIAS_FROM = ATT_TILE - MAX_DISTANCE
SCORE_COLS = 256
SOFTMAX_ROWS = 32


def _flash_stages(s_ref, p_ref, bias_ref, bias_shift, vt_ext, m_sc, acc_sc, h, first, bias_from=0):
    t_k, t_q = s_ref.shape[0] - SUBLANES, s_ref.shape[1]
    sub = SUBLANES
    state = {}

    def chunk(c):
        rows = slice(c * SOFTMAX_ROWS, (c + 1) * SOFTMAX_ROWS)
        s = s_ref[rows, :]
        if bias_ref is not None and rows.stop > bias_from:
            s = s + bias_ref[rows, :]
        return s if bias_shift is None else s + bias_shift

    def column_max():
        if bias_ref is None and bias_shift is None:
            m8 = s_ref[t_k:t_k + sub, :]
        else:
            m8 = None
            for c in range(t_k // SOFTMAX_ROWS):
                part = jnp.max(chunk(c).reshape(SOFTMAX_ROWS // sub, sub, t_q), axis=0)
                m8 = part if m8 is None else jnp.maximum(m8, part)
        m_new = jnp.max(m8, axis=0, keepdims=True)
        if not first:
            state['m_old'] = m_sc[h]
            m_new = jnp.maximum(state['m_old'], m_new)
        state['m_new'] = m_new

    def exponentials():
        for c in range(t_k // SOFTMAX_ROWS):
            p_ref[c * SOFTMAX_ROWS:(c + 1) * SOFTMAX_ROWS, :] = jnp.exp2(chunk(c) - state['m_new']).astype(BF16)

    def values():
        pv = _dot(vt_ext(), p_ref[...])
        if first:
            acc_sc[h] = pv
        else:
            acc_sc[h] = jnp.exp2(state['m_old'] - state['m_new']) * acc_sc[h] + pv
        m_sc[h] = state['m_new']

    return [column_max, exponentials, values]


def _interleave(score_parts, softmax_stages):
    for i in range(max(len(score_parts), len(softmax_stages))):
        if i < len(score_parts):
            score_parts[i]()
        if i < len(softmax_stages):
            softmax_stages[i]()


def _score_buffers(t):
    return [pltpu.VMEM((2, t + SUBLANES, t), F32)] * 2 + [pltpu.VMEM((2, t, t), BF16)] * 2


def _store_scores(s_ref, cols, s):
    t_k = s.shape[0]
    s_ref[0:t_k, cols] = s
    s_ref[t_k:t_k + SUBLANES, cols] = jnp.max(s.reshape(t_k // SUBLANES, SUBLANES, s.shape[1]), axis=0)


def _flash_sweep(qi, scores, consume):
    near = jnp.maximum(qi - 1, 0)
    n_far = near
    last = jnp.maximum(n_far - 1, 0)
    _interleave(scores(qi, 0), [])
    _interleave(scores(near, 1), consume(qi, 0, 0, None, True))
    _interleave(scores(0, 0), consume(near, 1, 1, jnp.where(qi == 0, NEG_INF, 0.0), False))

    def pair(p, carry):
        j = 2 * p
        _interleave(scores(jnp.minimum(j + 1, last), 1), consume(j, 0, None, None, False))
        _interleave(scores(jnp.minimum(j + 2, last), 0), consume(j + 1, 1, None, None, False))
        return carry

    lax.fori_loop(0, n_far // 2, pair, 0)

    @pl.when(n_far % 2 == 1)
    def _():
        _interleave([], consume(last, 0, None, None, False))


def _moba_prompt_kernel(q_ref, k_ref, vt_ref, kmean_ref, bias_ref, o_ref, m_sc, acc_sc, s0_sc, s1_sc, p0_sc, p1_sc,
                        qx_sc):
    t = ATT_TILE
    qi = pl.program_id(2)
    qt = q_ref[...].astype(F32).T
    row = lax.broadcasted_iota(jnp.int32, (LANES, t), 0)
    head0 = row < DH
    q_heads = (jnp.where(head0, qt, 0.0).astype(BF16), jnp.where(head0, 0.0, qt).astype(BF16))
    kmean = kmean_ref[...].astype(BF16)
    nb = kmean.shape[0]
    blk = lax.broadcasted_iota(jnp.int32, (nb, t), 0)
    cur = qi * BLOCKS_PER_TILE + lax.broadcasted_iota(jnp.int32, (nb, t), 1) // MOBA_BLOCK

    def block_mask(qh):
        gate = jnp.where(blk < cur, _dot(kmean, qh), NEG_INF)
        chosen = blk == cur
        for j in range(MOBA_TOPK):
            best = jnp.max(gate, axis=0, keepdims=True)
            idx = jnp.min(jnp.where(gate == best, blk, nb), axis=0, keepdims=True)
            hit = blk == idx
            chosen = jnp.logical_or(chosen, jnp.logical_and(hit, cur > j))
            gate = jnp.where(hit, -3e38, gate)
        return jnp.where(chosen, 0.0, NEG_INF).astype(BF16)

    unused = jnp.zeros((LANES - nb, t), BF16)
    for h in range(2):
        qx_sc[h] = jnp.concatenate([q_heads[h], block_mask(q_heads[h]), unused], axis=0)
    lane = lax.broadcasted_iota(jnp.int32, (1, LANES), 1)
    ones = jnp.ones((BF16_ROWS, t), BF16)

    s_bufs, p_bufs = (s0_sc, s1_sc), (p0_sc, p1_sc)

    def scores(kj, slot):
        def part(h, c):
            rows = pl.ds(pl.multiple_of(kj * t, t), t)
            onehot = jnp.concatenate(
                [jnp.broadcast_to(jnp.where(lane == kj * BLOCKS_PER_TILE + n, 1.0, 0.0).astype(BF16),
                                  (MOBA_BLOCK, LANES)) for n in range(BLOCKS_PER_TILE)], axis=0)
            k_ext = jnp.concatenate([k_ref[rows, :], onehot], axis=1)
            cols = slice(c * SCORE_COLS, (c + 1) * SCORE_COLS)
            _store_scores(s_bufs[slot].at[h], cols, _dot(k_ext, qx_sc[h, :, cols]))
        return [functools.partial(part, h, c) for h in range(2) for c in range(t // SCORE_COLS)]

    def consume(kj, slot, bias_idx, bias_shift, first):
        stages = []
        for h in range(2):
            vt_ext = lambda h=h: jnp.concatenate([vt_ref[kj, h * DH:(h + 1) * DH, :], ones], axis=0)
            bias = None if bias_idx is None else bias_ref.at[h, bias_idx]
            stages += _flash_stages(s_bufs[slot].at[h], p_bufs[slot].at[h], bias, bias_shift, vt_ext, m_sc,
                                    acc_sc, h, first, bias_from=PREV_TILE_BIAS_FROM if bias_idx == 1 else 0)
        return stages

    _flash_sweep(qi, scores, consume)
    out = [acc_sc[h, 0:DH] / acc_sc[h, DH:DH + 1] for h in range(2)]
    o_ref[...] = jnp.concatenate(out, axis=0).T.astype(o_ref.dtype)


def _moba_prompt(hq, vat16, kmean, bias_a, *, batch, seq, h_a):
    t = ATT_TILE
    nb = kmean.shape[1]
    assert nb % BF16_ROWS == 0 and nb <= LANES
    pairs = h_a * DH // LANES
    seg_blocks = hq.shape[1] // 4 // LANES
    nq = seq // t
    return pl.pallas_call(
        _moba_prompt_kernel,
        grid=(batch, pairs, nq),
        in_specs=[
            pl.BlockSpec((t, LANES), lambda b, j, i: (b * nq + i, j)),
            pl.BlockSpec((seq, LANES), lambda b, j, i: (b, seg_blocks + j)),
            pl.BlockSpec((None, nq, LANES, t), lambda b, j, i: (b, 0, j, 0)),
            pl.BlockSpec((None, nb, LANES), lambda b, j, i: (b, 0, j)),
            pl.BlockSpec((2, 2, t, t), lambda b, j, i: (j, 0, 0, 0)),
        ],
        out_specs=pl.BlockSpec((t, LANES), lambda b, j, i: (b * nq + i, j)),
        out_shape=jax.ShapeDtypeStruct((batch * seq, pairs * LANES), BF16),
        scratch_shapes=([pltpu.VMEM((2, 1, t), F32), pltpu.VMEM((2, DH + BF16_ROWS, t), F32)] + _score_buffers(t)
                        + [pltpu.VMEM((2, 2 * LANES, t), BF16)]),
        compiler_params=_cparams(("parallel", "parallel", "arbitrary")),
    )(hq, hq, vat16, kmean, bias_a)


def _diff_prompt_kernel(lam_ref, q_ref, k_ref, vt_ref, bias_ref, g_ref, o_ref, m_sc, acc_sc,
                        s0_sc, s1_sc, p0_sc, p1_sc, qx_sc, *, out_scale):
    t = ATT_TILE
    dv = LANES
    qi = pl.program_id(2)
    qt = q_ref[...].astype(F32).T
    row = lax.broadcasted_iota(jnp.int32, (LANES, t), 0)
    half0 = row < DH
    qx_sc[0] = jnp.where(half0, qt, 0.0).astype(BF16)
    qx_sc[1] = jnp.where(half0, 0.0, qt).astype(BF16)
    ones = jnp.ones((BF16_ROWS, t), BF16)

    s_bufs, p_bufs = (s0_sc, s1_sc), (p0_sc, p1_sc)

    def scores(kj, slot):
        def part(h, c):
            k = k_ref[pl.ds(pl.multiple_of(kj * t, t), t), :]
            cols = slice(c * SCORE_COLS, (c + 1) * SCORE_COLS)
            _store_scores(s_bufs[slot].at[h], cols, _dot(k, qx_sc[h, :, cols]))
        return [functools.partial(part, h, c) for h in range(2) for c in range(t // SCORE_COLS)]

    def consume(kj, slot, bias_idx, bias_shift, first):
        vt_ext = lambda: jnp.concatenate([vt_ref[kj], ones], axis=0)
        bias = None if bias_idx is None else bias_ref.at[bias_idx]
        stages = []
        for h in range(2):
            stages += _flash_stages(s_bufs[slot].at[h], p_bufs[slot].at[h], bias, bias_shift, vt_ext, m_sc,
                                    acc_sc, h, first, bias_from=PREV_TILE_BIAS_FROM if bias_idx == 1 else 0)
        return stages

    _flash_sweep(qi, scores, consume)
    o = (acc_sc[0, 0:dv] / acc_sc[0, dv:dv + 1]
         - lam_ref[0] * (acc_sc[1, 0:dv] / acc_sc[1, dv:dv + 1]))
    rms = lax.rsqrt(jnp.mean(o * o, axis=0, keepdims=True) + RMS_EPS)
    o_ref[...] = (o * rms * g_ref[...] * out_scale).T.astype(o_ref.dtype)


def _diff_prompt(lam, hq, vbt16, bias_b, subln_g, *, batch, seq, h_b, lam_init):
    t = ATT_TILE
    seg_blocks = hq.shape[1] // 4 // LANES
    nq = seq // t
    g_spread = jnp.broadcast_to(subln_g.reshape(LANES, 1), (LANES, t))
    return pl.pallas_call(
        functools.partial(_diff_prompt_kernel, out_scale=1.0 - lam_init),
        grid=(batch, h_b, nq),
        in_specs=[
            pl.BlockSpec(memory_space=pltpu.SMEM),
            pl.BlockSpec((t, LANES), lambda b, h, i: (b * nq + i, 2 * seg_blocks + h)),
            pl.BlockSpec((seq, LANES), lambda b, h, i: (b, 3 * seg_blocks + h)),
            pl.BlockSpec((None, nq, LANES, t), lambda b, h, i: (b, 0, h, 0)),
            pl.BlockSpec((None, 2, t, t), lambda b, h, i: (h, 0, 0, 0)),
            pl.BlockSpec((LANES, t), lambda b, h, i: (0, 0)),
        ],
        out_specs=pl.BlockSpec((t, LANES), lambda b, h, i: (b * nq + i, h)),
        out_shape=jax.ShapeDtypeStruct((batch * seq, h_b * LANES), BF16),
        scratch_shapes=([pltpu.VMEM((2, 1, t), F32), pltpu.VMEM((2, LANES + BF16_ROWS, t), F32)] + _score_buffers(t)
                        + [pltpu.VMEM((2, LANES, t), BF16)]),
        compiler_params=_cparams(("parallel", "parallel", "arbitrary")),
    )(lam, hq, hq, vbt16, bias_b, g_spread)


def _softmax_tile(s):
    m = jnp.max(s, axis=1, keepdims=True)
    p = jnp.exp2(s - m)
    return m, p, jnp.sum(p, axis=1, keepdims=True)


def _sample_attn_kernel(pt_ref, lam_ref, hq_ref, ka_new_ref, va_new_ref, kb_new_ref, vb_new_ref,
                        bias_last_ref, bias_new_ref, g_ref, *refs,
                        n_steps, blocks_per_step, n_new, h_a, h_b, page, out_scale):
    del pt_ref
    n_pages = 2 * blocks_per_step
    page_refs = refs[:4 * n_pages]
    ya_ref, yb_ref = refs[4 * n_pages:4 * n_pages + 2]
    w_sc, gate_sc, m_sc, l_sc, o_sc = refs[4 * n_pages + 2:]
    g = pl.program_id(1)
    w_a = h_a * DH
    rows_a = h_a * n_new
    rows = rows_a + 2 * h_b * n_new
    n_past = n_steps * blocks_per_step

    at_step = (lambda step: pl.when(g == step)) if n_steps > 1 else (lambda step: (lambda body: body()))

    @at_step(0)
    def _():
        tiled = jnp.concatenate([hq_ref[...]] * (rows // n_new), axis=0)
        r = lax.broadcasted_iota(jnp.int32, tiled.shape, 0)
        c = lax.broadcasted_iota(jnp.int32, tiled.shape, 1)
        w_sc[...] = jnp.where(c // DH == r // n_new, tiled, 0.0).astype(BF16)

    def partial_block(n, s, bias, pv_a, pv_b):
        spread = lambda col: jnp.broadcast_to(col, (rows, LANES))
        gate_sc[n] = spread(jnp.sum(s, axis=1, keepdims=True))
        m, p, l = _softmax_tile(s + bias)
        p = p.astype(BF16)
        o_sc[n, 0:rows_a, :] = pv_a(p[0:rows_a])
        o_sc[n, rows_a:rows, :] = pv_b(p[rows_a:rows])
        m_sc[n] = spread(m)
        l_sc[n] = spread(l)

    def heads_on_lanes(ref):
        return jnp.concatenate([ref[pl.ds(h, page, stride=h_b), :] for h in range(h_b)], axis=1).astype(BF16)

    for u in range(blocks_per_step):
        n = g * blocks_per_step + u
        pages = [page_refs[4 * (2 * u + v):4 * (2 * u + v) + 4] for v in range(2)]
        ka_t = [pg[0][...].reshape(w_a, page).astype(BF16) for pg in pages]
        va_t = [pg[1][...].reshape(w_a, page).astype(BF16) for pg in pages]
        kb = [heads_on_lanes(pg[2]) for pg in pages]
        vb = [heads_on_lanes(pg[3]) for pg in pages]
        q_a = w_sc[0:rows_a, 0:w_a]
        q_b = w_sc[rows_a:rows, w_a:2 * w_a]
        s = jnp.concatenate([
            jnp.concatenate([_dot(q_a, k) for k in ka_t], axis=1),
            jnp.concatenate([_dot_nt(q_b, k) for k in kb], axis=1)], axis=0)
        pv_a = lambda p: sum(_dot_nt(p[:, v * page:(v + 1) * page], va_t[v]) for v in range(2))
        pv_b = lambda p: sum(_dot(p[:, v * page:(v + 1) * page], vb[v]) for v in range(2))
        is_last = (n == n_past - 1).astype(F32)
        partial_block(n, s, bias_last_ref[...] * is_last, pv_a, pv_b)

    @at_step(n_steps - 1)
    def _():
        pad =jnp.zeros((LANES - n_new, w_a), F32)
        padded = lambda ref: jnp.concatenate([ref[...], pad], axis=0).astype(BF16)
        k_cat = jnp.concatenate([padded(ka_new_ref), padded(kb_new_ref)], axis=1)
        va, vb = padded(va_new_ref), padded(vb_new_ref)
        partial_block(n_past, _dot_nt(w_sc[...], k_cat), bias_new_ref[...],
                      lambda p: _dot(p, va), lambda p: _dot(p, vb))

        rs_a = slice(0, rows_a)
        gates = [gate_sc[n, rs_a, :] for n in range(n_past)]
        kept = []
        for n in range(n_past):
            ahead = jnp.zeros((rows_a, LANES), F32)
            for j in range(n_past):
                if j != n:
                    ahead = ahead + jnp.where(gates[j] >= gates[n] if j < n else gates[j] > gates[n], 1.0, 0.0)
            kept.append(ahead < min(MOBA_TOPK, n_past))

        def combine(rs, kept):
            blocks = range(n_past + 1)
            on = lambda n: kept is None or n == n_past or kept[n]
            m_fin = None
            for n in blocks:
                m_n = m_sc[n, rs, :] if on(n) is True else jnp.where(on(n), m_sc[n, rs, :], NEG_INF)
                m_fin = m_n if m_fin is None else jnp.maximum(m_fin, m_n)
            l_fin, o = None, None
            for n in blocks:
                wgt = jnp.exp2(m_sc[n, rs, :] - m_fin)
                if on(n) is not True:
                    wgt = jnp.where(on(n), wgt, 0.0)
                l_n = wgt * l_sc[n, rs, :]
                o_n = jnp.concatenate([wgt] * (w_a // LANES), axis=1) * o_sc[n, rs, :]
                l_fin, o = (l_n, o_n) if o is None else (l_fin + l_n, o + o_n)
            return o / jnp.concatenate([l_fin] * (w_a // LANES), axis=1)

        o = jnp.concatenate([combine(rs_a, kept), combine(slice(rows_a, rows), None)], axis=0)

        col = lax.broadcasted_iota(jnp.int32, (n_new, w_a), 1)
        ya = jnp.zeros((n_new, w_a), F32)
        for h in range(h_a):
            ya = ya + jnp.where(col // DH == h, o[h * n_new:(h + 1) * n_new], 0.0)
        ya_ref[...] = ya
        lam = lam_ref[0]
        dv = w_a // h_b
        outs = []
        for h in range(h_b):
            r0 = rows_a + 2 * h * n_new
            d = (o[r0:r0 + n_new, h * dv:(h + 1) * dv]
                 - lam * o[r0 + n_new:r0 + 2 * n_new, h * dv:(h + 1) * dv])
            rms = lax.rsqrt(jnp.mean(d * d, axis=1, keepdims=True) + RMS_EPS)
            outs.append(d * rms * g_ref[...] * out_scale)
        yb_ref[...] = jnp.concatenate(outs, axis=1)


def _sample_attn(page_table, lam, hq, ka, va, kb, vb, caches, layer, bias_last, bias_new, subln_g,
                 *, n_new, h_a, h_b, lam_init, blocks_per_step):
    dec_batch, n_pages = page_table.shape
    page = caches[0].shape[2]
    assert 2 * page == MOBA_BLOCK and n_pages % (2 * blocks_per_step) == 0
    w_a = h_a * DH
    rows = (h_a + 2 * h_b) * n_new
    n_steps = n_pages // (2 * blocks_per_step)
    pps = 2 * blocks_per_step
    n_blocks = n_pages // 2 + 1
    caches = [caches[0].transpose(0, 1, 3, 4, 2), caches[1].transpose(0, 1, 3, 4, 2),
              caches[2].reshape(caches[2].shape[:2] + (page * h_b, -1)),
              caches[3].reshape(caches[3].shape[:2] + (page * h_b, -1))]

    def page_spec(v, c):
        shape = (None, None) + c.shape[2:]
        zeros = (0,) * (c.ndim - 2)
        return pl.BlockSpec(shape, lambda b, g, pt: (layer, pt[b * n_pages + g * pps + v]) + zeros)

    new_spec = pl.BlockSpec((n_new, w_a), lambda b, g, pt: (b, 0))
    in_specs = [
        pl.BlockSpec(memory_space=pltpu.SMEM),
        pl.BlockSpec((n_new, hq.shape[1]), lambda b, g, pt: (b, 0)),
        new_spec, new_spec, new_spec, new_spec,
        pl.BlockSpec((rows, MOBA_BLOCK), lambda b, g, pt: (0, 0)),
        pl.BlockSpec((rows, LANES), lambda b, g, pt: (0, 0)),
        pl.BlockSpec((1, w_a // h_b), lambda b, g, pt: (0, 0)),
    ]
    operands = [lam, hq, ka, va, kb, vb, bias_last, bias_new, subln_g]
    for v in range(pps):
        for c in caches:
            in_specs.append(page_spec(v, c))
            operands.append(c)
    grid_spec = pltpu.PrefetchScalarGridSpec(
        num_scalar_prefetch=1,
        grid=(dec_batch, n_steps),
        in_specs=in_specs,
        out_specs=[new_spec, new_spec],
        scratch_shapes=[
            pltpu.VMEM((rows, 2 * w_a), BF16),
            pltpu.VMEM((n_blocks, rows, LANES), F32), pltpu.VMEM((n_blocks, rows, LANES), F32),
            pltpu.VMEM((n_blocks, rows, LANES), F32),
            pltpu.VMEM((n_blocks, rows, w_a), F32),
        ],
    )
    return pl.pallas_call(
        functools.partial(_sample_attn_kernel, n_steps=n_steps, blocks_per_step=blocks_per_step, n_new=n_new,
                          h_a=h_a, h_b=h_b, page=page, out_scale=1.0 - lam_init),
        grid_spec=grid_spec,
        out_shape=[jax.ShapeDtypeStruct((dec_batch * n_new, w_a), F32)] * 2,
        compiler_params=_cparams(("parallel", "arbitrary")),
    )(page_table.reshape(-1), *operands)


def _layernorm(x, g, b):
    mu = jnp.mean(x, axis=-1, keepdims=True)
    xc = x - mu
    var = jnp.mean(xc * xc, axis=-1, keepdims=True)
    return xc * lax.rsqrt(var + LN_EPS) * g + b


def _mix_kernel(x_ref, ya_ref, yb_ref, wg_ref, bg_ref, wa_ref, wb_ref, wo_ref, g_ref, b_ref, o_ref, *, alpha):
    x = x_ref[...]
    d = x.shape[1]
    gates = jax.nn.sigmoid(_dot(x.astype(BF16), wg_ref[...]) + bg_ref[...])
    z = (gates[:, :d] * _dot(ya_ref[...].astype(BF16), wa_ref[...])
         + gates[:, d:] * _dot(yb_ref[...].astype(BF16), wb_ref[...]))
    mix = _dot(z.astype(BF16), wo_ref[...])
    o_ref[...] = _layernorm(alpha * x + mix, g_ref[...], b_ref[...])


def _mix(x2, ya, yb, wg, bg, wa, wb, wo, g, b, *, tm, alpha):
    m, d = x2.shape
    w = ya.shape[1]
    row = lambda c: pl.BlockSpec((tm, c), lambda i: (i, 0))
    return pl.pallas_call(
        functools.partial(_mix_kernel, alpha=alpha),
        grid=(m // tm,),
        in_specs=[row(d), row(w), row(w), _resident(wg.shape), _resident(bg.shape), _resident(wa.shape),
                  _resident(wb.shape), _resident(wo.shape), _resident(g.shape), _resident(b.shape)],
        out_specs=row(d),
        out_shape=jax.ShapeDtypeStruct((m, d), F32),
        compiler_params=_cparams(("parallel",)),
    )(x2, ya, yb, wg, bg, wa, wb, wo, g, b)


def _gelu_tanh(x):
    return 0.5 * x * (1.0 + jnp.tanh(math.sqrt(2.0 / math.pi) * (x + 0.044715 * (x * x * x))))


def _ffn_kernel(x_ref, p_ref, prev_ref, wu_ref, cw_ref, cb_ref, wd_ref, g_ref, b_ref, wpg_ref, wp_ref,
                o_ref, st_ref, carry_sc, act_sc, u_sc, *, alpha, seq_rows, n_chunks):
    x = x_ref[...]
    tm, d = x.shape
    xb = x.astype(BF16)
    cols = lambda c: slice(c * FFN_CHUNK, (c + 1) * FFN_CHUNK)
    long_seq = seq_rows is None
    if long_seq:
        first = pl.program_id(1) == 0
    else:
        groups = tm // seq_rows
        sub = lax.broadcasted_iota(jnp.int32, (groups, seq_rows, FFN_CHUNK), 1)

    def conv(c, u):
        cw = cw_ref[:, cols(c)]
        u3 = u.reshape(groups, seq_rows, FFN_CHUNK)
        hist = prev_ref[:, :, c * FFN_CHUNK:(c + 1) * FFN_CHUNK]
        back = lambda k: pltpu.roll(u, k, 0).reshape(groups, seq_rows, FFN_CHUNK)
        u1 = jnp.where(sub == 0, hist[:, 1:2], back(1))
        u2 = jnp.where(sub == 0, hist[:, 0:1], jnp.where(sub == 1, hist[:, 1:2], back(2)))
        st_ref[:, :, c * FFN_CHUNK:(c + 1) * FFN_CHUNK] = u3[:, seq_rows - 2:seq_rows]
        out = cw[0:1] * u2 + cw[1:2] * u1 + cw[2:3] * u3 + cb_ref[:, cols(c)]
        return out.reshape(tm, FFN_CHUNK)

    def up_long(c, slot):
        for half in range(2):
            cc = half * n_chunks + c
            u = _dot(xb, wu_ref[:, cols(cc)])
            u_sc[slot, half, CONV_HIST - (CONV_W - 1):CONV_HIST, :] = carry_sc[cc]
            u_sc[slot, half, CONV_HIST:CONV_HIST + tm, :] = u
            carry_sc[cc] = u[tm - (CONV_W - 1):tm]
            st_ref[0, :, cc * FFN_CHUNK:(cc + 1) * FFN_CHUNK] = u[tm - (CONV_W - 1):tm]

    def act_long(c, slot):
        for r0 in range(0, tm, FFN_ROWS):
            halves = []
            for half in range(2):
                cc = half * n_chunks + c
                cw = cw_ref[:, cols(cc)]
                out = cb_ref[:, cols(cc)]
                for j in range(CONV_W):
                    start = CONV_HIST + r0 - (CONV_W - 1 - j)
                    out = out + cw[j:j + 1] * u_sc[slot, half, start:start + FFN_ROWS, :]
                halves.append(out)
            act_sc[r0:r0 + FFN_ROWS, c * FFN_CHUNK:(c + 1) * FFN_CHUNK] = (
                _gelu_tanh(halves[0]) * halves[1]).astype(BF16)

    if long_seq:
        @pl.when(first)
        def _():
            for cc in range(2 * n_chunks):
                carry_sc[cc] = prev_ref[0, :, cc * FFN_CHUNK:(cc + 1) * FFN_CHUNK]

        up_long(0, 0)
        for c in range(n_chunks):
            if c + 1 < n_chunks:
                up_long(c + 1, (c + 1) % 2)
            act_long(c, c % 2)
    else:
        for c in range(n_chunks):
            c_gate = conv(c, _dot(xb, wu_ref[:, cols(c)]))
            c_val = conv(n_chunks + c, _dot(xb, wu_ref[:, cols(n_chunks + c)]))
            act_sc[:, c * FFN_CHUNK:(c + 1) * FFN_CHUNK] = (_gelu_tanh(c_gate) * c_val).astype(BF16)
    f = _dot(act_sc[...], wd_ref[...])
    y = _layernorm(alpha * x + f, g_ref[...], b_ref[...])
    gate = jax.nn.sigmoid(_dot(y.astype(BF16), wpg_ref[...]))
    o_ref[...] = y + gate * _dot(p_ref[...].astype(BF16), wp_ref[...])


def _ffn(x2, p2, prev, wu, cw, cb, wd, g, b, wpg, wp, *, tm, alpha, batch, seq_rows):
    m, d = x2.shape
    d_up = prev.shape[-1]
    n_chunks = wu.shape[1] // 2 // FFN_CHUNK
    d_ple = p2.shape[1]
    if seq_rows is None:
        tiles = m // batch // tm
        grid = (batch, tiles)
        row = lambda c: pl.BlockSpec((tm, c), lambda bb, i: (bb * tiles + i, 0))
        st_spec = pl.BlockSpec((1, CONV_W - 1, d_up), lambda bb, i: (bb, 0, 0))
        sem = ("parallel", "arbitrary")
    else:
        grid = (1, m // tm)
        row = lambda c: pl.BlockSpec((tm, c), lambda bb, i: (i, 0))
        st_spec = pl.BlockSpec((tm // seq_rows, CONV_W - 1, d_up), lambda bb, i: (i, 0, 0))
        sem = ("arbitrary", "arbitrary")
    n_seq = prev.shape[0]
    return pl.pallas_call(
        functools.partial(_ffn_kernel, alpha=alpha, seq_rows=seq_rows, n_chunks=n_chunks),
        grid=grid,
        in_specs=[row(d), row(d_ple), st_spec, _resident(wu.shape), _resident(cw.shape), _resident(cb.shape),
                  _resident(wd.shape), _resident(g.shape), _resident(b.shape), _resident(wpg.shape),
                  _resident(wp.shape)],
        out_specs=[row(d), st_spec],
        out_shape=[jax.ShapeDtypeStruct((m, d), F32), jax.ShapeDtypeStruct((n_seq, CONV_W - 1, d_up), F32)],
        scratch_shapes=[pltpu.VMEM((2 * n_chunks, CONV_W - 1, FFN_CHUNK), F32),
                        pltpu.VMEM((tm, n_chunks * FFN_CHUNK), BF16),
                        pltpu.VMEM((2, 2, CONV_HIST + tm, FFN_CHUNK), F32)],
        compiler_params=_cparams(sem),
    )(x2, p2, prev, wu, cw, cb, wd, g, b, wpg, wp)


def _layer_weights(l, w):
    return {
        'w_in': w['w_in'][l].astype(BF16),
        'w_gate': w['w_gate'][l].astype(BF16),
        'b_gate': w['b_gate'][l][None, :],
        'w_branch_a': w['w_branch_a'][l].astype(BF16),
        'w_branch_b': w['w_branch_b'][l].astype(BF16),
        'w_out': w['w_out'][l].astype(BF16),
        'ln1_g': w['ln1_g'][l][None, :], 'ln1_b': w['ln1_b'][l][None, :],
        'w_up': w['w_up'][l].astype(BF16),
        'conv_w': w['conv_w'][l],
        'conv_b': w['conv_b'][l][None, :],
        'w_down': w['w_down'][l].astype(BF16),
        'ln2_g': w['ln2_g'][l][None, :], 'ln2_b': w['ln2_b'][l][None, :],
        'w_ple_gate': w['w_ple_gate'][l].astype(BF16),
        'w_ple': w['w_ple'][l].astype(BF16),
        'subln_g': w['diff_subln_g'][l][None, :],
        'lam': (jnp.exp(jnp.sum(w['diff_lq1'][l] * w['diff_lk1'][l]))
                - jnp.exp(jnp.sum(w['diff_lq2'][l] * w['diff_lk2'][l]))
                + (0.8 - 0.6 * math.exp(-0.3 * l))).reshape(1).astype(F32),
    }


def _tail(x2, ya, yb, p2, prev, wl, *, alpha, tm_mix, tm_ffn, batch, seq_rows):
    x1 = _mix(x2, ya, yb, wl['w_gate'], wl['b_gate'], wl['w_branch_a'], wl['w_branch_b'], wl['w_out'],
              wl['ln1_g'], wl['ln1_b'], tm=tm_mix, alpha=alpha)
    return _ffn(x1, p2, prev, wl['w_up'], wl['conv_w'], wl['conv_b'], wl['w_down'], wl['ln2_g'], wl['ln2_b'],
                wl['w_ple_gate'], wl['w_ple'], tm=tm_ffn, alpha=alpha, batch=batch, seq_rows=seq_rows)


def _forward(x_prompt, x_sample, cache_a_k, cache_a_v, cache_b_k, cache_b_v, state_conv, page_table,
             p_prompt, p_sample, w, *, tm_prompt=256, tm_sample=256, blocks_per_step=8):
    depth = w['w_in'].shape[0]
    batch, seq, d = x_prompt.shape
    dec_batch, n_new, _ = x_sample.shape
    h_a, dh = cache_a_k.shape[3], cache_a_k.shape[4]
    h_b, dv = cache_b_v.shape[3], cache_b_v.shape[4]
    assert dh == DH and cache_b_k.shape[4] == 2 * DH and dv == LANES and seq % ATT_TILE == 0
    d_up = state_conv.shape[-1]
    past_len = page_table.shape[1] * cache_a_k.shape[2]
    alpha = (2 * depth) ** 0.25
    tab = w['rel_bias'].T
    bias_prompt = _prompt_bias_tiles(tab)
    bias_last, bias_new = _sample_bias_tiles(tab, h_a, n_new, past_len)
    caches = (cache_a_k, cache_a_v, cache_b_k, cache_b_v)

    xp = x_prompt.reshape(batch * seq, d)
    xs = x_sample.reshape(dec_batch * n_new, d)
    conv0 = jnp.zeros((batch, CONV_W - 1, d_up), F32)
    rows_p, rows_s = [], []
    for l in range(depth):
        wl = _layer_weights(l, w)
        lam_init = 0.8 - 0.6 * math.exp(-0.3 * l)

        hq, kat, vat, kb, vb, vat16, vbt16, kmean = _proj_prompt(xp, wl['w_in'], batch=batch, seq=seq, h_b=h_b)
        ya = _moba_prompt(hq, vat16, kmean.reshape(batch, seq // MOBA_BLOCK, h_a * DH), bias_prompt[:h_a],
                          batch=batch, seq=seq, h_a=h_a)
        yb = _diff_prompt(wl['lam'], hq, vbt16, bias_prompt[h_a:], wl['subln_g'], batch=batch, seq=seq, h_b=h_b,
                          lam_init=lam_init)
        xp, conv_p = _tail(xp, ya, yb, p_prompt[l].reshape(batch * seq, -1), conv0, wl, alpha=alpha,
                           tm_mix=tm_prompt, tm_ffn=tm_prompt, batch=batch, seq_rows=None)
        token_major = lambda a: a.reshape(batch, h_a, DH, seq).transpose(0, 3, 1, 2)
        rows_p.append((token_major(kat), token_major(vat), kb.reshape(batch, seq, h_b, 2 * DH),
                       vb.reshape(batch, seq, h_b, dv), conv_p))

        hq, ka, va, kb, vb = _proj_sample(xs, wl['w_in'], tm=tm_sample)
        ya, yb = _sample_attn(page_table, wl['lam'], hq, ka, va, kb, vb, caches, l, bias_last, bias_new,
                              wl['subln_g'], n_new=n_new, h_a=h_a, h_b=h_b, lam_init=lam_init,
                              blocks_per_step=blocks_per_step)
        xs, conv_s = _tail(xs, ya, yb, p_sample[l].reshape(dec_batch * n_new, -1), state_conv[l], wl,
                           alpha=alpha, tm_mix=tm_sample, tm_ffn=tm_sample, batch=1, seq_rows=n_new)
        rows_s.append((ka.reshape(dec_batch, n_new, h_a, DH), va.reshape(dec_batch, n_new, h_a, DH),
                       kb.reshape(dec_batch, n_new, h_b, 2 * DH), vb.reshape(dec_batch, n_new, h_b, dv), conv_s))

    stack = lambda rows, i: jnp.stack([r[i] for r in rows])
    return ((xp.reshape(batch, seq, d), xs.reshape(dec_batch, n_new, d))
            + tuple(stack(rows_p, i) for i in range(5)) + tuple(stack(rows_s, i) for i in range(5)))


def kernel(x_prompt, x_sample, cache_a_k, cache_a_v, cache_b_k, cache_b_v, state_conv, page_table, p_prompt, p_sample, w_in, w_gate, b_gate, w_branch_a, w_branch_b, w_out, rel_bias, diff_lq1, diff_lk1, diff_lq2, diff_lk2, diff_subln_g, ln1_g, ln1_b, w_up, conv_w, conv_b, w_down, ln2_g, ln2_b, w_ple, w_ple_gate):
    w = {'w_in': w_in, 'w_gate': w_gate, 'b_gate': b_gate, 'w_branch_a': w_branch_a,
         'w_branch_b': w_branch_b, 'w_out': w_out, 'rel_bias': rel_bias, 'diff_lq1': diff_lq1,
         'diff_lk1': diff_lk1, 'diff_lq2': diff_lq2, 'diff_lk2': diff_lk2, 'diff_subln_g': diff_subln_g,
         'ln1_g': ln1_g, 'ln1_b': ln1_b, 'w_up': w_up, 'conv_w': conv_w, 'conv_b': conv_b,
         'w_down': w_down, 'ln2_g': ln2_g, 'ln2_b': ln2_b, 'w_ple': w_ple, 'w_ple_gate': w_ple_gate}
    return _forward(x_prompt, x_sample, cache_a_k, cache_a_v, cache_b_k, cache_b_v, state_conv, page_table,
                    p_prompt, p_sample, w)
```

```python
import functools
import math

import numpy as np
import jax
import jax.numpy as jnp
from jax import lax
from jax.experimental import pallas as pl
from jax.experimental.pallas import tpu as pltpu

F32 = jnp.float32
BF16 = jnp.bfloat16

DH = 64
MOBA_BLOCK = 256
MOBA_TOPK = 3
N_BUCKETS = 32
MAX_DISTANCE = 128
CONV_W = 3
LN_EPS = 1e-5
RMS_EPS = 1e-5
NEG_INF = -1e30
LOG2E = math.log2(math.e)
LANES = 128
SUBLANES = 8
BF16_ROWS = 16
VMEM_LIMIT = 56 * 1024 * 1024

ATT_TILE = 2 * MOBA_BLOCK
BLOCKS_PER_TILE = ATT_TILE // MOBA_BLOCK
FFN_CHUNK = 256
FFN_ROWS = 64
CONV_HIST = 8


def _cparams(sem):
    return pltpu.CompilerParams(dimension_semantics=sem, vmem_limit_bytes=VMEM_LIMIT)


def _dot(a, b):
    return jnp.dot(a, b, preferred_element_type=F32)


def _dot_nt(a, b):
    return lax.dot_general(a, b, (((1,), (1,)), ((), ())), preferred_element_type=F32)


def _resident(shape):
    return pl.BlockSpec(shape, lambda *_: (0,) * len(shape), pipeline_mode=pl.Buffered(1))


def _t5_bucket_np(dist):
    dist = np.maximum(dist, 0)
    exact = N_BUCKETS // 2
    logv = (np.log(np.maximum(dist, 1).astype(np.float32) / np.float32(exact))
            / np.float32(math.log(MAX_DISTANCE / exact))).astype(np.float32)
    large = np.minimum(exact + (logv * np.float32(N_BUCKETS - exact)).astype(np.int32), N_BUCKETS - 1)
    return np.where(dist < exact, dist, large).astype(np.int32)


def _bias_lookup(tab_rows, buckets, per_row=False):
    shape = buckets.shape if per_row else (tab_rows.shape[0],) + buckets.shape
    out = jnp.zeros(shape, F32)
    for n in np.unique(buckets):
        col = tab_rows[:, n].reshape((-1,) + (1,) * (len(shape) - 1))
        out = jnp.where(jnp.asarray(buckets == n), col, out)
    return out


def _prompt_bias_tiles(tab):
    t = ATT_TILE
    h = tab.shape[0]
    far = tab[:, N_BUCKETS - 1:]
    dist = np.arange(-(t - 1), 2 * t)
    by_dist = jnp.where(jnp.asarray(dist >= 0)[None], (_bias_lookup(tab, _t5_bucket_np(dist)) - far) * LOG2E,
                        NEG_INF)
    flat = jnp.tile(jnp.pad(by_dist, ((0, 0), (0, 1))), (1, t))[:, :t * (3 * t - 1)]
    both = flat.reshape(h, t, 3 * t - 1)[:, :, t - 1:]
    return jnp.stack([both[:, :, :t], both[:, :, t:]], axis=1).astype(F32)


def _sample_bias_tiles(tab, h_a, n_new, past_len):
    h_b = tab.shape[0] - h_a
    rows = (h_a + 2 * h_b) * n_new
    r = np.arange(rows)
    qi = (r % n_new)[:, None]
    head = np.where(r < h_a * n_new, r // n_new, h_a + (r - h_a * n_new) // (2 * n_new))
    onehot = jnp.asarray(head[:, None] == np.arange(tab.shape[0])[None, :], F32)
    tab_rows = jnp.dot(onehot, tab, precision=lax.Precision.HIGHEST)
    far = tab_rows[:, N_BUCKETS - 1:]
    t = np.arange(MOBA_BLOCK)[None, :]
    dist_last = past_len + qi - (past_len - MOBA_BLOCK + t)
    last = (_bias_lookup(tab_rows, _t5_bucket_np(dist_last), per_row=True) - far) * LOG2E
    j = np.arange(LANES)[None, :]
    new = jnp.where(jnp.asarray((j <= qi) & (j < n_new)),
                    (_bias_lookup(tab_rows, _t5_bucket_np(qi - j), per_row=True) - far) * LOG2E, NEG_INF)
    return last.astype(F32), new.astype(F32)


def _proj_prompt_kernel(x_ref, w_ref, *refs, seg, q_scale, h_b, n_prev):
    prev = refs[:4] if n_prev else ()
    hq_ref, kat_ref, vat_ref, kb_ref, vb_ref, vat16_ref, vbt16_ref, kmean_ref = refs[len(prev):]
    for src, dst in zip(prev, (kat_ref, vat_ref, kb_ref, vb_ref)):
        dst[0:n_prev] = src[...]
    x = x_ref[...].astype(BF16)
    dv = seg // h_b
    for s in range(6):
        h = _dot(x, w_ref[:, s * seg:(s + 1) * seg])
        if s == 0:
            hq_ref[:, 0:seg] = (h * q_scale).astype(BF16)
        elif s == 1:
            hq_ref[:, seg:2 * seg] = h.astype(BF16)
            for n in range(BLOCKS_PER_TILE):
                blk = h[n * MOBA_BLOCK:(n + 1) * MOBA_BLOCK]
                kmean_ref[n] = jnp.sum(blk, axis=0, keepdims=True) * (1.0 / MOBA_BLOCK)
            kat_ref[n_prev] = h.T
        elif s == 2:
            ht = h.T
            vat_ref[n_prev] = ht
            vat16_ref[...] = ht.astype(BF16)
        elif s == 3:
            hq_ref[:, 2 * seg:3 * seg] = (h * q_scale).astype(BF16)
        elif s == 4:
            hq_ref[:, 3 * seg:4 * seg] = h.astype(BF16)
            for hh in range(h_b):
                kb_ref[n_prev, :, hh, :] = h[:, hh * dv:(hh + 1) * dv]
        else:
            for hh in range(h_b):
                vb_ref[n_prev, :, hh, :] = h[:, hh * dv:(hh + 1) * dv]
            vbt16_ref[...] = h.T.astype(BF16)


def _proj_prompt(x2, w_in_l, prev_rows, *, batch, seq, h_b):
    tm = ATT_TILE
    m, d = x2.shape
    seg = w_in_l.shape[1] // 6
    nt = seq // tm
    dv = seg // h_b
    n_prev = prev_rows[0].shape[0] if prev_rows else 0
    row = lambda w: pl.BlockSpec((tm, w), lambda b, i: (b * nt + i, 0))
    feat = lambda n: pl.BlockSpec((n, None, seg, tm), lambda b, i: (0, b, 0, i))
    heads = lambda n: pl.BlockSpec((n, tm, h_b, dv), lambda b, i: (0, b * nt + i, 0, 0))
    tiles = pl.BlockSpec((None, None, seg, tm), lambda b, i: (b, i, 0, 0))
    stacked = lambda n: [feat(n), feat(n), heads(n), heads(n)]
    return pl.pallas_call(
        functools.partial(_proj_prompt_kernel, seg=seg, q_scale=DH ** -0.5 * LOG2E, h_b=h_b, n_prev=n_prev),
        grid=(batch, nt),
        in_specs=[row(d), _resident(w_in_l.shape)] + (stacked(n_prev) if n_prev else []),
        out_specs=[row(4 * seg)] + stacked(n_prev + 1) + [
            tiles, tiles, pl.BlockSpec((BLOCKS_PER_TILE, 1, seg), lambda b, i: (b * nt + i, 0, 0))],
        out_shape=[jax.ShapeDtypeStruct((m, 4 * seg), BF16),
                   jax.ShapeDtypeStruct((n_prev + 1, batch, seg, seq), F32),
                   jax.ShapeDtypeStruct((n_prev + 1, batch, seg, seq), F32),
                   jax.ShapeDtypeStruct((n_prev + 1, m, h_b, dv), F32),
                   jax.ShapeDtypeStruct((n_prev + 1, m, h_b, dv), F32),
                   jax.ShapeDtypeStruct((batch, nt, seg, tm), BF16), jax.ShapeDtypeStruct((batch, nt, seg, tm), BF16),
                   jax.ShapeDtypeStruct((m // MOBA_BLOCK, 1, seg), F32)],
        compiler_params=_cparams(("parallel", "parallel")),
    )(x2, w_in_l, *prev_rows)


def _proj_sample_kernel(x_ref, w_ref, hq_ref, ka_ref, va_ref, kb_ref, vb_ref, *, seg, q_scale):
    x = x_ref[...].astype(BF16)
    outs = (None, ka_ref, va_ref, None, kb_ref, vb_ref)
    for s in range(6):
        h = _dot(x, w_ref[:, s * seg:(s + 1) * seg])
        if s == 0:
            hq_ref[:, 0:seg] = h * q_scale
        elif s == 3:
            hq_ref[:, seg:2 * seg] = h * q_scale
        else:
            outs[s][...] = h


def _proj_sample(x2, w_in_l, *, tm):
    m, d = x2.shape
    seg = w_in_l.shape[1] // 6
    row = lambda w: pl.BlockSpec((tm, w), lambda i: (i, 0))
    return pl.pallas_call(
        functools.partial(_proj_sample_kernel, seg=seg, q_scale=DH ** -0.5 * LOG2E),
        grid=(m // tm,),
        in_specs=[row(d), _resident(w_in_l.shape)],
        out_specs=[row(2 * seg)] + [row(seg)] * 4,
        out_shape=[jax.ShapeDtypeStruct((m, 2 * seg), F32)] + [jax.ShapeDtypeStruct((m, seg), F32)] * 4,
        compiler_params=_cparams(("parallel",)),
    )(x2, w_in_l)


PREV_TILE_BIAS_FROM = ATT_TILE - MAX_DISTANCE
SCORE_COLS = 256
SOFTMAX_ROWS = 32


def _flash_stages(s_ref, p_ref, bias_ref, bias_shift, vt_ext, m_sc, acc_sc, h, first, bias_from=0):
    t_k, t_q = s_ref.shape[0] - SUBLANES, s_ref.shape[1]
    sub = SUBLANES
    state = {}

    def chunk(c):
        rows = slice(c * SOFTMAX_ROWS, (c + 1) * SOFTMAX_ROWS)
        s = s_ref[rows, :]
        if bias_ref is not None and rows.stop > bias_from:
            s = s + bias_ref[rows, :]
        return s if bias_shift is None else s + bias_shift

    def column_max():
        if bias_ref is None and bias_shift is None:
            m8 = s_ref[t_k:t_k + sub, :]
        else:
            m8 = None
            for c in range(t_k // SOFTMAX_ROWS):
                part = jnp.max(chunk(c).reshape(SOFTMAX_ROWS // sub, sub, t_q), axis=0)
                m8 = part if m8 is None else jnp.maximum(m8, part)
        m_new = jnp.max(m8, axis=0, keepdims=True)
        if not first:
            state['m_old'] = m_sc[h]
            m_new = jnp.maximum(state['m_old'], m_new)
        state['m_new'] = m_new

    def exponentials():
        for c in range(t_k // SOFTMAX_ROWS):
            p_ref[c * SOFTMAX_ROWS:(c + 1) * SOFTMAX_ROWS, :] = jnp.exp2(chunk(c) - state['m_new']).astype(BF16)

    def values():
        pv = _dot(vt_ext(), p_ref[...])
        if first:
            acc_sc[h] = pv
        else:
            acc_sc[h] = jnp.exp2(state['m_old'] - state['m_new']) * acc_sc[h] + pv
        m_sc[h] = state['m_new']

    return [column_max, exponentials, values]


def _interleave(score_parts, softmax_stages):
    for i in range(max(len(score_parts), len(softmax_stages))):
        if i < len(score_parts):
            score_parts[i]()
        if i < len(softmax_stages):
            softmax_stages[i]()


def _score_buffers(t):
    return [pltpu.VMEM((2, t + SUBLANES, t), F32)] * 2 + [pltpu.VMEM((2, t, t), BF16)] * 2


def _store_scores(s_ref, cols, s):
    t_k = s.shape[0]
    s_ref[0:t_k, cols] = s
    s_ref[t_k:t_k + SUBLANES, cols] = jnp.max(s.reshape(t_k // SUBLANES, SUBLANES, s.shape[1]), axis=0)


def _flash_sweep(qi, scores, consume):
    near = jnp.maximum(qi - 1, 0)
    n_far = near
    last = jnp.maximum(n_far - 1, 0)
    _interleave(scores(qi, 0), [])
    _interleave(scores(near, 1), consume(qi, 0, 0, None, True))
    _interleave(scores(0, 0), consume(near, 1, 1, jnp.where(qi == 0, NEG_INF, 0.0), False))

    def pair(p, carry):
        j = 2 * p
        _interleave(scores(jnp.minimum(j + 1, last), 1), consume(j, 0, None, None, False))
        _interleave(scores(jnp.minimum(j + 2, last), 0), consume(j + 1, 1, None, None, False))
        return carry

    lax.fori_loop(0, n_far // 2, pair, 0)

    @pl.when(n_far % 2 == 1)
    def _():
        _interleave([], consume(last, 0, None, None, False))


def _moba_prompt_kernel(q_ref, k_ref, vt_ref, kmean_ref, bias_ref, o_ref, m_sc, acc_sc, s0_sc, s1_sc, p0_sc, p1_sc,
                        qx_sc):
    t = ATT_TILE
    qi = pl.program_id(2)
    qt = q_ref[...].astype(F32).T
    row = lax.broadcasted_iota(jnp.int32, (LANES, t), 0)
    head0 = row < DH
    q_heads = (jnp.where(head0, qt, 0.0).astype(BF16), jnp.where(head0, 0.0, qt).astype(BF16))
    kmean = kmean_ref[...].astype(BF16)
    nb = kmean.shape[0]
    blk = lax.broadcasted_iota(jnp.int32, (nb, t), 0)
    cur = qi * BLOCKS_PER_TILE + lax.broadcasted_iota(jnp.int32, (nb, t), 1) // MOBA_BLOCK

    def block_mask(qh):
        gate = jnp.where(blk < cur, _dot(kmean, qh), NEG_INF)
        chosen = blk == cur
        for j in range(MOBA_TOPK):
            best = jnp.max(gate, axis=0, keepdims=True)
            idx = jnp.min(jnp.where(gate == best, blk, nb), axis=0, keepdims=True)
            hit = blk == idx
            chosen = jnp.logical_or(chosen, jnp.logical_and(hit, cur > j))
            gate = jnp.where(hit, -3e38, gate)
        return jnp.where(chosen, 0.0, NEG_INF).astype(BF16)

    unused = jnp.zeros((LANES - nb, t), BF16)
    for h in range(2):
        qx_sc[h] = jnp.concatenate([q_heads[h], block_mask(q_heads[h]), unused], axis=0)
    lane = lax.broadcasted_iota(jnp.int32, (1, LANES), 1)
    ones = jnp.ones((BF16_ROWS, t), BF16)

    s_bufs, p_bufs = (s0_sc, s1_sc), (p0_sc, p1_sc)

    def scores(kj, slot):
        def part(h, c):
            rows = pl.ds(pl.multiple_of(kj * t, t), t)
            onehot = jnp.concatenate(
                [jnp.broadcast_to(jnp.where(lane == kj * BLOCKS_PER_TILE + n, 1.0, 0.0).astype(BF16),
                                  (MOBA_BLOCK, LANES)) for n in range(BLOCKS_PER_TILE)], axis=0)
            k_ext = jnp.concatenate([k_ref[rows, :], onehot], axis=1)
            cols = slice(c * SCORE_COLS, (c + 1) * SCORE_COLS)
            _store_scores(s_bufs[slot].at[h], cols, _dot(k_ext, qx_sc[h, :, cols]))
        return [functools.partial(part, h, c) for h in range(2) for c in range(t // SCORE_COLS)]

    def consume(kj, slot, bias_idx, bias_shift, first):
        stages = []
        for h in range(2):
            vt_ext = lambda h=h: jnp.concatenate([vt_ref[kj, h * DH:(h + 1) * DH, :], ones], axis=0)
            bias = None if bias_idx is None else bias_ref.at[h, bias_idx]
            stages += _flash_stages(s_bufs[slot].at[h], p_bufs[slot].at[h], bias, bias_shift, vt_ext, m_sc,
                                    acc_sc, h, first, bias_from=PREV_TILE_BIAS_FROM if bias_idx == 1 else 0)
        return stages

    _flash_sweep(qi, scores, consume)
    out = [acc_sc[h, 0:DH] / acc_sc[h, DH:DH + 1] for h in range(2)]
    o_ref[...] = jnp.concatenate(out, axis=0).T.astype(o_ref.dtype)


def _moba_prompt(hq, vat16, kmean, bias_a, *, batch, seq, h_a):
    t = ATT_TILE
    nb = kmean.shape[1]
    assert nb % BF16_ROWS == 0 and nb <= LANES
    pairs = h_a * DH // LANES
    seg_blocks = hq.shape[1] // 4 // LANES
    nq = seq // t
    return pl.pallas_call(
        _moba_prompt_kernel,
        grid=(batch, pairs, nq),
        in_specs=[
            pl.BlockSpec((t, LANES), lambda b, j, i: (b * nq + i, j)),
            pl.BlockSpec((seq, LANES), lambda b, j, i: (b, seg_blocks + j)),
            pl.BlockSpec((None, nq, LANES, t), lambda b, j, i: (b, 0, j, 0)),
            pl.BlockSpec((None, nb, LANES), lambda b, j, i: (b, 0, j)),
            pl.BlockSpec((2, 2, t, t), lambda b, j, i: (j, 0, 0, 0)),
        ],
        out_specs=pl.BlockSpec((t, LANES), lambda b, j, i: (b * nq + i, j)),
        out_shape=jax.ShapeDtypeStruct((batch * seq, pairs * LANES), BF16),
        scratch_shapes=([pltpu.VMEM((2, 1, t), F32), pltpu.VMEM((2, DH + BF16_ROWS, t), F32)] + _score_buffers(t)
                        + [pltpu.VMEM((2, 2 * LANES, t), BF16)]),
        compiler_params=_cparams(("parallel", "parallel", "arbitrary")),
    )(hq, hq, vat16, kmean, bias_a)


def _diff_prompt_kernel(lam_ref, q_ref, k_ref, vt_ref, bias_ref, g_ref, o_ref, m_sc, acc_sc,
                        s0_sc, s1_sc, p0_sc, p1_sc, qx_sc, *, out_scale):
    t = ATT_TILE
    dv = LANES
    qi = pl.program_id(2)
    qt = q_ref[...].astype(F32).T
    row = lax.broadcasted_iota(jnp.int32, (LANES, t), 0)
    half0 = row < DH
    qx_sc[0] = jnp.where(half0, qt, 0.0).astype(BF16)
    qx_sc[1] = jnp.where(half0, 0.0, qt).astype(BF16)
    ones = jnp.ones((BF16_ROWS, t), BF16)

    s_bufs, p_bufs = (s0_sc, s1_sc), (p0_sc, p1_sc)

    def scores(kj, slot):
        def part(h, c):
            k = k_ref[pl.ds(pl.multiple_of(kj * t, t), t), :]
            cols = slice(c * SCORE_COLS, (c + 1) * SCORE_COLS)
            _store_scores(s_bufs[slot].at[h], cols, _dot(k, qx_sc[h, :, cols]))
        return [functools.partial(part, h, c) for h in range(2) for c in range(t // SCORE_COLS)]

    def consume(kj, slot, bias_idx, bias_shift, first):
        vt_ext = lambda: jnp.concatenate([vt_ref[kj], ones], axis=0)
        bias = None if bias_idx is None else bias_ref.at[bias_idx]
        stages = []
        for h in range(2):
            stages += _flash_stages(s_bufs[slot].at[h], p_bufs[slot].at[h], bias, bias_shift, vt_ext, m_sc,
                                    acc_sc, h, first, bias_from=PREV_TILE_BIAS_FROM if bias_idx == 1 else 0)
        return stages

    _flash_sweep(qi, scores, consume)
    o = (acc_sc[0, 0:dv] / acc_sc[0, dv:dv + 1]
         - lam_ref[0] * (acc_sc[1, 0:dv] / acc_sc[1, dv:dv + 1]))
    rms = lax.rsqrt(jnp.mean(o * o, axis=0, keepdims=True) + RMS_EPS)
    o_ref[...] = (o * rms * g_ref[...] * out_scale).T.astype(o_ref.dtype)


def _diff_prompt(lam, hq, vbt16, bias_b, subln_g, *, batch, seq, h_b, lam_init):
    t = ATT_TILE
    seg_blocks = hq.shape[1] // 4 // LANES
    nq = seq // t
    g_spread = jnp.broadcast_to(subln_g.reshape(LANES, 1), (LANES, t))
    return pl.pallas_call(
        functools.partial(_diff_prompt_kernel, out_scale=1.0 - lam_init),
        grid=(batch, h_b, nq),
        in_specs=[
            pl.BlockSpec(memory_space=pltpu.SMEM),
            pl.BlockSpec((t, LANES), lambda b, h, i: (b * nq + i, 2 * seg_blocks + h)),
            pl.BlockSpec((seq, LANES), lambda b, h, i: (b, 3 * seg_blocks + h)),
            pl.BlockSpec((None, nq, LANES, t), lambda b, h, i: (b, 0, h, 0)),
            pl.BlockSpec((None, 2, t, t), lambda b, h, i: (h, 0, 0, 0)),
            pl.BlockSpec((LANES, t), lambda b, h, i: (0, 0)),
        ],
        out_specs=pl.BlockSpec((t, LANES), lambda b, h, i: (b * nq + i, h)),
        out_shape=jax.ShapeDtypeStruct((batch * seq, h_b * LANES), BF16),
        scratch_shapes=([pltpu.VMEM((2, 1, t), F32), pltpu.VMEM((2, LANES + BF16_ROWS, t), F32)] + _score_buffers(t)
                        + [pltpu.VMEM((2, LANES, t), BF16)]),
        compiler_params=_cparams(("parallel", "parallel", "arbitrary")),
    )(lam, hq, hq, vbt16, bias_b, g_spread)


def _softmax_tile(s):
    m = jnp.max(s, axis=1, keepdims=True)
    p = jnp.exp2(s - m)
    return m, p, jnp.sum(p, axis=1, keepdims=True)


def _sample_attn_kernel(pt_ref, lam_ref, hq_ref, ka_new_ref, va_new_ref, kb_new_ref, vb_new_ref,
                        bias_last_ref, bias_new_ref, g_ref, *refs,
                        n_steps, blocks_per_step, n_new, h_a, h_b, page, out_scale):
    del pt_ref
    n_pages = 2 * blocks_per_step
    page_refs = refs[:4 * n_pages]
    ya_ref, yb_ref = refs[4 * n_pages:4 * n_pages + 2]
    w_sc, gate_sc, m_sc, l_sc, o_sc = refs[4 * n_pages + 2:]
    g = pl.program_id(1)
    w_a = h_a * DH
    rows_a = h_a * n_new
    rows = rows_a + 2 * h_b * n_new
    n_past = n_steps * blocks_per_step

    at_step = (lambda step: pl.when(g == step)) if n_steps > 1 else (lambda step: (lambda body: body()))

    @at_step(0)
    def _():
        tiled = jnp.concatenate([hq_ref[...]] * (rows // n_new), axis=0)
        r = lax.broadcasted_iota(jnp.int32, tiled.shape, 0)
        c = lax.broadcasted_iota(jnp.int32, tiled.shape, 1)
        w_sc[...] = jnp.where(c // DH == r // n_new, tiled, 0.0).astype(BF16)

    def partial_block(n, s, bias, pv_a, pv_b):
        spread = lambda col: jnp.broadcast_to(col, (rows, LANES))
        gate_sc[n] = spread(jnp.sum(s, axis=1, keepdims=True))
        m, p, l = _softmax_tile(s + bias)
        p = p.astype(BF16)
        o_sc[n, 0:rows_a, :] = pv_a(p[0:rows_a])
        o_sc[n, rows_a:rows, :] = pv_b(p[rows_a:rows])
        m_sc[n] = spread(m)
        l_sc[n] = spread(l)

    def heads_on_lanes(ref):
        return jnp.concatenate([ref[pl.ds(h, page, stride=h_b), :] for h in range(h_b)], axis=1).astype(BF16)

    for u in range(blocks_per_step):
        n = g * blocks_per_step + u
        pages = [page_refs[4 * (2 * u + v):4 * (2 * u + v) + 4] for v in range(2)]
        ka_t = [pg[0][...].reshape(w_a, page).astype(BF16) for pg in pages]
        va_t = [pg[1][...].reshape(w_a, page).astype(BF16) for pg in pages]
        kb = [heads_on_lanes(pg[2]) for pg in pages]
        vb = [heads_on_lanes(pg[3]) for pg in pages]
        q_a = w_sc[0:rows_a, 0:w_a]
        q_b = w_sc[rows_a:rows, w_a:2 * w_a]
        s = jnp.concatenate([
            jnp.concatenate([_dot(q_a, k) for k in ka_t], axis=1),
            jnp.concatenate([_dot_nt(q_b, k) for k in kb], axis=1)], axis=0)
        pv_a = lambda p: sum(_dot_nt(p[:, v * page:(v + 1) * page], va_t[v]) for v in range(2))
        pv_b = lambda p: sum(_dot(p[:, v * page:(v + 1) * page], vb[v]) for v in range(2))
        is_last = (n == n_past - 1).astype(F32)
        partial_block(n, s, bias_last_ref[...] * is_last, pv_a, pv_b)

    @at_step(n_steps - 1)
    def _():
        pad =jnp.zeros((LANES - n_new, w_a), F32)
        padded = lambda ref: jnp.concatenate([ref[...], pad], axis=0).astype(BF16)
        k_cat = jnp.concatenate([padded(ka_new_ref), padded(kb_new_ref)], axis=1)
        va, vb = padded(va_new_ref), padded(vb_new_ref)
        partial_block(n_past, _dot_nt(w_sc[...], k_cat), bias_new_ref[...],
                      lambda p: _dot(p, va), lambda p: _dot(p, vb))

        rs_a = slice(0, rows_a)
        gates = [gate_sc[n, rs_a, :] for n in range(n_past)]
        kept = []
        for n in range(n_past):
            ahead = jnp.zeros((rows_a, LANES), F32)
            for j in range(n_past):
                if j != n:
                    ahead = ahead + jnp.where(gates[j] >= gates[n] if j < n else gates[j] > gates[n], 1.0, 0.0)
            kept.append(ahead < min(MOBA_TOPK, n_past))

        def combine(rs, kept):
            blocks = range(n_past + 1)
            on = lambda n: kept is None or n == n_past or kept[n]
            m_fin = None
            for n in blocks:
                m_n = m_sc[n, rs, :] if on(n) is True else jnp.where(on(n), m_sc[n, rs, :], NEG_INF)
                m_fin = m_n if m_fin is None else jnp.maximum(m_fin, m_n)
            l_fin, o = None, None
            for n in blocks:
                wgt = jnp.exp2(m_sc[n, rs, :] - m_fin)
                if on(n) is not True:
                    wgt = jnp.where(on(n), wgt, 0.0)
                l_n = wgt * l_sc[n, rs, :]
                o_n = jnp.concatenate([wgt] * (w_a // LANES), axis=1) * o_sc[n, rs, :]
                l_fin, o = (l_n, o_n) if o is None else (l_fin + l_n, o + o_n)
            return o / jnp.concatenate([l_fin] * (w_a // LANES), axis=1)

        o = jnp.concatenate([combine(rs_a, kept), combine(slice(rows_a, rows), None)], axis=0)

        col = lax.broadcasted_iota(jnp.int32, (n_new, w_a), 1)
        ya = jnp.zeros((n_new, w_a), F32)
        for h in range(h_a):
            ya = ya + jnp.where(col // DH == h, o[h * n_new:(h + 1) * n_new], 0.0)
        ya_ref[...] = ya
        lam = lam_ref[0]
        dv = w_a // h_b
        outs = []
        for h in range(h_b):
            r0 = rows_a + 2 * h * n_new
            d = (o[r0:r0 + n_new, h * dv:(h + 1) * dv]
                 - lam * o[r0 + n_new:r0 + 2 * n_new, h * dv:(h + 1) * dv])
            rms = lax.rsqrt(jnp.mean(d * d, axis=1, keepdims=True) + RMS_EPS)
            outs.append(d * rms * g_ref[...] * out_scale)
        yb_ref[...] = jnp.concatenate(outs, axis=1)


def _sample_attn(page_table, lam, hq, ka, va, kb, vb, caches, layer, bias_last, bias_new, subln_g,
                 *, n_new, h_a, h_b, lam_init, blocks_per_step):
    dec_batch, n_pages = page_table.shape
    page = caches[0].shape[2]
    assert 2 * page == MOBA_BLOCK and n_pages % (2 * blocks_per_step) == 0
    w_a = h_a * DH
    rows = (h_a + 2 * h_b) * n_new
    n_steps = n_pages // (2 * blocks_per_step)
    pps = 2 * blocks_per_step
    n_blocks = n_pages // 2 + 1
    caches = [caches[0].transpose(0, 1, 3, 4, 2), caches[1].transpose(0, 1, 3, 4, 2),
              caches[2].reshape(caches[2].shape[:2] + (page * h_b, -1)),
              caches[3].reshape(caches[3].shape[:2] + (page * h_b, -1))]

    def page_spec(v, c):
        shape = (None, None) + c.shape[2:]
        zeros = (0,) * (c.ndim - 2)
        return pl.BlockSpec(shape, lambda b, g, pt: (layer, pt[b * n_pages + g * pps + v]) + zeros)

    new_spec = pl.BlockSpec((n_new, w_a), lambda b, g, pt: (b, 0))
    in_specs = [
        pl.BlockSpec(memory_space=pltpu.SMEM),
        pl.BlockSpec((n_new, hq.shape[1]), lambda b, g, pt: (b, 0)),
        new_spec, new_spec, new_spec, new_spec,
        pl.BlockSpec((rows, MOBA_BLOCK), lambda b, g, pt: (0, 0)),
        pl.BlockSpec((rows, LANES), lambda b, g, pt: (0, 0)),
        pl.BlockSpec((1, w_a // h_b), lambda b, g, pt: (0, 0)),
    ]
    operands = [lam, hq, ka, va, kb, vb, bias_last, bias_new, subln_g]
    for v in range(pps):
        for c in caches:
            in_specs.append(page_spec(v, c))
            operands.append(c)
    grid_spec = pltpu.PrefetchScalarGridSpec(
        num_scalar_prefetch=1,
        grid=(dec_batch, n_steps),
        in_specs=in_specs,
        out_specs=[new_spec, new_spec],
        scratch_shapes=[
            pltpu.VMEM((rows, 2 * w_a), BF16),
            pltpu.VMEM((n_blocks, rows, LANES), F32), pltpu.VMEM((n_blocks, rows, LANES), F32),
            pltpu.VMEM((n_blocks, rows, LANES), F32),
            pltpu.VMEM((n_blocks, rows, w_a), F32),
        ],
    )
    return pl.pallas_call(
        functools.partial(_sample_attn_kernel, n_steps=n_steps, blocks_per_step=blocks_per_step, n_new=n_new,
                          h_a=h_a, h_b=h_b, page=page, out_scale=1.0 - lam_init),
        grid_spec=grid_spec,
        out_shape=[jax.ShapeDtypeStruct((dec_batch * n_new, w_a), F32)] * 2,
        compiler_params=_cparams(("parallel", "arbitrary")),
    )(page_table.reshape(-1), *operands)


def _layernorm(x, g, b):
    mu = jnp.mean(x, axis=-1, keepdims=True)
    xc = x - mu
    var = jnp.mean(xc * xc, axis=-1, keepdims=True)
    return xc * lax.rsqrt(var + LN_EPS) * g + b


def _mix_kernel(x_ref, ya_ref, yb_ref, wg_ref, bg_ref, wa_ref, wb_ref, wo_ref, g_ref, b_ref, o_ref, *, alpha):
    x = x_ref[...]
    d = x.shape[1]
    gates = jax.nn.sigmoid(_dot(x.astype(BF16), wg_ref[...]) + bg_ref[...])
    z = (gates[:, :d] * _dot(ya_ref[...].astype(BF16), wa_ref[...])
         + gates[:, d:] * _dot(yb_ref[...].astype(BF16), wb_ref[...]))
    mix = _dot(z.astype(BF16), wo_ref[...])
    o_ref[...] = _layernorm(alpha * x + mix, g_ref[...], b_ref[...])


def _mix(x2, ya, yb, wg, bg, wa, wb, wo, g, b, *, tm, alpha):
    m, d = x2.shape
    w = ya.shape[1]
    row = lambda c: pl.BlockSpec((tm, c), lambda i: (i, 0))
    return pl.pallas_call(
        functools.partial(_mix_kernel, alpha=alpha),
        grid=(m // tm,),
        in_specs=[row(d), row(w), row(w), _resident(wg.shape), _resident(bg.shape), _resident(wa.shape),
                  _resident(wb.shape), _resident(wo.shape), _resident(g.shape), _resident(b.shape)],
        out_specs=row(d),
        out_shape=jax.ShapeDtypeStruct((m, d), F32),
        compiler_params=_cparams(("parallel",)),
    )(x2, ya, yb, wg, bg, wa, wb, wo, g, b)


def _gelu_tanh(x):
    return 0.5 * x * (1.0 + jnp.tanh(math.sqrt(2.0 / math.pi) * (x + 0.044715 * (x * x * x))))


def _ffn_kernel(x_ref, p_ref, prev_ref, wu_ref, cw_ref, cb_ref, wd_ref, g_ref, b_ref, wpg_ref, wp_ref,
                o_ref, st_ref, carry_sc, act_sc, u_sc, *, alpha, seq_rows, n_chunks):
    x = x_ref[...]
    tm, d = x.shape
    xb = x.astype(BF16)
    cols = lambda c: slice(c * FFN_CHUNK, (c + 1) * FFN_CHUNK)
    long_seq = seq_rows is None
    if long_seq:
        first = pl.program_id(1) == 0
    else:
        groups = tm // seq_rows
        sub = lax.broadcasted_iota(jnp.int32, (groups, seq_rows, FFN_CHUNK), 1)

    def conv(c, u):
        cw = cw_ref[:, cols(c)]
        u3 = u.reshape(groups, seq_rows, FFN_CHUNK)
        hist = prev_ref[:, :, c * FFN_CHUNK:(c + 1) * FFN_CHUNK]
        back = lambda k: pltpu.roll(u, k, 0).reshape(groups, seq_rows, FFN_CHUNK)
        u1 = jnp.where(sub == 0, hist[:, 1:2], back(1))
        u2 = jnp.where(sub == 0, hist[:, 0:1], jnp.where(sub == 1, hist[:, 1:2], back(2)))
        st_ref[:, :, c * FFN_CHUNK:(c + 1) * FFN_CHUNK] = u3[:, seq_rows - 2:seq_rows]
        out = cw[0:1] * u2 + cw[1:2] * u1 + cw[2:3] * u3 + cb_ref[:, cols(c)]
        return out.reshape(tm, FFN_CHUNK)

    def up_long(c, slot):
        for half in range(2):
            cc = half * n_chunks + c
            u = _dot(xb, wu_ref[:, cols(cc)])
            u_sc[slot, half, CONV_HIST - (CONV_W - 1):CONV_HIST, :] = carry_sc[cc]
            u_sc[slot, half, CONV_HIST:CONV_HIST + tm, :] = u
            carry_sc[cc] = u[tm - (CONV_W - 1):tm]
            st_ref[0, :, cc * FFN_CHUNK:(cc + 1) * FFN_CHUNK] = u[tm - (CONV_W - 1):tm]

    def act_long(c, slot):
        for r0 in range(0, tm, FFN_ROWS):
            halves = []
            for half in range(2):
                cc = half * n_chunks + c
                cw = cw_ref[:, cols(cc)]
                out = cb_ref[:, cols(cc)]
                for j in range(CONV_W):
                    start = CONV_HIST + r0 - (CONV_W - 1 - j)
                    out = out + cw[j:j + 1] * u_sc[slot, half, start:start + FFN_ROWS, :]
                halves.append(out)
            act_sc[r0:r0 + FFN_ROWS, c * FFN_CHUNK:(c + 1) * FFN_CHUNK] = (
                _gelu_tanh(halves[0]) * halves[1]).astype(BF16)

    if long_seq:
        @pl.when(first)
        def _():
            for cc in range(2 * n_chunks):
                carry_sc[cc] = prev_ref[0, :, cc * FFN_CHUNK:(cc + 1) * FFN_CHUNK]

        up_long(0, 0)
        for c in range(n_chunks):
            if c + 1 < n_chunks:
                up_long(c + 1, (c + 1) % 2)
            act_long(c, c % 2)
    else:
        for c in range(n_chunks):
            c_gate = conv(c, _dot(xb, wu_ref[:, cols(c)]))
            c_val = conv(n_chunks + c, _dot(xb, wu_ref[:, cols(n_chunks + c)]))
            act_sc[:, c * FFN_CHUNK:(c + 1) * FFN_CHUNK] = (_gelu_tanh(c_gate) * c_val).astype(BF16)
    f = _dot(act_sc[...], wd_ref[...])
    y = _layernorm(alpha * x + f, g_ref[...], b_ref[...])
    gate = jax.nn.sigmoid(_dot(y.astype(BF16), wpg_ref[...]))
    o_ref[...] = y + gate * _dot(p_ref[...].astype(BF16), wp_ref[...])


def _ffn(x2, p2, prev, wu, cw, cb, wd, g, b, wpg, wp, *, tm, alpha, batch, seq_rows):
    m, d = x2.shape
    d_up = prev.shape[-1]
    n_chunks = wu.shape[1] // 2 // FFN_CHUNK
    d_ple = p2.shape[1]
    if seq_rows is None:
        tiles = m // batch // tm
        grid = (batch, tiles)
        row = lambda c: pl.BlockSpec((tm, c), lambda bb, i: (bb * tiles + i, 0))
        st_spec = pl.BlockSpec((1, CONV_W - 1, d_up), lambda bb, i: (bb, 0, 0))
        sem = ("parallel", "arbitrary")
    else:
        grid = (1, m // tm)
        row = lambda c: pl.BlockSpec((tm, c), lambda bb, i: (i, 0))
        st_spec = pl.BlockSpec((tm // seq_rows, CONV_W - 1, d_up), lambda bb, i: (i, 0, 0))
        sem = ("arbitrary", "arbitrary")
    n_seq = prev.shape[0]
    return pl.pallas_call(
        functools.partial(_ffn_kernel, alpha=alpha, seq_rows=seq_rows, n_chunks=n_chunks),
        grid=grid,
        in_specs=[row(d), row(d_ple), st_spec, _resident(wu.shape), _resident(cw.shape), _resident(cb.shape),
                  _resident(wd.shape), _resident(g.shape), _resident(b.shape), _resident(wpg.shape),
                  _resident(wp.shape)],
        out_specs=[row(d), st_spec],
        out_shape=[jax.ShapeDtypeStruct((m, d), F32), jax.ShapeDtypeStruct((n_seq, CONV_W - 1, d_up), F32)],
        scratch_shapes=[pltpu.VMEM((2 * n_chunks, CONV_W - 1, FFN_CHUNK), F32),
                        pltpu.VMEM((tm, n_chunks * FFN_CHUNK), BF16),
                        pltpu.VMEM((2, 2, CONV_HIST + tm, FFN_CHUNK), F32)],
        compiler_params=_cparams(sem),
    )(x2, p2, prev, wu, cw, cb, wd, g, b, wpg, wp)


def _layer_weights(l, w):
    return {
        'w_in': w['w_in'][l].astype(BF16),
        'w_gate': w['w_gate'][l].astype(BF16),
        'b_gate': w['b_gate'][l][None, :],
        'w_branch_a': w['w_branch_a'][l].astype(BF16),
        'w_branch_b': w['w_branch_b'][l].astype(BF16),
        'w_out': w['w_out'][l].astype(BF16),
        'ln1_g': w['ln1_g'][l][None, :], 'ln1_b': w['ln1_b'][l][None, :],
        'w_up': w['w_up'][l].astype(BF16),
        'conv_w': w['conv_w'][l],
        'conv_b': w['conv_b'][l][None, :],
        'w_down': w['w_down'][l].astype(BF16),
        'ln2_g': w['ln2_g'][l][None, :], 'ln2_b': w['ln2_b'][l][None, :],
        'w_ple_gate': w['w_ple_gate'][l].astype(BF16),
        'w_ple': w['w_ple'][l].astype(BF16),
        'subln_g': w['diff_subln_g'][l][None, :],
        'lam': (jnp.exp(jnp.sum(w['diff_lq1'][l] * w['diff_lk1'][l]))
                - jnp.exp(jnp.sum(w['diff_lq2'][l] * w['diff_lk2'][l]))
                + (0.8 - 0.6 * math.exp(-0.3 * l))).reshape(1).astype(F32),
    }


def _tail(x2, ya, yb, p2, prev, wl, *, alpha, tm_mix, tm_ffn, batch, seq_rows):
    x1 = _mix(x2, ya, yb, wl['w_gate'], wl['b_gate'], wl['w_branch_a'], wl['w_branch_b'], wl['w_out'],
              wl['ln1_g'], wl['ln1_b'], tm=tm_mix, alpha=alpha)
    return _ffn(x1, p2, prev, wl['w_up'], wl['conv_w'], wl['conv_b'], wl['w_down'], wl['ln2_g'], wl['ln2_b'],
                wl['w_ple_gate'], wl['w_ple'], tm=tm_ffn, alpha=alpha, batch=batch, seq_rows=seq_rows)


def _forward(x_prompt, x_sample, cache_a_k, cache_a_v, cache_b_k, cache_b_v, state_conv, page_table,
             p_prompt, p_sample, w, *, tm_prompt=256, tm_sample=256, blocks_per_step=8):
    depth = w['w_in'].shape[0]
    batch, seq, d = x_prompt.shape
    dec_batch, n_new, _ = x_sample.shape
    h_a, dh = cache_a_k.shape[3], cache_a_k.shape[4]
    h_b, dv = cache_b_v.shape[3], cache_b_v.shape[4]
    assert dh == DH and cache_b_k.shape[4] == 2 * DH and dv == LANES and seq % ATT_TILE == 0
    d_up = state_conv.shape[-1]
    past_len = page_table.shape[1] * cache_a_k.shape[2]
    alpha = (2 * depth) ** 0.25
    tab = w['rel_bias'].T
    bias_prompt = _prompt_bias_tiles(tab)
    bias_last, bias_new = _sample_bias_tiles(tab, h_a, n_new, past_len)
    caches = (cache_a_k, cache_a_v, cache_b_k, cache_b_v)

    xp = x_prompt.reshape(batch * seq, d)
    xs = x_sample.reshape(dec_batch * n_new, d)
    conv0 = jnp.zeros((batch, CONV_W - 1, d_up), F32)
    kv_p, conv_p_all, rows_s = (), [], []
    for l in range(depth):
        wl = _layer_weights(l, w)
        lam_init = 0.8 - 0.6 * math.exp(-0.3 * l)

        hq, *kv_p, vat16, vbt16, kmean = _proj_prompt(xp, wl['w_in'], tuple(kv_p), batch=batch, seq=seq, h_b=h_b)
        ya = _moba_prompt(hq, vat16, kmean.reshape(batch, seq // MOBA_BLOCK, h_a * DH), bias_prompt[:h_a],
                          batch=batch, seq=seq, h_a=h_a)
        yb = _diff_prompt(wl['lam'], hq, vbt16, bias_prompt[h_a:], wl['subln_g'], batch=batch, seq=seq, h_b=h_b,
                          lam_init=lam_init)
        xp, conv_p = _tail(xp, ya, yb, p_prompt[l].reshape(batch * seq, -1), conv0, wl, alpha=alpha,
                           tm_mix=tm_prompt, tm_ffn=tm_prompt, batch=batch, seq_rows=None)
        conv_p_all.append(conv_p)

        hq, ka, va, kb, vb = _proj_sample(xs, wl['w_in'], tm=tm_sample)
        ya, yb = _sample_attn(page_table, wl['lam'], hq, ka, va, kb, vb, caches, l, bias_last, bias_new,
                              wl['subln_g'], n_new=n_new, h_a=h_a, h_b=h_b, lam_init=lam_init,
                              blocks_per_step=blocks_per_step)
        xs, conv_s = _tail(xs, ya, yb, p_sample[l].reshape(dec_batch * n_new, -1), state_conv[l], wl,
                           alpha=alpha, tm_mix=tm_sample, tm_ffn=tm_sample, batch=1, seq_rows=n_new)
        rows_s.append((ka.reshape(dec_batch, n_new, h_a, DH), va.reshape(dec_batch, n_new, h_a, DH),
                       kb.reshape(dec_batch, n_new, h_b, 2 * DH), vb.reshape(dec_batch, n_new, h_b, dv), conv_s))

    kat, vat, kb, vb = kv_p
    token_major = lambda a: a.reshape(depth, batch, h_a, DH, seq).transpose(0, 1, 4, 2, 3)
    stack = lambda rows, i: jnp.stack([r[i] for r in rows])
    return ((xp.reshape(batch, seq, d), xs.reshape(dec_batch, n_new, d), token_major(kat), token_major(vat),
             kb.reshape(depth, batch, seq, h_b, 2 * DH), vb.reshape(depth, batch, seq, h_b, dv),
             jnp.stack(conv_p_all)) + tuple(stack(rows_s, i) for i in range(5)))


def kernel(x_prompt, x_sample, cache_a_k, cache_a_v, cache_b_k, cache_b_v, state_conv, page_table, p_prompt, p_sample, w_in, w_gate, b_gate, w_branch_a, w_branch_b, w_out, rel_bias, diff_lq1, diff_lk1, diff_lq2, diff_lk2, diff_subln_g, ln1_g, ln1_b, w_up, conv_w, conv_b, w_down, ln2_g, ln2_b, w_ple, w_ple_gate):
    w = {'w_in': w_in, 'w_gate': w_gate, 'b_gate': b_gate, 'w_branch_a': w_branch_a,
         'w_branch_b': w_branch_b, 'w_out': w_out, 'rel_bias': rel_bias, 'diff_lq1': diff_lq1,
         'diff_lk1': diff_lk1, 'diff_lq2': diff_lq2, 'diff_lk2': diff_lk2, 'diff_subln_g': diff_subln_g,
         'ln1_g': ln1_g, 'ln1_b': ln1_b, 'w_up': w_up, 'conv_w': conv_w, 'conv_b': conv_b,
         'w_down': w_down, 'ln2_g': ln2_g, 'ln2_b': ln2_b, 'w_ple': w_ple, 'w_ple_gate': w_ple_gate}
    return _forward(x_prompt, x_sample, cache_a_k, cache_a_v, cache_b_k, cache_b_v, state_conv, page_table,
                    p_prompt, p_sample, w)
```

```python
import functools
import math

import numpy as np
import jax
import jax.numpy as jnp
from jax import lax
from jax.experimental import pallas as pl
from jax.experimental.pallas import tpu as pltpu

F32 = jnp.float32
BF16 = jnp.bfloat16

DH = 64
MOBA_BLOCK = 256
MOBA_TOPK = 3
N_BUCKETS = 32
MAX_DISTANCE = 128
CONV_W = 3
LN_EPS = 1e-5
RMS_EPS = 1e-5
NEG_INF = -1e30
LOG2E = math.log2(math.e)
LANES = 128
SUBLANES = 8
BF16_ROWS = 16
VMEM_LIMIT = 56 * 1024 * 1024

ATT_TILE = 2 * MOBA_BLOCK
BLOCKS_PER_TILE = ATT_TILE // MOBA_BLOCK
FFN_CHUNK = 256
FFN_ROWS = 64
CONV_HIST = 8


def _cparams(sem):
    return pltpu.CompilerParams(dimension_semantics=sem, vmem_limit_bytes=VMEM_LIMIT)


def _dot(a, b):
    return jnp.dot(a, b, preferred_element_type=F32)


def _dot_nt(a, b):
    return lax.dot_general(a, b, (((1,), (1,)), ((), ())), preferred_element_type=F32)


def _resident(shape):
    return pl.BlockSpec(shape, lambda *_: (0,) * len(shape), pipeline_mode=pl.Buffered(1))


def _t5_bucket_np(dist):
    dist = np.maximum(dist, 0)
    exact = N_BUCKETS // 2
    logv = (np.log(np.maximum(dist, 1).astype(np.float32) / np.float32(exact))
            / np.float32(math.log(MAX_DISTANCE / exact))).astype(np.float32)
    large = np.minimum(exact + (logv * np.float32(N_BUCKETS - exact)).astype(np.int32), N_BUCKETS - 1)
    return np.where(dist < exact, dist, large).astype(np.int32)


def _bias_lookup(tab_rows, buckets, per_row=False):
    shape = buckets.shape if per_row else (tab_rows.shape[0],) + buckets.shape
    out = jnp.zeros(shape, F32)
    for n in np.unique(buckets):
        col = tab_rows[:, n].reshape((-1,) + (1,) * (len(shape) - 1))
        out = jnp.where(jnp.asarray(buckets == n), col, out)
    return out


def _prompt_bias_tiles(tab):
    t, bsz = ATT_TILE, MAX_DISTANCE
    far = tab[:, N_BUCKETS - 1][:, None, None]
    r = np.arange(bsz)[:, None]
    s = np.arange(bsz)[None, :]

    def block(delta):
        dist = delta * bsz + s - r
        return jnp.where(jnp.asarray(dist >= 0)[None], (_bias_lookup(tab, _t5_bucket_np(dist)) - far) * LOG2E,
                         NEG_INF)

    delta = np.arange(2 * t)[None, :] // bsz - np.arange(t)[:, None] // bsz
    reps = (1, t // bsz, 2 * t // bsz)
    both = jnp.where(jnp.asarray(delta == 0)[None], jnp.tile(block(0), reps),
                     jnp.where(jnp.asarray(delta == 1)[None], jnp.tile(block(1), reps),
                               jnp.where(jnp.asarray(delta < 0)[None], NEG_INF, 0.0)))
    return jnp.stack([both[:, :, :t], both[:, :, t:]], axis=1).astype(F32)


def _sample_bias_tiles(tab, h_a, n_new, past_len):
    h_b = tab.shape[0] - h_a
    rows = (h_a + 2 * h_b) * n_new
    r = np.arange(rows)
    qi = (r % n_new)[:, None]
    head = np.where(r < h_a * n_new, r // n_new, h_a + (r - h_a * n_new) // (2 * n_new))
    onehot = jnp.asarray(head[:, None] == np.arange(tab.shape[0])[None, :], F32)
    tab_rows = jnp.dot(onehot, tab, precision=lax.Precision.HIGHEST)
    far = tab_rows[:, N_BUCKETS - 1:]
    t = np.arange(MOBA_BLOCK)[None, :]
    dist_last = past_len + qi - (past_len - MOBA_BLOCK + t)
    last = (_bias_lookup(tab_rows, _t5_bucket_np(dist_last), per_row=True) - far) * LOG2E
    j = np.arange(LANES)[None, :]
    new = jnp.where(jnp.asarray((j <= qi) & (j < n_new)),
                    (_bias_lookup(tab_rows, _t5_bucket_np(qi - j), per_row=True) - far) * LOG2E, NEG_INF)
    return last.astype(F32), new.astype(F32)


def _proj_prompt_kernel(x_ref, w_ref, *refs, seg, q_scale, h_b, n_prev):
    prev = refs[:4] if n_prev else ()
    hq_ref, kat_ref, vat_ref, kb_ref, vb_ref, vat16_ref, vbt16_ref, kmean_ref = refs[len(prev):]
    for src, dst in zip(prev, (kat_ref, vat_ref, kb_ref, vb_ref)):
        dst[0:n_prev] = src[...]
    x = x_ref[...].astype(BF16)
    dv = seg // h_b
    for s in range(6):
        h = _dot(x, w_ref[:, s * seg:(s + 1) * seg])
        if s == 0:
            hq_ref[:, 0:seg] = (h * q_scale).astype(BF16)
        elif s == 1:
            hq_ref[:, seg:2 * seg] = h.astype(BF16)
            for n in range(BLOCKS_PER_TILE):
                blk = h[n * MOBA_BLOCK:(n + 1) * MOBA_BLOCK]
                kmean_ref[n] = jnp.sum(blk, axis=0, keepdims=True) * (1.0 / MOBA_BLOCK)
            kat_ref[n_prev] = h.T
        elif s == 2:
            ht = h.T
            vat_ref[n_prev] = ht
            vat16_ref[...] = ht.astype(BF16)
        elif s == 3:
            hq_ref[:, 2 * seg:3 * seg] = (h * q_scale).astype(BF16)
        elif s == 4:
            hq_ref[:, 3 * seg:4 * seg] = h.astype(BF16)
            for hh in range(h_b):
                kb_ref[n_prev, :, hh, :] = h[:, hh * dv:(hh + 1) * dv]
        else:
            for hh in range(h_b):
                vb_ref[n_prev, :, hh, :] = h[:, hh * dv:(hh + 1) * dv]
            vbt16_ref[...] = h.T.astype(BF16)


def _proj_prompt(x2, w_in_l, prev_rows, *, batch, seq, h_b):
    tm = ATT_TILE
    m, d = x2.shape
    seg = w_in_l.shape[1] // 6
    nt = seq // tm
    dv = seg // h_b
    n_prev = prev_rows[0].shape[0] if prev_rows else 0
    row = lambda w: pl.BlockSpec((tm, w), lambda b, i: (b * nt + i, 0))
    feat = lambda n: pl.BlockSpec((n, None, seg, tm), lambda b, i: (0, b, 0, i))
    heads = lambda n: pl.BlockSpec((n, tm, h_b, dv), lambda b, i: (0, b * nt + i, 0, 0))
    tiles = pl.BlockSpec((None, None, seg, tm), lambda b, i: (b, i, 0, 0))
    stacked = lambda n: [feat(n), feat(n), heads(n), heads(n)]
    return pl.pallas_call(
        functools.partial(_proj_prompt_kernel, seg=seg, q_scale=DH ** -0.5 * LOG2E, h_b=h_b, n_prev=n_prev),
        grid=(batch, nt),
        in_specs=[row(d), _resident(w_in_l.shape)] + (stacked(n_prev) if n_prev else []),
        out_specs=[row(4 * seg)] + stacked(n_prev + 1) + [
            tiles, tiles, pl.BlockSpec((BLOCKS_PER_TILE, 1, seg), lambda b, i: (b * nt + i, 0, 0))],
        out_shape=[jax.ShapeDtypeStruct((m, 4 * seg), BF16),
                   jax.ShapeDtypeStruct((n_prev + 1, batch, seg, seq), F32),
                   jax.ShapeDtypeStruct((n_prev + 1, batch, seg, seq), F32),
                   jax.ShapeDtypeStruct((n_prev + 1, m, h_b, dv), F32),
                   jax.ShapeDtypeStruct((n_prev + 1, m, h_b, dv), F32),
                   jax.ShapeDtypeStruct((batch, nt, seg, tm), BF16), jax.ShapeDtypeStruct((batch, nt, seg, tm), BF16),
                   jax.ShapeDtypeStruct((m // MOBA_BLOCK, 1, seg), F32)],
        compiler_params=_cparams(("parallel", "parallel")),
    )(x2, w_in_l, *prev_rows)


def _proj_sample_kernel(x_ref, w_ref, hq_ref, ka_ref, va_ref, kb_ref, vb_ref, *, seg, q_scale):
    x = x_ref[...].astype(BF16)
    outs = (None, ka_ref, va_ref, None, kb_ref, vb_ref)
    for s in range(6):
        h = _dot(x, w_ref[:, s * seg:(s + 1) * seg])
        if s == 0:
            hq_ref[:, 0:seg] = h * q_scale
        elif s == 3:
            hq_ref[:, seg:2 * seg] = h * q_scale
        else:
            outs[s][...] = h


def _proj_sample(x2, w_in_l, *, tm):
    m, d = x2.shape
    seg = w_in_l.shape[1] // 6
    row = lambda w: pl.BlockSpec((tm, w), lambda i: (i, 0))
    return pl.pallas_call(
        functools.partial(_proj_sample_kernel, seg=seg, q_scale=DH ** -0.5 * LOG2E),
        grid=(m // tm,),
        in_specs=[row(d), _resident(w_in_l.shape)],
        out_specs=[row(2 * seg)] + [row(seg)] * 4,
        out_shape=[jax.ShapeDtypeStruct((m, 2 * seg), F32)] + [jax.ShapeDtypeStruct((m, seg), F32)] * 4,
        compiler_params=_cparams(("parallel",)),
    )(x2, w_in_l)


PREV_TILE_BIAS_FROM = ATT_TILE - MAX_DISTANCE
SCORE_COLS = 256
SOFTMAX_ROWS = 32


def _flash_stages(s_ref, p_ref, bias_ref, bias_shift, vt_ext, m_sc, acc_sc, h, first, bias_from=0):
    t_k, t_q = s_ref.shape[0] - SUBLANES, s_ref.shape[1]
    sub = SUBLANES
    state = {}

    def chunk(c):
        rows = slice(c * SOFTMAX_ROWS, (c + 1) * SOFTMAX_ROWS)
        s = s_ref[rows, :]
        if bias_ref is not None and rows.stop > bias_from:
            s = s + bias_ref[rows, :]
        return s if bias_shift is None else s + bias_shift

    def column_max():
        if bias_ref is None and bias_shift is None:
            m8 = s_ref[t_k:t_k + sub, :]
        else:
            m8 = None
            for c in range(t_k // SOFTMAX_ROWS):
                part = jnp.max(chunk(c).reshape(SOFTMAX_ROWS // sub, sub, t_q), axis=0)
                m8 = part if m8 is None else jnp.maximum(m8, part)
        m_new = jnp.max(m8, axis=0, keepdims=True)
        if not first:
            state['m_old'] = m_sc[h]
            m_new = jnp.maximum(state['m_old'], m_new)
        state['m_new'] = m_new

    def exponentials():
        for c in range(t_k // SOFTMAX_ROWS):
            p_ref[c * SOFTMAX_ROWS:(c + 1) * SOFTMAX_ROWS, :] = jnp.exp2(chunk(c) - state['m_new']).astype(BF16)

    def values():
        pv = _dot(vt_ext(), p_ref[...])
        if first:
            acc_sc[h] = pv
        else:
            acc_sc[h] = jnp.exp2(state['m_old'] - state['m_new']) * acc_sc[h] + pv
        m_sc[h] = state['m_new']

    return [column_max, exponentials, values]


def _interleave(score_parts, softmax_stages):
    for i in range(max(len(score_parts), len(softmax_stages))):
        if i < len(score_parts):
            score_parts[i]()
        if i < len(softmax_stages):
            softmax_stages[i]()


def _score_buffers(t):
    return [pltpu.VMEM((2, t + SUBLANES, t), F32)] * 2 + [pltpu.VMEM((2, t, t), BF16)] * 2


def _store_scores(s_ref, cols, s):
    t_k = s.shape[0]
    s_ref[0:t_k, cols] = s
    s_ref[t_k:t_k + SUBLANES, cols] = jnp.max(s.reshape(t_k // SUBLANES, SUBLANES, s.shape[1]), axis=0)


def _flash_sweep(qi, scores, consume):
    near = jnp.maximum(qi - 1, 0)
    n_far = near
    last = jnp.maximum(n_far - 1, 0)
    _interleave(scores(qi, 0), [])
    _interleave(scores(near, 1), consume(qi, 0, 0, None, True))
    _interleave(scores(0, 0), consume(near, 1, 1, jnp.where(qi == 0, NEG_INF, 0.0), False))

    def pair(p, carry):
        j = 2 * p
        _interleave(scores(jnp.minimum(j + 1, last), 1), consume(j, 0, None, None, False))
        _interleave(scores(jnp.minimum(j + 2, last), 0), consume(j + 1, 1, None, None, False))
        return carry

    lax.fori_loop(0, n_far // 2, pair, 0)

    @pl.when(n_far % 2 == 1)
    def _():
        _interleave([], consume(last, 0, None, None, False))


def _moba_prompt_kernel(q_ref, k_ref, vt_ref, kmean_ref, bias_ref, o_ref, m_sc, acc_sc, s0_sc, s1_sc, p0_sc, p1_sc,
                        qx_sc):
    t = ATT_TILE
    qi = pl.program_id(2)
    qt = q_ref[...].astype(F32).T
    row = lax.broadcasted_iota(jnp.int32, (LANES, t), 0)
    head0 = row < DH
    q_heads = (jnp.where(head0, qt, 0.0).astype(BF16), jnp.where(head0, 0.0, qt).astype(BF16))
    kmean = kmean_ref[...].astype(BF16)
    nb = kmean.shape[0]
    blk = lax.broadcasted_iota(jnp.int32, (nb, t), 0)
    cur = qi * BLOCKS_PER_TILE + lax.broadcasted_iota(jnp.int32, (nb, t), 1) // MOBA_BLOCK

    def block_mask(qh):
        gate = jnp.where(blk < cur, _dot(kmean, qh), NEG_INF)
        chosen = blk == cur
        for j in range(MOBA_TOPK):
            best = jnp.max(gate, axis=0, keepdims=True)
            idx = jnp.min(jnp.where(gate == best, blk, nb), axis=0, keepdims=True)
            hit = blk == idx
            chosen = jnp.logical_or(chosen, jnp.logical_and(hit, cur > j))
            gate = jnp.where(hit, -3e38, gate)
        return jnp.where(chosen, 0.0, NEG_INF).astype(BF16)

    unused = jnp.zeros((LANES - nb, t), BF16)
    for h in range(2):
        qx_sc[h] = jnp.concatenate([q_heads[h], block_mask(q_heads[h]), unused], axis=0)
    lane = lax.broadcasted_iota(jnp.int32, (1, LANES), 1)
    ones = jnp.ones((BF16_ROWS, t), BF16)

    s_bufs, p_bufs = (s0_sc, s1_sc), (p0_sc, p1_sc)

    def scores(kj, slot):
        def part(h, c):
            rows = pl.ds(pl.multiple_of(kj * t, t), t)
            onehot = jnp.concatenate(
                [jnp.broadcast_to(jnp.where(lane == kj * BLOCKS_PER_TILE + n, 1.0, 0.0).astype(BF16),
                                  (MOBA_BLOCK, LANES)) for n in range(BLOCKS_PER_TILE)], axis=0)
            k_ext = jnp.concatenate([k_ref[rows, :], onehot], axis=1)
            cols = slice(c * SCORE_COLS, (c + 1) * SCORE_COLS)
            _store_scores(s_bufs[slot].at[h], cols, _dot(k_ext, qx_sc[h, :, cols]))
        return [functools.partial(part, h, c) for h in range(2) for c in range(t // SCORE_COLS)]

    def consume(kj, slot, bias_idx, bias_shift, first):
        stages = []
        for h in range(2):
            vt_ext = lambda h=h: jnp.concatenate([vt_ref[kj, h * DH:(h + 1) * DH, :], ones], axis=0)
            bias = None if bias_idx is None else bias_ref.at[h, bias_idx]
            stages += _flash_stages(s_bufs[slot].at[h], p_bufs[slot].at[h], bias, bias_shift, vt_ext, m_sc,
                                    acc_sc, h, first, bias_from=PREV_TILE_BIAS_FROM if bias_idx == 1 else 0)
        return stages

    _flash_sweep(qi, scores, consume)
    out = [acc_sc[h, 0:DH] / acc_sc[h, DH:DH + 1] for h in range(2)]
    o_ref[...] = jnp.concatenate(out, axis=0).T.astype(o_ref.dtype)


def _moba_prompt(hq, vat16, kmean, bias_a, *, batch, seq, h_a):
    t = ATT_TILE
    nb = kmean.shape[1]
    assert nb % BF16_ROWS == 0 and nb <= LANES
    pairs = h_a * DH // LANES
    seg_blocks = hq.shape[1] // 4 // LANES
    nq = seq // t
    return pl.pallas_call(
        _moba_prompt_kernel,
        grid=(batch, pairs, nq),
        in_specs=[
            pl.BlockSpec((t, LANES), lambda b, j, i: (b * nq + i, j)),
            pl.BlockSpec((seq, LANES), lambda b, j, i: (b, seg_blocks + j)),
            pl.BlockSpec((None, nq, LANES, t), lambda b, j, i: (b, 0, j, 0)),
            pl.BlockSpec((None, nb, LANES), lambda b, j, i: (b, 0, j)),
            pl.BlockSpec((2, 2, t, t), lambda b, j, i: (j, 0, 0, 0)),
        ],
        out_specs=pl.BlockSpec((t, LANES), lambda b, j, i: (b * nq + i, j)),
        out_shape=jax.ShapeDtypeStruct((batch * seq, pairs * LANES), BF16),
        scratch_shapes=([pltpu.VMEM((2, 1, t), F32), pltpu.VMEM((2, DH + BF16_ROWS, t), F32)] + _score_buffers(t)
                        + [pltpu.VMEM((2, 2 * LANES, t), BF16)]),
        compiler_params=_cparams(("parallel", "parallel", "arbitrary")),
    )(hq, hq, vat16, kmean, bias_a)


def _diff_prompt_kernel(lam_ref, q_ref, k_ref, vt_ref, bias_ref, g_ref, o_ref, m_sc, acc_sc,
                        s0_sc, s1_sc, p0_sc, p1_sc, qx_sc, *, out_scale):
    t = ATT_TILE
    dv = LANES
    qi = pl.program_id(2)
    qt = q_ref[...].astype(F32).T
    row = lax.broadcasted_iota(jnp.int32, (LANES, t), 0)
    half0 = row < DH
    qx_sc[0] = jnp.where(half0, qt, 0.0).astype(BF16)
    qx_sc[1] = jnp.where(half0, 0.0, qt).astype(BF16)
    ones = jnp.ones((BF16_ROWS, t), BF16)

    s_bufs, p_bufs = (s0_sc, s1_sc), (p0_sc, p1_sc)

    def scores(kj, slot):
        def part(h, c):
            k = k_ref[pl.ds(pl.multiple_of(kj * t, t), t), :]
            cols = slice(c * SCORE_COLS, (c + 1) * SCORE_COLS)
            _store_scores(s_bufs[slot].at[h], cols, _dot(k, qx_sc[h, :, cols]))
        return [functools.partial(part, h, c) for h in range(2) for c in range(t // SCORE_COLS)]

    def consume(kj, slot, bias_idx, bias_shift, first):
        vt_ext = lambda: jnp.concatenate([vt_ref[kj], ones], axis=0)
        bias = None if bias_idx is None else bias_ref.at[bias_idx]
        stages = []
        for h in range(2):
            stages += _flash_stages(s_bufs[slot].at[h], p_bufs[slot].at[h], bias, bias_shift, vt_ext, m_sc,
                                    acc_sc, h, first, bias_from=PREV_TILE_BIAS_FROM if bias_idx == 1 else 0)
        return stages

    _flash_sweep(qi, scores, consume)
    o = (acc_sc[0, 0:dv] / acc_sc[0, dv:dv + 1]
         - lam_ref[0] * (acc_sc[1, 0:dv] / acc_sc[1, dv:dv + 1]))
    rms = lax.rsqrt(jnp.mean(o * o, axis=0, keepdims=True) + RMS_EPS)
    o_ref[...] = (o * rms * g_ref[...] * out_scale).T.astype(o_ref.dtype)


def _diff_prompt(lam, hq, vbt16, bias_b, subln_g, *, batch, seq, h_b, lam_init):
    t = ATT_TILE
    seg_blocks = hq.shape[1] // 4 // LANES
    nq = seq // t
    g_spread = jnp.broadcast_to(subln_g.reshape(LANES, 1), (LANES, t))
    return pl.pallas_call(
        functools.partial(_diff_prompt_kernel, out_scale=1.0 - lam_init),
        grid=(batch, h_b, nq),
        in_specs=[
            pl.BlockSpec(memory_space=pltpu.SMEM),
            pl.BlockSpec((t, LANES), lambda b, h, i: (b * nq + i, 2 * seg_blocks + h)),
            pl.BlockSpec((seq, LANES), lambda b, h, i: (b, 3 * seg_blocks + h)),
            pl.BlockSpec((None, nq, LANES, t), lambda b, h, i: (b, 0, h, 0)),
            pl.BlockSpec((None, 2, t, t), lambda b, h, i: (h, 0, 0, 0)),
            pl.BlockSpec((LANES, t), lambda b, h, i: (0, 0)),
        ],
        out_specs=pl.BlockSpec((t, LANES), lambda b, h, i: (b * nq + i, h)),
        out_shape=jax.ShapeDtypeStruct((batch * seq, h_b * LANES), BF16),
        scratch_shapes=([pltpu.VMEM((2, 1, t), F32), pltpu.VMEM((2, LANES + BF16_ROWS, t), F32)] + _score_buffers(t)
                        + [pltpu.VMEM((2, LANES, t), BF16)]),
        compiler_params=_cparams(("parallel", "parallel", "arbitrary")),
    )(lam, hq, hq, vbt16, bias_b, g_spread)


def _softmax_tile(s):
    m = jnp.max(s, axis=1, keepdims=True)
    p = jnp.exp2(s - m)
    return m, p, jnp.sum(p, axis=1, keepdims=True)


def _sample_attn_kernel(pt_ref, lam_ref, hq_ref, ka_new_ref, va_new_ref, kb_new_ref, vb_new_ref,
                        bias_last_ref, bias_new_ref, g_ref, *refs,
                        n_steps, blocks_per_step, n_new, h_a, h_b, page, out_scale):
    del pt_ref
    n_pages = 2 * blocks_per_step
    page_refs = refs[:4 * n_pages]
    ya_ref, yb_ref = refs[4 * n_pages:4 * n_pages + 2]
    w_sc, gate_sc, m_sc, l_sc, o_sc = refs[4 * n_pages + 2:]
    g = pl.program_id(1)
    w_a = h_a * DH
    rows_a = h_a * n_new
    rows = rows_a + 2 * h_b * n_new
    n_past = n_steps * blocks_per_step

    at_step = (lambda step: pl.when(g == step)) if n_steps > 1 else (lambda step: (lambda body: body()))

    @at_step(0)
    def _():
        tiled = jnp.concatenate([hq_ref[...]] * (rows // n_new), axis=0)
        r = lax.broadcasted_iota(jnp.int32, tiled.shape, 0)
        c = lax.broadcasted_iota(jnp.int32, tiled.shape, 1)
        w_sc[...] = jnp.where(c // DH == r // n_new, tiled, 0.0).astype(BF16)

    def partial_block(n, s, bias, pv_a, pv_b):
        spread = lambda col: jnp.broadcast_to(col, (rows, LANES))
        gate_sc[n] = spread(jnp.sum(s, axis=1, keepdims=True))
        m, p, l = _softmax_tile(s + bias)
        p = p.astype(BF16)
        o_sc[n, 0:rows_a, :] = pv_a(p[0:rows_a])
        o_sc[n, rows_a:rows, :] = pv_b(p[rows_a:rows])
        m_sc[n] = spread(m)
        l_sc[n] = spread(l)

    def heads_on_lanes(ref):
        return jnp.concatenate([ref[pl.ds(h, page, stride=h_b), :] for h in range(h_b)], axis=1).astype(BF16)

    for u in range(blocks_per_step):
        n = g * blocks_per_step + u
        pages = [page_refs[4 * (2 * u + v):4 * (2 * u + v) + 4] for v in range(2)]
        ka_t = [pg[0][...].reshape(w_a, page).astype(BF16) for pg in pages]
        va_t = [pg[1][...].reshape(w_a, page).astype(BF16) for pg in pages]
        kb = [heads_on_lanes(pg[2]) for pg in pages]
        vb = [heads_on_lanes(pg[3]) for pg in pages]
        q_a = w_sc[0:rows_a, 0:w_a]
        q_b = w_sc[rows_a:rows, w_a:2 * w_a]
        s = jnp.concatenate([
            jnp.concatenate([_dot(q_a, k) for k in ka_t], axis=1),
            jnp.concatenate([_dot_nt(q_b, k) for k in kb], axis=1)], axis=0)
        pv_a = lambda p: sum(_dot_nt(p[:, v * page:(v + 1) * page], va_t[v]) for v in range(2))
        pv_b = lambda p: sum(_dot(p[:, v * page:(v + 1) * page], vb[v]) for v in range(2))
        is_last = (n == n_past - 1).astype(F32)
        partial_block(n, s, bias_last_ref[...] * is_last, pv_a, pv_b)

    @at_step(n_steps - 1)
    def _():
        pad =jnp.zeros((LANES - n_new, w_a), F32)
        padded = lambda ref: jnp.concatenate([ref[...], pad], axis=0).astype(BF16)
        k_cat = jnp.concatenate([padded(ka_new_ref), padded(kb_new_ref)], axis=1)
        va, vb = padded(va_new_ref), padded(vb_new_ref)
        partial_block(n_past, _dot_nt(w_sc[...], k_cat), bias_new_ref[...],
                      lambda p: _dot(p, va), lambda p: _dot(p, vb))

        rs_a = slice(0, rows_a)
        gates = [gate_sc[n, rs_a, :] for n in range(n_past)]
        kept = []
        for n in range(n_past):
            ahead = jnp.zeros((rows_a, LANES), F32)
            for j in range(n_past):
                if j != n:
                    ahead = ahead + jnp.where(gates[j] >= gates[n] if j < n else gates[j] > gates[n], 1.0, 0.0)
            kept.append(ahead < min(MOBA_TOPK, n_past))

        def combine(rs, kept):
            blocks = range(n_past + 1)
            on = lambda n: kept is None or n == n_past or kept[n]
            m_fin = None
            for n in blocks:
                m_n = m_sc[n, rs, :] if on(n) is True else jnp.where(on(n), m_sc[n, rs, :], NEG_INF)
                m_fin = m_n if m_fin is None else jnp.maximum(m_fin, m_n)
            l_fin, o = None, None
            for n in blocks:
                wgt = jnp.exp2(m_sc[n, rs, :] - m_fin)
                if on(n) is not True:
                    wgt = jnp.where(on(n), wgt, 0.0)
                l_n = wgt * l_sc[n, rs, :]
                o_n = jnp.concatenate([wgt] * (w_a // LANES), axis=1) * o_sc[n, rs, :]
                l_fin, o = (l_n, o_n) if o is None else (l_fin + l_n, o + o_n)
            return o / jnp.concatenate([l_fin] * (w_a // LANES), axis=1)

        o = jnp.concatenate([combine(rs_a, kept), combine(slice(rows_a, rows), None)], axis=0)

        col = lax.broadcasted_iota(jnp.int32, (n_new, w_a), 1)
        ya = jnp.zeros((n_new, w_a), F32)
        for h in range(h_a):
            ya = ya + jnp.where(col // DH == h, o[h * n_new:(h + 1) * n_new], 0.0)
        ya_ref[...] = ya
        lam = lam_ref[0]
        dv = w_a // h_b
        outs = []
        for h in range(h_b):
            r0 = rows_a + 2 * h * n_new
            d = (o[r0:r0 + n_new, h * dv:(h + 1) * dv]
                 - lam * o[r0 + n_new:r0 + 2 * n_new, h * dv:(h + 1) * dv])
            rms = lax.rsqrt(jnp.mean(d * d, axis=1, keepdims=True) + RMS_EPS)
            outs.append(d * rms * g_ref[...] * out_scale)
        yb_ref[...] = jnp.concatenate(outs, axis=1)


def _sample_attn(page_table, lam, hq, ka, va, kb, vb, caches, layer, bias_last, bias_new, subln_g,
                 *, n_new, h_a, h_b, lam_init, blocks_per_step):
    dec_batch, n_pages = page_table.shape
    page = caches[0].shape[2]
    assert 2 * page == MOBA_BLOCK and n_pages % (2 * blocks_per_step) == 0
    w_a = h_a * DH
    rows = (h_a + 2 * h_b) * n_new
    n_steps = n_pages // (2 * blocks_per_step)
    pps = 2 * blocks_per_step
    n_blocks = n_pages // 2 + 1
    caches = [caches[0].transpose(0, 1, 3, 4, 2), caches[1].transpose(0, 1, 3, 4, 2),
              caches[2].reshape(caches[2].shape[:2] + (page * h_b, -1)),
              caches[3].reshape(caches[3].shape[:2] + (page * h_b, -1))]

    def page_spec(v, c):
        shape = (None, None) + c.shape[2:]
        zeros = (0,) * (c.ndim - 2)
        return pl.BlockSpec(shape, lambda b, g, pt: (layer, pt[b * n_pages + g * pps + v]) + zeros)

    new_spec = pl.BlockSpec((n_new, w_a), lambda b, g, pt: (b, 0))
    in_specs = [
        pl.BlockSpec(memory_space=pltpu.SMEM),
        pl.BlockSpec((n_new, hq.shape[1]), lambda b, g, pt: (b, 0)),
        new_spec, new_spec, new_spec, new_spec,
        pl.BlockSpec((rows, MOBA_BLOCK), lambda b, g, pt: (0, 0)),
        pl.BlockSpec((rows, LANES), lambda b, g, pt: (0, 0)),
        pl.BlockSpec((1, w_a // h_b), lambda b, g, pt: (0, 0)),
    ]
    operands = [lam, hq, ka, va, kb, vb, bias_last, bias_new, subln_g]
    for v in range(pps):
        for c in caches:
            in_specs.append(page_spec(v, c))
            operands.append(c)
    grid_spec = pltpu.PrefetchScalarGridSpec(
        num_scalar_prefetch=1,
        grid=(dec_batch, n_steps),
        in_specs=in_specs,
        out_specs=[new_spec, new_spec],
        scratch_shapes=[
            pltpu.VMEM((rows, 2 * w_a), BF16),
            pltpu.VMEM((n_blocks, rows, LANES), F32), pltpu.VMEM((n_blocks, rows, LANES), F32),
            pltpu.VMEM((n_blocks, rows, LANES), F32),
            pltpu.VMEM((n_blocks, rows, w_a), F32),
        ],
    )
    return pl.pallas_call(
        functools.partial(_sample_attn_kernel, n_steps=n_steps, blocks_per_step=blocks_per_step, n_new=n_new,
                          h_a=h_a, h_b=h_b, page=page, out_scale=1.0 - lam_init),
        grid_spec=grid_spec,
        out_shape=[jax.ShapeDtypeStruct((dec_batch * n_new, w_a), F32)] * 2,
        compiler_params=_cparams(("parallel", "arbitrary")),
    )(page_table.reshape(-1), *operands)


def _layernorm(x, g, b):
    mu = jnp.mean(x, axis=-1, keepdims=True)
    xc = x - mu
    var = jnp.mean(xc * xc, axis=-1, keepdims=True)
    return xc * lax.rsqrt(var + LN_EPS) * g + b


def _mix_kernel(x_ref, ya_ref, yb_ref, wg_ref, bg_ref, wa_ref, wb_ref, wo_ref, g_ref, b_ref, o_ref, *, alpha):
    x = x_ref[...]
    d = x.shape[1]
    gates = jax.nn.sigmoid(_dot(x.astype(BF16), wg_ref[...]) + bg_ref[...])
    z = (gates[:, :d] * _dot(ya_ref[...].astype(BF16), wa_ref[...])
         + gates[:, d:] * _dot(yb_ref[...].astype(BF16), wb_ref[...]))
    mix = _dot(z.astype(BF16), wo_ref[...])
    o_ref[...] = _layernorm(alpha * x + mix, g_ref[...], b_ref[...])


def _mix(x2, ya, yb, wg, bg, wa, wb, wo, g, b, *, tm, alpha):
    m, d = x2.shape
    w = ya.shape[1]
    row = lambda c: pl.BlockSpec((tm, c), lambda i: (i, 0))
    return pl.pallas_call(
        functools.partial(_mix_kernel, alpha=alpha),
        grid=(m // tm,),
        in_specs=[row(d), row(w), row(w), _resident(wg.shape), _resident(bg.shape), _resident(wa.shape),
                  _resident(wb.shape), _resident(wo.shape), _resident(g.shape), _resident(b.shape)],
        out_specs=row(d),
        out_shape=jax.ShapeDtypeStruct((m, d), F32),
        compiler_params=_cparams(("parallel",)),
    )(x2, ya, yb, wg, bg, wa, wb, wo, g, b)


def _gelu_tanh(x):
    return 0.5 * x * (1.0 + jnp.tanh(math.sqrt(2.0 / math.pi) * (x + 0.044715 * (x * x * x))))


def _ffn_kernel(x_ref, p_ref, prev_ref, wu_ref, cw_ref, cb_ref, wd_ref, g_ref, b_ref, wpg_ref, wp_ref,
                o_ref, st_ref, carry_sc, act_sc, u_sc, *, alpha, seq_rows, n_chunks):
    x = x_ref[...]
    tm, d = x.shape
    xb = x.astype(BF16)
    cols = lambda c: slice(c * FFN_CHUNK, (c + 1) * FFN_CHUNK)
    long_seq = seq_rows is None
    if long_seq:
        first = pl.program_id(1) == 0
    else:
        groups = tm // seq_rows
        sub = lax.broadcasted_iota(jnp.int32, (groups, seq_rows, FFN_CHUNK), 1)

    def conv(c, u):
        cw = cw_ref[:, cols(c)]
        u3 = u.reshape(groups, seq_rows, FFN_CHUNK)
        hist = prev_ref[:, :, c * FFN_CHUNK:(c + 1) * FFN_CHUNK]
        back = lambda k: pltpu.roll(u, k, 0).reshape(groups, seq_rows, FFN_CHUNK)
        u1 = jnp.where(sub == 0, hist[:, 1:2], back(1))
        u2 = jnp.where(sub == 0, hist[:, 0:1], jnp.where(sub == 1, hist[:, 1:2], back(2)))
        st_ref[:, :, c * FFN_CHUNK:(c + 1) * FFN_CHUNK] = u3[:, seq_rows - 2:seq_rows]
        out = cw[0:1] * u2 + cw[1:2] * u1 + cw[2:3] * u3 + cb_ref[:, cols(c)]
        return out.reshape(tm, FFN_CHUNK)

    def up_long(c, slot):
        for half in range(2):
            cc = half * n_chunks + c
            u = _dot(xb, wu_ref[:, cols(cc)])
            u_sc[slot, half, CONV_HIST - (CONV_W - 1):CONV_HIST, :] = carry_sc[cc]
            u_sc[slot, half, CONV_HIST:CONV_HIST + tm, :] = u
            carry_sc[cc] = u[tm - (CONV_W - 1):tm]
            st_ref[0, :, cc * FFN_CHUNK:(cc + 1) * FFN_CHUNK] = u[tm - (CONV_W - 1):tm]

    def act_long(c, slot):
        for r0 in range(0, tm, FFN_ROWS):
            halves = []
            for half in range(2):
                cc = half * n_chunks + c
                cw = cw_ref[:, cols(cc)]
                out = cb_ref[:, cols(cc)]
                for j in range(CONV_W):
                    start = CONV_HIST + r0 - (CONV_W - 1 - j)
                    out = out + cw[j:j + 1] * u_sc[slot, half, start:start + FFN_ROWS, :]
                halves.append(out)
            act_sc[r0:r0 + FFN_ROWS, c * FFN_CHUNK:(c + 1) * FFN_CHUNK] = (
                _gelu_tanh(halves[0]) * halves[1]).astype(BF16)

    if long_seq:
        @pl.when(first)
        def _():
            for cc in range(2 * n_chunks):
                carry_sc[cc] = prev_ref[0, :, cc * FFN_CHUNK:(cc + 1) * FFN_CHUNK]

        up_long(0, 0)
        for c in range(n_chunks):
            if c + 1 < n_chunks:
                up_long(c + 1, (c + 1) % 2)
            act_long(c, c % 2)
    else:
        for c in range(n_chunks):
            c_gate = conv(c, _dot(xb, wu_ref[:, cols(c)]))
            c_val = conv(n_chunks + c, _dot(xb, wu_ref[:, cols(n_chunks + c)]))
            act_sc[:, c * FFN_CHUNK:(c + 1) * FFN_CHUNK] = (_gelu_tanh(c_gate) * c_val).astype(BF16)
    f = _dot(act_sc[...], wd_ref[...])
    y = _layernorm(alpha * x + f, g_ref[...], b_ref[...])
    gate = jax.nn.sigmoid(_dot(y.astype(BF16), wpg_ref[...]))
    o_ref[...] = y + gate * _dot(p_ref[...].astype(BF16), wp_ref[...])


def _ffn(x2, p2, prev, wu, cw, cb, wd, g, b, wpg, wp, *, tm, alpha, batch, seq_rows):
    m, d = x2.shape
    d_up = prev.shape[-1]
    n_chunks = wu.shape[1] // 2 // FFN_CHUNK
    d_ple = p2.shape[1]
    if seq_rows is None:
        tiles = m // batch // tm
        grid = (batch, tiles)
        row = lambda c: pl.BlockSpec((tm, c), lambda bb, i: (bb * tiles + i, 0))
        st_spec = pl.BlockSpec((1, CONV_W - 1, d_up), lambda bb, i: (bb, 0, 0))
        sem = ("parallel", "arbitrary")
    else:
        grid = (1, m // tm)
        row = lambda c: pl.BlockSpec((tm, c), lambda bb, i: (i, 0))
        st_spec = pl.BlockSpec((tm // seq_rows, CONV_W - 1, d_up), lambda bb, i: (i, 0, 0))
        sem = ("arbitrary", "arbitrary")
    n_seq = prev.shape[0]
    return pl.pallas_call(
        functools.partial(_ffn_kernel, alpha=alpha, seq_rows=seq_rows, n_chunks=n_chunks),
        grid=grid,
        in_specs=[row(d), row(d_ple), st_spec, _resident(wu.shape), _resident(cw.shape), _resident(cb.shape),
                  _resident(wd.shape), _resident(g.shape), _resident(b.shape), _resident(wpg.shape),
                  _resident(wp.shape)],
        out_specs=[row(d), st_spec],
        out_shape=[jax.ShapeDtypeStruct((m, d), F32), jax.ShapeDtypeStruct((n_seq, CONV_W - 1, d_up), F32)],
        scratch_shapes=[pltpu.VMEM((2 * n_chunks, CONV_W - 1, FFN_CHUNK), F32),
                        pltpu.VMEM((tm, n_chunks * FFN_CHUNK), BF16),
                        pltpu.VMEM((2, 2, CONV_HIST + tm, FFN_CHUNK), F32)],
        compiler_params=_cparams(sem),
    )(x2, p2, prev, wu, cw, cb, wd, g, b, wpg, wp)


def _layer_weights(l, w):
    return {
        'w_in': w['w_in'][l].astype(BF16),
        'w_gate': w['w_gate'][l].astype(BF16),
        'b_gate': w['b_gate'][l][None, :],
        'w_branch_a': w['w_branch_a'][l].astype(BF16),
        'w_branch_b': w['w_branch_b'][l].astype(BF16),
        'w_out': w['w_out'][l].astype(BF16),
        'ln1_g': w['ln1_g'][l][None, :], 'ln1_b': w['ln1_b'][l][None, :],
        'w_up': w['w_up'][l].astype(BF16),
        'conv_w': w['conv_w'][l],
        'conv_b': w['conv_b'][l][None, :],
        'w_down': w['w_down'][l].astype(BF16),
        'ln2_g': w['ln2_g'][l][None, :], 'ln2_b': w['ln2_b'][l][None, :],
        'w_ple_gate': w['w_ple_gate'][l].astype(BF16),
        'w_ple': w['w_ple'][l].astype(BF16),
        'subln_g': w['diff_subln_g'][l][None, :],
        'lam': (jnp.exp(jnp.sum(w['diff_lq1'][l] * w['diff_lk1'][l]))
                - jnp.exp(jnp.sum(w['diff_lq2'][l] * w['diff_lk2'][l]))
                + (0.8 - 0.6 * math.exp(-0.3 * l))).reshape(1).astype(F32),
    }


def _tail(x2, ya, yb, p2, prev, wl, *, alpha, tm_mix, tm_ffn, batch, seq_rows):
    x1 = _mix(x2, ya, yb, wl['w_gate'], wl['b_gate'], wl['w_branch_a'], wl['w_branch_b'], wl['w_out'],
              wl['ln1_g'], wl['ln1_b'], tm=tm_mix, alpha=alpha)
    return _ffn(x1, p2, prev, wl['w_up'], wl['conv_w'], wl['conv_b'], wl['w_down'], wl['ln2_g'], wl['ln2_b'],
                wl['w_ple_gate'], wl['w_ple'], tm=tm_ffn, alpha=alpha, batch=batch, seq_rows=seq_rows)


def _forward(x_prompt, x_sample, cache_a_k, cache_a_v, cache_b_k, cache_b_v, state_conv, page_table,
             p_prompt, p_sample, w, *, tm_prompt=256, tm_sample=256, blocks_per_step=8):
    depth = w['w_in'].shape[0]
    batch, seq, d = x_prompt.shape
    dec_batch, n_new, _ = x_sample.shape
    h_a, dh = cache_a_k.shape[3], cache_a_k.shape[4]
    h_b, dv = cache_b_v.shape[3], cache_b_v.shape[4]
    assert dh == DH and cache_b_k.shape[4] == 2 * DH and dv == LANES and seq % ATT_TILE == 0
    d_up = state_conv.shape[-1]
    past_len = page_table.shape[1] * cache_a_k.shape[2]
    alpha = (2 * depth) ** 0.25
    tab = w['rel_bias'].T
    bias_prompt = _prompt_bias_tiles(tab)
    bias_last, bias_new = _sample_bias_tiles(tab, h_a, n_new, past_len)
    caches = (cache_a_k, cache_a_v, cache_b_k, cache_b_v)

    xp = x_prompt.reshape(batch * seq, d)
    xs = x_sample.reshape(dec_batch * n_new, d)
    conv0 = jnp.zeros((batch, CONV_W - 1, d_up), F32)
    kv_p, conv_p_all, rows_s = (), [], []
    for l in range(depth):
        wl = _layer_weights(l, w)
        lam_init = 0.8 - 0.6 * math.exp(-0.3 * l)

        hq, *kv_p, vat16, vbt16, kmean = _proj_prompt(xp, wl['w_in'], tuple(kv_p), batch=batch, seq=seq, h_b=h_b)
        ya = _moba_prompt(hq, vat16, kmean.reshape(batch, seq // MOBA_BLOCK, h_a * DH), bias_prompt[:h_a],
                          batch=batch, seq=seq, h_a=h_a)
        yb = _diff_prompt(wl['lam'], hq, vbt16, bias_prompt[h_a:], wl['subln_g'], batch=batch, seq=seq, h_b=h_b,
                          lam_init=lam_init)
        xp, conv_p = _tail(xp, ya, yb, p_prompt[l].reshape(batch * seq, -1), conv0, wl, alpha=alpha,
                           tm_mix=tm_prompt, tm_ffn=tm_prompt, batch=batch, seq_rows=None)
        conv_p_all.append(conv_p)

        hq, ka, va, kb, vb = _proj_sample(xs, wl['w_in'], tm=tm_sample)
        ya, yb = _sample_attn(page_table, wl['lam'], hq, ka, va, kb, vb, caches, l, bias_last, bias_new,
                              wl['subln_g'], n_new=n_new, h_a=h_a, h_b=h_b, lam_init=lam_init,
                              blocks_per_step=blocks_per_step)
        xs, conv_s = _tail(xs, ya, yb, p_sample[l].reshape(dec_batch * n_new, -1), state_conv[l], wl,
                           alpha=alpha, tm_mix=tm_sample, tm_ffn=tm_sample, batch=1, seq_rows=n_new)
        rows_s.append((ka.reshape(dec_batch, n_new, h_a, DH), va.reshape(dec_batch, n_new, h_a, DH),
                       kb.reshape(dec_batch, n_new, h_b, 2 * DH), vb.reshape(dec_batch, n_new, h_b, dv), conv_s))

    kat, vat, kb, vb = kv_p
    token_major = lambda a: a.reshape(depth, batch, h_a, DH, seq).transpose(0, 1, 4, 2, 3)
    stack = lambda rows, i: jnp.stack([r[i] for r in rows])
    return ((xp.reshape(batch, seq, d), xs.reshape(dec_batch, n_new, d), token_major(kat), token_major(vat),
             kb.reshape(depth, batch, seq, h_b, 2 * DH), vb.reshape(depth, batch, seq, h_b, dv),
             jnp.stack(conv_p_all)) + tuple(stack(rows_s, i) for i in range(5)))


def kernel(x_prompt, x_sample, cache_a_k, cache_a_v, cache_b_k, cache_b_v, state_conv, page_table, p_prompt, p_sample, w_in, w_gate, b_gate, w_branch_a, w_branch_b, w_out, rel_bias, diff_lq1, diff_lk1, diff_lq2, diff_lk2, diff_subln_g, ln1_g, ln1_b, w_up, conv_w, conv_b, w_down, ln2_g, ln2_b, w_ple, w_ple_gate):
    w = {'w_in': w_in, 'w_gate': w_gate, 'b_gate': b_gate, 'w_branch_a': w_branch_a,
         'w_branch_b': w_branch_b, 'w_out': w_out, 'rel_bias': rel_bias, 'diff_lq1': diff_lq1,
         'diff_lk1': diff_lk1, 'diff_lq2': diff_lq2, 'diff_lk2': diff_lk2, 'diff_subln_g': diff_subln_g,
         'ln1_g': ln1_g, 'ln1_b': ln1_b, 'w_up': w_up, 'conv_w': conv_w, 'conv_b': conv_b,
         'w_down': w_down, 'ln2_g': ln2_g, 'ln2_b': ln2_b, 'w_ple': w_ple, 'w_ple_gate': w_ple_gate}
    return _forward(x_prompt, x_sample, cache_a_k, cache_a_v, cache_b_k, cache_b_v, state_conv, page_table,
                    p_prompt, p_sample, w)
```

```python
import functools
import math

import numpy as np
import jax
import jax.numpy as jnp
from jax import lax
from jax.experimental import pallas as pl
from jax.experimental.pallas import tpu as pltpu

F32 = jnp.float32
BF16 = jnp.bfloat16

DH = 64
MOBA_BLOCK = 256
MOBA_TOPK = 3
N_BUCKETS = 32
MAX_DISTANCE = 128
CONV_W = 3
LN_EPS = 1e-5
RMS_EPS = 1e-5
NEG_INF = -1e30
LOG2E = math.log2(math.e)
LANES = 128
SUBLANES = 8
BF16_ROWS = 16
VMEM_LIMIT = 56 * 1024 * 1024

ATT_TILE = 2 * MOBA_BLOCK
BLOCKS_PER_TILE = ATT_TILE // MOBA_BLOCK
FFN_CHUNK = 256
FFN_ROWS = 64
CONV_HIST = 8


def _cparams(sem):
    return pltpu.CompilerParams(dimension_semantics=sem, vmem_limit_bytes=VMEM_LIMIT)


def _dot(a, b):
    return jnp.dot(a, b, preferred_element_type=F32)


def _dot_nt(a, b):
    return lax.dot_general(a, b, (((1,), (1,)), ((), ())), preferred_element_type=F32)


def _resident(shape):
    return pl.BlockSpec(shape, lambda *_: (0,) * len(shape), pipeline_mode=pl.Buffered(1))


def _t5_bucket_np(dist):
    dist = np.maximum(dist, 0)
    exact = N_BUCKETS // 2
    logv = (np.log(np.maximum(dist, 1).astype(np.float32) / np.float32(exact))
            / np.float32(math.log(MAX_DISTANCE / exact))).astype(np.float32)
    large = np.minimum(exact + (logv * np.float32(N_BUCKETS - exact)).astype(np.int32), N_BUCKETS - 1)
    return np.where(dist < exact, dist, large).astype(np.int32)


def _bias_lookup(tab_rows, buckets, per_row=False):
    shape = buckets.shape if per_row else (tab_rows.shape[0],) + buckets.shape
    out = jnp.zeros(shape, F32)
    for n in np.unique(buckets):
        col = tab_rows[:, n].reshape((-1,) + (1,) * (len(shape) - 1))
        out = jnp.where(jnp.asarray(buckets == n), col, out)
    return out


def _prompt_bias_tiles(tab):
    t, bsz = ATT_TILE, MAX_DISTANCE
    far = tab[:, N_BUCKETS - 1][:, None, None]
    r = np.arange(bsz)[:, None]
    s = np.arange(bsz)[None, :]

    def block(delta):
        dist = delta * bsz + s - r
        return jnp.where(jnp.asarray(dist >= 0)[None], (_bias_lookup(tab, _t5_bucket_np(dist)) - far) * LOG2E,
                         NEG_INF)

    delta = np.arange(2 * t)[None, :] // bsz - np.arange(t)[:, None] // bsz
    reps = (1, t // bsz, 2 * t // bsz)
    both = jnp.where(jnp.asarray(delta == 0)[None], jnp.tile(block(0), reps),
                     jnp.where(jnp.asarray(delta == 1)[None], jnp.tile(block(1), reps),
                               jnp.where(jnp.asarray(delta < 0)[None], NEG_INF, 0.0)))
    return jnp.stack([both[:, :, :t], both[:, :, t:]], axis=1).astype(F32)


def _sample_bias_tiles(tab, h_a, n_new, past_len):
    h_b = tab.shape[0] - h_a
    rows = (h_a + 2 * h_b) * n_new
    r = np.arange(rows)
    qi = (r % n_new)[:, None]
    head = np.where(r < h_a * n_new, r // n_new, h_a + (r - h_a * n_new) // (2 * n_new))
    onehot = jnp.asarray(head[:, None] == np.arange(tab.shape[0])[None, :], F32)
    tab_rows = jnp.dot(onehot, tab, precision=lax.Precision.HIGHEST)
    far = tab_rows[:, N_BUCKETS - 1:]
    t = np.arange(MOBA_BLOCK)[None, :]
    dist_last = past_len + qi - (past_len - MOBA_BLOCK + t)
    last = (_bias_lookup(tab_rows, _t5_bucket_np(dist_last), per_row=True) - far) * LOG2E
    j = np.arange(LANES)[None, :]
    new = jnp.where(jnp.asarray((j <= qi) & (j < n_new)),
                    (_bias_lookup(tab_rows, _t5_bucket_np(qi - j), per_row=True) - far) * LOG2E, NEG_INF)
    return last.astype(F32), new.astype(F32)


def _proj_prompt_kernel(x_ref, w_ref, *refs, seg, q_scale, h_b, n_prev):
    prev = refs[:4] if n_prev else ()
    hq_ref, kat_ref, vat_ref, kb_ref, vb_ref, vat16_ref, vbt16_ref, kmean_ref = refs[len(prev):]
    for src, dst in zip(prev, (kat_ref, vat_ref, kb_ref, vb_ref)):
        dst[0:n_prev] = src[...]
    x = x_ref[...].astype(BF16)
    dv = seg // h_b
    for s in range(6):
        h = _dot(x, w_ref[:, s * seg:(s + 1) * seg])
        if s == 0:
            hq_ref[:, 0:seg] = (h * q_scale).astype(BF16)
        elif s == 1:
            hq_ref[:, seg:2 * seg] = h.astype(BF16)
            for n in range(BLOCKS_PER_TILE):
                blk = h[n * MOBA_BLOCK:(n + 1) * MOBA_BLOCK]
                kmean_ref[n] = jnp.sum(blk, axis=0, keepdims=True) * (1.0 / MOBA_BLOCK)
            kat_ref[n_prev] = h.T
        elif s == 2:
            ht = h.T
            vat_ref[n_prev] = ht
            vat16_ref[...] = ht.astype(BF16)
        elif s == 3:
            hq_ref[:, 2 * seg:3 * seg] = (h * q_scale).astype(BF16)
        elif s == 4:
            hq_ref[:, 3 * seg:4 * seg] = h.astype(BF16)
            for hh in range(h_b):
                kb_ref[n_prev, :, hh, :] = h[:, hh * dv:(hh + 1) * dv]
        else:
            for hh in range(h_b):
                vb_ref[n_prev, :, hh, :] = h[:, hh * dv:(hh + 1) * dv]
            vbt16_ref[...] = h.T.astype(BF16)


def _proj_prompt(x2, w_in_l, prev_rows, *, batch, seq, h_b):
    tm = ATT_TILE
    m, d = x2.shape
    seg = w_in_l.shape[1] // 6
    nt = seq // tm
    dv = seg // h_b
    n_prev = prev_rows[0].shape[0] if prev_rows else 0
    row = lambda w: pl.BlockSpec((tm, w), lambda b, i: (b * nt + i, 0))
    feat = lambda n: pl.BlockSpec((n, None, seg, tm), lambda b, i: (0, b, 0, i))
    heads = lambda n: pl.BlockSpec((n, tm, h_b, dv), lambda b, i: (0, b * nt + i, 0, 0))
    tiles = pl.BlockSpec((None, None, seg, tm), lambda b, i: (b, i, 0, 0))
    stacked = lambda n: [feat(n), feat(n), heads(n), heads(n)]
    return pl.pallas_call(
        functools.partial(_proj_prompt_kernel, seg=seg, q_scale=DH ** -0.5 * LOG2E, h_b=h_b, n_prev=n_prev),
        grid=(batch, nt),
        in_specs=[row(d), _resident(w_in_l.shape)] + (stacked(n_prev) if n_prev else []),
        out_specs=[row(4 * seg)] + stacked(n_prev + 1) + [
            tiles, tiles, pl.BlockSpec((BLOCKS_PER_TILE, 1, seg), lambda b, i: (b * nt + i, 0, 0))],
        out_shape=[jax.ShapeDtypeStruct((m, 4 * seg), BF16),
                   jax.ShapeDtypeStruct((n_prev + 1, batch, seg, seq), F32),
                   jax.ShapeDtypeStruct((n_prev + 1, batch, seg, seq), F32),
                   jax.ShapeDtypeStruct((n_prev + 1, m, h_b, dv), F32),
                   jax.ShapeDtypeStruct((n_prev + 1, m, h_b, dv), F32),
                   jax.ShapeDtypeStruct((batch, nt, seg, tm), BF16), jax.ShapeDtypeStruct((batch, nt, seg, tm), BF16),
                   jax.ShapeDtypeStruct((m // MOBA_BLOCK, 1, seg), F32)],
        compiler_params=_cparams(("parallel", "parallel")),
    )(x2, w_in_l, *prev_rows)


def _proj_sample_kernel(x_ref, w_ref, hq_ref, ka_ref, va_ref, kb_ref, vb_ref, *, seg, q_scale):
    x = x_ref[...].astype(BF16)
    outs = (None, ka_ref, va_ref, None, kb_ref, vb_ref)
    for s in range(6):
        h = _dot(x, w_ref[:, s * seg:(s + 1) * seg])
        if s == 0:
            hq_ref[:, 0:seg] = h * q_scale
        elif s == 3:
            hq_ref[:, seg:2 * seg] = h * q_scale
        else:
            outs[s][...] = h


def _proj_sample(x2, w_in_l, *, tm):
    m, d = x2.shape
    seg = w_in_l.shape[1] // 6
    row = lambda w: pl.BlockSpec((tm, w), lambda i: (i, 0))
    return pl.pallas_call(
        functools.partial(_proj_sample_kernel, seg=seg, q_scale=DH ** -0.5 * LOG2E),
        grid=(m // tm,),
        in_specs=[row(d), _resident(w_in_l.shape)],
        out_specs=[row(2 * seg)] + [row(seg)] * 4,
        out_shape=[jax.ShapeDtypeStruct((m, 2 * seg), F32)] + [jax.ShapeDtypeStruct((m, seg), F32)] * 4,
        compiler_params=_cparams(("parallel",)),
    )(x2, w_in_l)


PREV_TILE_BIAS_FROM = ATT_TILE - MAX_DISTANCE
SCORE_COLS = 256
SOFTMAX_ROWS = 32


def _flash_stages(s_ref, p_ref, bias_ref, bias_shift, vt_ext, m_sc, acc_sc, h, first, bias_from=0, exp_dtype=F32):
    t_k, t_q = s_ref.shape[0] - SUBLANES, s_ref.shape[1]
    sub = SUBLANES
    state = {}

    def chunk(c):
        rows = slice(c * SOFTMAX_ROWS, (c + 1) * SOFTMAX_ROWS)
        s = s_ref[rows, :]
        if bias_ref is not None and rows.stop > bias_from:
            s = s + bias_ref[rows, :]
        return s if bias_shift is None else s + bias_shift

    def column_max():
        if bias_ref is None and bias_shift is None:
            m8 = s_ref[t_k:t_k + sub, :]
        else:
            m8 = None
            for c in range(t_k // SOFTMAX_ROWS):
                part = jnp.max(chunk(c).reshape(SOFTMAX_ROWS // sub, sub, t_q), axis=0)
                m8 = part if m8 is None else jnp.maximum(m8, part)
        m_new = jnp.max(m8, axis=0, keepdims=True)
        if not first:
            state['m_old'] = m_sc[h]
            m_new = jnp.maximum(state['m_old'], m_new)
        state['m_new'] = m_new

    def exponentials():
        for c in range(t_k // SOFTMAX_ROWS):
            p_ref[c * SOFTMAX_ROWS:(c + 1) * SOFTMAX_ROWS, :] = jnp.exp2(
                (chunk(c) - state['m_new']).astype(exp_dtype)).astype(BF16)

    def values():
        pv = _dot(vt_ext(), p_ref[...])
        if first:
            acc_sc[h] = pv
        else:
            acc_sc[h] = jnp.exp2(state['m_old'] - state['m_new']) * acc_sc[h] + pv
        m_sc[h] = state['m_new']

    return [column_max, exponentials, values]


def _interleave(score_parts, softmax_stages):
    for i in range(max(len(score_parts), len(softmax_stages))):
        if i < len(score_parts):
            score_parts[i]()
        if i < len(softmax_stages):
            softmax_stages[i]()


def _score_buffers(t):
    return [pltpu.VMEM((2, t + SUBLANES, t), F32)] * 2 + [pltpu.VMEM((2, t, t), BF16)] * 2


def _store_scores(s_ref, cols, s):
    t_k = s.shape[0]
    s_ref[0:t_k, cols] = s
    s_ref[t_k:t_k + SUBLANES, cols] = jnp.max(s.reshape(t_k // SUBLANES, SUBLANES, s.shape[1]), axis=0)


def _flash_sweep(qi, scores, consume):
    near = jnp.maximum(qi - 1, 0)
    n_far = near
    last = jnp.maximum(n_far - 1, 0)
    _interleave(scores(qi, 0), [])
    _interleave(scores(near, 1), consume(qi, 0, 0, None, True))
    _interleave(scores(0, 0), consume(near, 1, 1, jnp.where(qi == 0, NEG_INF, 0.0), False))

    def pair(p, carry):
        j = 2 * p
        _interleave(scores(jnp.minimum(j + 1, last), 1), consume(j, 0, None, None, False))
        _interleave(scores(jnp.minimum(j + 2, last), 0), consume(j + 1, 1, None, None, False))
        return carry

    lax.fori_loop(0, n_far // 2, pair, 0)

    @pl.when(n_far % 2 == 1)
    def _():
        _interleave([], consume(last, 0, None, None, False))


def _moba_prompt_kernel(q_ref, k_ref, vt_ref, kmean_ref, bias_ref, o_ref, m_sc, acc_sc, s0_sc, s1_sc, p0_sc, p1_sc,
                        qx_sc):
    t = ATT_TILE
    qi = pl.program_id(2)
    qt = q_ref[...].astype(F32).T
    row = lax.broadcasted_iota(jnp.int32, (LANES, t), 0)
    head0 = row < DH
    q_heads = (jnp.where(head0, qt, 0.0).astype(BF16), jnp.where(head0, 0.0, qt).astype(BF16))
    kmean = kmean_ref[...].astype(BF16)
    nb = kmean.shape[0]
    blk = lax.broadcasted_iota(jnp.int32, (nb, t), 0)
    cur = qi * BLOCKS_PER_TILE + lax.broadcasted_iota(jnp.int32, (nb, t), 1) // MOBA_BLOCK

    def block_mask(qh):
        gate = jnp.where(blk < cur, _dot(kmean, qh), NEG_INF)
        chosen = blk == cur
        for j in range(MOBA_TOPK):
            best = jnp.max(gate, axis=0, keepdims=True)
            idx = jnp.min(jnp.where(gate == best, blk, nb), axis=0, keepdims=True)
            hit = blk == idx
            chosen = jnp.logical_or(chosen, jnp.logical_and(hit, cur > j))
            gate = jnp.where(hit, -3e38, gate)
        return jnp.where(chosen, 0.0, NEG_INF).astype(BF16)

    unused = jnp.zeros((LANES - nb, t), BF16)
    for h in range(2):
        qx_sc[h] = jnp.concatenate([q_heads[h], block_mask(q_heads[h]), unused], axis=0)
    lane = lax.broadcasted_iota(jnp.int32, (1, LANES), 1)
    ones = jnp.ones((BF16_ROWS, t), BF16)

    s_bufs, p_bufs = (s0_sc, s1_sc), (p0_sc, p1_sc)

    def scores(kj, slot):
        def part(h, c):
            rows = pl.ds(pl.multiple_of(kj * t, t), t)
            onehot = jnp.concatenate(
                [jnp.broadcast_to(jnp.where(lane == kj * BLOCKS_PER_TILE + n, 1.0, 0.0).astype(BF16),
                                  (MOBA_BLOCK, LANES)) for n in range(BLOCKS_PER_TILE)], axis=0)
            k_ext = jnp.concatenate([k_ref[rows, :], onehot], axis=1)
            cols = slice(c * SCORE_COLS, (c + 1) * SCORE_COLS)
            _store_scores(s_bufs[slot].at[h], cols, _dot(k_ext, qx_sc[h, :, cols]))
        return [functools.partial(part, h, c) for h in range(2) for c in range(t // SCORE_COLS)]

    def consume(kj, slot, bias_idx, bias_shift, first):
        stages = []
        for h in range(2):
            vt_ext = lambda h=h: jnp.concatenate([vt_ref[kj, h * DH:(h + 1) * DH, :], ones], axis=0)
            bias = None if bias_idx is None else bias_ref.at[h, bias_idx]
            stages += _flash_stages(s_bufs[slot].at[h], p_bufs[slot].at[h], bias, bias_shift, vt_ext, m_sc,
                                    acc_sc, h, first, bias_from=PREV_TILE_BIAS_FROM if bias_idx == 1 else 0)
        return stages

    _flash_sweep(qi, scores, consume)
    out = [acc_sc[h, 0:DH] / acc_sc[h, DH:DH + 1] for h in range(2)]
    o_ref[...] = jnp.concatenate(out, axis=0).T.astype(o_ref.dtype)


def _moba_prompt(hq, vat16, kmean, bias_a, *, batch, seq, h_a):
    t = ATT_TILE
    nb = kmean.shape[1]
    assert nb % BF16_ROWS == 0 and nb <= LANES
    pairs = h_a * DH // LANES
    seg_blocks = hq.shape[1] // 4 // LANES
    nq = seq // t
    return pl.pallas_call(
        _moba_prompt_kernel,
        grid=(batch, pairs, nq),
        in_specs=[
            pl.BlockSpec((t, LANES), lambda b, j, i: (b * nq + i, j)),
            pl.BlockSpec((seq, LANES), lambda b, j, i: (b, seg_blocks + j)),
            pl.BlockSpec((None, nq, LANES, t), lambda b, j, i: (b, 0, j, 0)),
            pl.BlockSpec((None, nb, LANES), lambda b, j, i: (b, 0, j)),
            pl.BlockSpec((2, 2, t, t), lambda b, j, i: (j, 0, 0, 0)),
        ],
        out_specs=pl.BlockSpec((t, LANES), lambda b, j, i: (b * nq + i, j)),
        out_shape=jax.ShapeDtypeStruct((batch * seq, pairs * LANES), BF16),
        scratch_shapes=([pltpu.VMEM((2, 1, t), F32), pltpu.VMEM((2, DH + BF16_ROWS, t), F32)] + _score_buffers(t)
                        + [pltpu.VMEM((2, 2 * LANES, t), BF16)]),
        compiler_params=_cparams(("parallel", "parallel", "arbitrary")),
    )(hq, hq, vat16, kmean, bias_a)


def _diff_prompt_kernel(lam_ref, q_ref, k_ref, vt_ref, bias_ref, g_ref, o_ref, m_sc, acc_sc,
                        s0_sc, s1_sc, p0_sc, p1_sc, qx_sc, *, out_scale):
    t = ATT_TILE
    dv = LANES
    qi = pl.program_id(2)
    qt = q_ref[...].astype(F32).T
    row = lax.broadcasted_iota(jnp.int32, (LANES, t), 0)
    half0 = row < DH
    qx_sc[0] = jnp.where(half0, qt, 0.0).astype(BF16)
    qx_sc[1] = jnp.where(half0, 0.0, qt).astype(BF16)
    ones = jnp.ones((BF16_ROWS, t), BF16)

    s_bufs, p_bufs = (s0_sc, s1_sc), (p0_sc, p1_sc)

    def scores(kj, slot):
        def part(h, c):
            k = k_ref[pl.ds(pl.multiple_of(kj * t, t), t), :]
            cols = slice(c * SCORE_COLS, (c + 1) * SCORE_COLS)
            _store_scores(s_bufs[slot].at[h], cols, _dot(k, qx_sc[h, :, cols]))
        return [functools.partial(part, h, c) for h in range(2) for c in range(t // SCORE_COLS)]

    def consume(kj, slot, bias_idx, bias_shift, first):
        vt_ext = lambda: jnp.concatenate([vt_ref[kj], ones], axis=0)
        bias = None if bias_idx is None else bias_ref.at[bias_idx]
        stages = []
        for h in range(2):
            stages += _flash_stages(s_bufs[slot].at[h], p_bufs[slot].at[h], bias, bias_shift, vt_ext, m_sc,
                                    acc_sc, h, first, bias_from=PREV_TILE_BIAS_FROM if bias_idx == 1 else 0,
                                    exp_dtype=BF16)
        return stages

    _flash_sweep(qi, scores, consume)
    o = (acc_sc[0, 0:dv] / acc_sc[0, dv:dv + 1]
         - lam_ref[0] * (acc_sc[1, 0:dv] / acc_sc[1, dv:dv + 1]))
    rms = lax.rsqrt(jnp.mean(o * o, axis=0, keepdims=True) + RMS_EPS)
    o_ref[...] = (o * rms * g_ref[...] * out_scale).T.astype(o_ref.dtype)


def _diff_prompt(lam, hq, vbt16, bias_b, subln_g, *, batch, seq, h_b, lam_init):
    t = ATT_TILE
    seg_blocks = hq.shape[1] // 4 // LANES
    nq = seq // t
    g_spread = jnp.broadcast_to(subln_g.reshape(LANES, 1), (LANES, t))
    return pl.pallas_call(
        functools.partial(_diff_prompt_kernel, out_scale=1.0 - lam_init),
        grid=(batch, h_b, nq),
        in_specs=[
            pl.BlockSpec(memory_space=pltpu.SMEM),
            pl.BlockSpec((t, LANES), lambda b, h, i: (b * nq + i, 2 * seg_blocks + h)),
            pl.BlockSpec((seq, LANES), lambda b, h, i: (b, 3 * seg_blocks + h)),
            pl.BlockSpec((None, nq, LANES, t), lambda b, h, i: (b, 0, h, 0)),
            pl.BlockSpec((None, 2, t, t), lambda b, h, i: (h, 0, 0, 0)),
            pl.BlockSpec((LANES, t), lambda b, h, i: (0, 0)),
        ],
        out_specs=pl.BlockSpec((t, LANES), lambda b, h, i: (b * nq + i, h)),
        out_shape=jax.ShapeDtypeStruct((batch * seq, h_b * LANES), BF16),
        scratch_shapes=([pltpu.VMEM((2, 1, t), F32), pltpu.VMEM((2, LANES + BF16_ROWS, t), F32)] + _score_buffers(t)
                        + [pltpu.VMEM((2, LANES, t), BF16)]),
        compiler_params=_cparams(("parallel", "parallel", "arbitrary")),
    )(lam, hq, hq, vbt16, bias_b, g_spread)


def _softmax_tile(s):
    m = jnp.max(s, axis=1, keepdims=True)
    p = jnp.exp2(s - m)
    return m, p, jnp.sum(p, axis=1, keepdims=True)


def _sample_attn_kernel(pt_ref, lam_ref, hq_ref, ka_new_ref, va_new_ref, kb_new_ref, vb_new_ref,
                        bias_last_ref, bias_new_ref, g_ref, *refs,
                        n_steps, blocks_per_step, n_new, h_a, h_b, page, out_scale):
    del pt_ref
    n_pages = 2 * blocks_per_step
    page_refs = refs[:4 * n_pages]
    ya_ref, yb_ref = refs[4 * n_pages:4 * n_pages + 2]
    w_sc, gate_sc, m_sc, l_sc, o_sc = refs[4 * n_pages + 2:]
    g = pl.program_id(1)
    w_a = h_a * DH
    rows_a = h_a * n_new
    rows = rows_a + 2 * h_b * n_new
    n_past = n_steps * blocks_per_step

    at_step = (lambda step: pl.when(g == step)) if n_steps > 1 else (lambda step: (lambda body: body()))

    @at_step(0)
    def _():
        tiled = jnp.concatenate([hq_ref[...]] * (rows // n_new), axis=0)
        r = lax.broadcasted_iota(jnp.int32, tiled.shape, 0)
        c = lax.broadcasted_iota(jnp.int32, tiled.shape, 1)
        w_sc[...] = jnp.where(c // DH == r // n_new, tiled, 0.0).astype(BF16)

    def partial_block(n, s, bias, pv_a, pv_b):
        spread = lambda col: jnp.broadcast_to(col, (rows, LANES))
        gate_sc[n] = spread(jnp.sum(s, axis=1, keepdims=True))
        m, p, l = _softmax_tile(s + bias)
        p = p.astype(BF16)
        o_sc[n, 0:rows_a, :] = pv_a(p[0:rows_a])
        o_sc[n, rows_a:rows, :] = pv_b(p[rows_a:rows])
        m_sc[n] = spread(m)
        l_sc[n] = spread(l)

    def heads_on_lanes(ref):
        return jnp.concatenate([ref[pl.ds(h, page, stride=h_b), :] for h in range(h_b)], axis=1).astype(BF16)

    for u in range(blocks_per_step):
        n = g * blocks_per_step + u
        pages = [page_refs[4 * (2 * u + v):4 * (2 * u + v) + 4] for v in range(2)]
        ka_t = [pg[0][...].reshape(w_a, page).astype(BF16) for pg in pages]
        va_t = [pg[1][...].reshape(w_a, page).astype(BF16) for pg in pages]
        kb = [heads_on_lanes(pg[2]) for pg in pages]
        vb = [heads_on_lanes(pg[3]) for pg in pages]
        q_a = w_sc[0:rows_a, 0:w_a]
        q_b = w_sc[rows_a:rows, w_a:2 * w_a]
        s = jnp.concatenate([
            jnp.concatenate([_dot(q_a, k) for k in ka_t], axis=1),
            jnp.concatenate([_dot_nt(q_b, k) for k in kb], axis=1)], axis=0)
        pv_a = lambda p: sum(_dot_nt(p[:, v * page:(v + 1) * page], va_t[v]) for v in range(2))
        pv_b = lambda p: sum(_dot(p[:, v * page:(v + 1) * page], vb[v]) for v in range(2))
        is_last = (n == n_past - 1).astype(F32)
        partial_block(n, s, bias_last_ref[...] * is_last, pv_a, pv_b)

    @at_step(n_steps - 1)
    def _():
        pad =jnp.zeros((LANES - n_new, w_a), F32)
        padded = lambda ref: jnp.concatenate([ref[...], pad], axis=0).astype(BF16)
        k_cat = jnp.concatenate([padded(ka_new_ref), padded(kb_new_ref)], axis=1)
        va, vb = padded(va_new_ref), padded(vb_new_ref)
        partial_block(n_past, _dot_nt(w_sc[...], k_cat), bias_new_ref[...],
                      lambda p: _dot(p, va), lambda p: _dot(p, vb))

        rs_a = slice(0, rows_a)
        gates = [gate_sc[n, rs_a, :] for n in range(n_past)]
        kept = []
        for n in range(n_past):
            ahead = jnp.zeros((rows_a, LANES), F32)
            for j in range(n_past):
                if j != n:
                    ahead = ahead + jnp.where(gates[j] >= gates[n] if j < n else gates[j] > gates[n], 1.0, 0.0)
            kept.append(ahead < min(MOBA_TOPK, n_past))

        def combine(rs, kept):
            blocks = range(n_past + 1)
            on = lambda n: kept is None or n == n_past or kept[n]
            m_fin = None
            for n in blocks:
                m_n = m_sc[n, rs, :] if on(n) is True else jnp.where(on(n), m_sc[n, rs, :], NEG_INF)
                m_fin = m_n if m_fin is None else jnp.maximum(m_fin, m_n)
            l_fin, o = None, None
            for n in blocks:
                wgt = jnp.exp2(m_sc[n, rs, :] - m_fin)
                if on(n) is not True:
                    wgt = jnp.where(on(n), wgt, 0.0)
                l_n = wgt * l_sc[n, rs, :]
                o_n = jnp.concatenate([wgt] * (w_a // LANES), axis=1) * o_sc[n, rs, :]
                l_fin, o = (l_n, o_n) if o is None else (l_fin + l_n, o + o_n)
            return o / jnp.concatenate([l_fin] * (w_a // LANES), axis=1)

        o = jnp.concatenate([combine(rs_a, kept), combine(slice(rows_a, rows), None)], axis=0)

        col = lax.broadcasted_iota(jnp.int32, (n_new, w_a), 1)
        ya = jnp.zeros((n_new, w_a), F32)
        for h in range(h_a):
            ya = ya + jnp.where(col // DH == h, o[h * n_new:(h + 1) * n_new], 0.0)
        ya_ref[...] = ya
        lam = lam_ref[0]
        dv = w_a // h_b
        outs = []
        for h in range(h_b):
            r0 = rows_a + 2 * h * n_new
            d = (o[r0:r0 + n_new, h * dv:(h + 1) * dv]
                 - lam * o[r0 + n_new:r0 + 2 * n_new, h * dv:(h + 1) * dv])
            rms = lax.rsqrt(jnp.mean(d * d, axis=1, keepdims=True) + RMS_EPS)
            outs.append(d * rms * g_ref[...] * out_scale)
        yb_ref[...] = jnp.concatenate(outs, axis=1)


def _sample_attn(page_table, lam, hq, ka, va, kb, vb, caches, layer, bias_last, bias_new, subln_g,
                 *, n_new, h_a, h_b, lam_init, blocks_per_step):
    dec_batch, n_pages = page_table.shape
    page = caches[0].shape[2]
    assert 2 * page == MOBA_BLOCK and n_pages % (2 * blocks_per_step) == 0
    w_a = h_a * DH
    rows = (h_a + 2 * h_b) * n_new
    n_steps = n_pages // (2 * blocks_per_step)
    pps = 2 * blocks_per_step
    n_blocks = n_pages // 2 + 1
    caches = [caches[0].transpose(0, 1, 3, 4, 2), caches[1].transpose(0, 1, 3, 4, 2),
              caches[2].reshape(caches[2].shape[:2] + (page * h_b, -1)),
              caches[3].reshape(caches[3].shape[:2] + (page * h_b, -1))]

    def page_spec(v, c):
        shape = (None, None) + c.shape[2:]
        zeros = (0,) * (c.ndim - 2)
        return pl.BlockSpec(shape, lambda b, g, pt: (layer, pt[b * n_pages + g * pps + v]) + zeros)

    new_spec = pl.BlockSpec((n_new, w_a), lambda b, g, pt: (b, 0))
    in_specs = [
        pl.BlockSpec(memory_space=pltpu.SMEM),
        pl.BlockSpec((n_new, hq.shape[1]), lambda b, g, pt: (b, 0)),
        new_spec, new_spec, new_spec, new_spec,
        pl.BlockSpec((rows, MOBA_BLOCK), lambda b, g, pt: (0, 0)),
        pl.BlockSpec((rows, LANES), lambda b, g, pt: (0, 0)),
        pl.BlockSpec((1, w_a // h_b), lambda b, g, pt: (0, 0)),
    ]
    operands = [lam, hq, ka, va, kb, vb, bias_last, bias_new, subln_g]
    for v in range(pps):
        for c in caches:
            in_specs.append(page_spec(v, c))
            operands.append(c)
    grid_spec = pltpu.PrefetchScalarGridSpec(
        num_scalar_prefetch=1,
        grid=(dec_batch, n_steps),
        in_specs=in_specs,
        out_specs=[new_spec, new_spec],
        scratch_shapes=[
            pltpu.VMEM((rows, 2 * w_a), BF16),
            pltpu.VMEM((n_blocks, rows, LANES), F32), pltpu.VMEM((n_blocks, rows, LANES), F32),
            pltpu.VMEM((n_blocks, rows, LANES), F32),
            pltpu.VMEM((n_blocks, rows, w_a), F32),
        ],
    )
    return pl.pallas_call(
        functools.partial(_sample_attn_kernel, n_steps=n_steps, blocks_per_step=blocks_per_step, n_new=n_new,
                          h_a=h_a, h_b=h_b, page=page, out_scale=1.0 - lam_init),
        grid_spec=grid_spec,
        out_shape=[jax.ShapeDtypeStruct((dec_batch * n_new, w_a), F32)] * 2,
        compiler_params=_cparams(("parallel", "arbitrary")),
    )(page_table.reshape(-1), *operands)


def _layernorm(x, g, b):
    mu = jnp.mean(x, axis=-1, keepdims=True)
    xc = x - mu
    var = jnp.mean(xc * xc, axis=-1, keepdims=True)
    return xc * lax.rsqrt(var + LN_EPS) * g + b


def _mix_kernel(x_ref, ya_ref, yb_ref, wg_ref, bg_ref, wa_ref, wb_ref, wo_ref, g_ref, b_ref, o_ref, *, alpha):
    x = x_ref[...]
    d = x.shape[1]
    gates = jax.nn.sigmoid(_dot(x.astype(BF16), wg_ref[...]) + bg_ref[...])
    z = (gates[:, :d] * _dot(ya_ref[...].astype(BF16), wa_ref[...])
         + gates[:, d:] * _dot(yb_ref[...].astype(BF16), wb_ref[...]))
    mix = _dot(z.astype(BF16), wo_ref[...])
    o_ref[...] = _layernorm(alpha * x + mix, g_ref[...], b_ref[...])


def _mix(x2, ya, yb, wg, bg, wa, wb, wo, g, b, *, tm, alpha):
    m, d = x2.shape
    w = ya.shape[1]
    row = lambda c: pl.BlockSpec((tm, c), lambda i: (i, 0))
    return pl.pallas_call(
        functools.partial(_mix_kernel, alpha=alpha),
        grid=(m // tm,),
        in_specs=[row(d), row(w), row(w), _resident(wg.shape), _resident(bg.shape), _resident(wa.shape),
                  _resident(wb.shape), _resident(wo.shape), _resident(g.shape), _resident(b.shape)],
        out_specs=row(d),
        out_shape=jax.ShapeDtypeStruct((m, d), F32),
        compiler_params=_cparams(("parallel",)),
    )(x2, ya, yb, wg, bg, wa, wb, wo, g, b)


def _gelu_tanh(x):
    return 0.5 * x * (1.0 + jnp.tanh(math.sqrt(2.0 / math.pi) * (x + 0.044715 * (x * x * x))))


def _ffn_kernel(x_ref, p_ref, prev_ref, wu_ref, cw_ref, cb_ref, wd_ref, g_ref, b_ref, wpg_ref, wp_ref,
                o_ref, st_ref, carry_sc, act_sc, u_sc, *, alpha, seq_rows, n_chunks):
    x = x_ref[...]
    tm, d = x.shape
    xb = x.astype(BF16)
    cols = lambda c: slice(c * FFN_CHUNK, (c + 1) * FFN_CHUNK)
    long_seq = seq_rows is None
    if long_seq:
        first = pl.program_id(1) == 0
    else:
        groups = tm // seq_rows
        sub = lax.broadcasted_iota(jnp.int32, (groups, seq_rows, FFN_CHUNK), 1)

    def conv(c, u):
        cw = cw_ref[:, cols(c)]
        u3 = u.reshape(groups, seq_rows, FFN_CHUNK)
        hist = prev_ref[:, :, c * FFN_CHUNK:(c + 1) * FFN_CHUNK]
        back = lambda k: pltpu.roll(u, k, 0).reshape(groups, seq_rows, FFN_CHUNK)
        u1 = jnp.where(sub == 0, hist[:, 1:2], back(1))
        u2 = jnp.where(sub == 0, hist[:, 0:1], jnp.where(sub == 1, hist[:, 1:2], back(2)))
        st_ref[:, :, c * FFN_CHUNK:(c + 1) * FFN_CHUNK] = u3[:, seq_rows - 2:seq_rows]
        out = cw[0:1] * u2 + cw[1:2] * u1 + cw[2:3] * u3 + cb_ref[:, cols(c)]
        return out.reshape(tm, FFN_CHUNK)

    def up_long(c, slot):
        for half in range(2):
            cc = half * n_chunks + c
            u = _dot(xb, wu_ref[:, cols(cc)])
            u_sc[slot, half, CONV_HIST - (CONV_W - 1):CONV_HIST, :] = carry_sc[cc]
            u_sc[slot, half, CONV_HIST:CONV_HIST + tm, :] = u
            carry_sc[cc] = u[tm - (CONV_W - 1):tm]
            st_ref[0, :, cc * FFN_CHUNK:(cc + 1) * FFN_CHUNK] = u[tm - (CONV_W - 1):tm]

    def act_long(c, slot):
        for r0 in range(0, tm, FFN_ROWS):
            halves = []
            for half in range(2):
                cc = half * n_chunks + c
                cw = cw_ref[:, cols(cc)]
                out = cb_ref[:, cols(cc)]
                for j in range(CONV_W):
                    start = CONV_HIST + r0 - (CONV_W - 1 - j)
                    out = out + cw[j:j + 1] * u_sc[slot, half, start:start + FFN_ROWS, :]
                halves.append(out)
            act_sc[r0:r0 + FFN_ROWS, c * FFN_CHUNK:(c + 1) * FFN_CHUNK] = (
                _gelu_tanh(halves[0]) * halves[1]).astype(BF16)

    if long_seq:
        @pl.when(first)
        def _():
            for cc in range(2 * n_chunks):
                carry_sc[cc] = prev_ref[0, :, cc * FFN_CHUNK:(cc + 1) * FFN_CHUNK]

        up_long(0, 0)
        for c in range(n_chunks):
            if c + 1 < n_chunks:
                up_long(c + 1, (c + 1) % 2)
            act_long(c, c % 2)
    else:
        for c in range(n_chunks):
            c_gate = conv(c, _dot(xb, wu_ref[:, cols(c)]))
            c_val = conv(n_chunks + c, _dot(xb, wu_ref[:, cols(n_chunks + c)]))
            act_sc[:, c * FFN_CHUNK:(c + 1) * FFN_CHUNK] = (_gelu_tanh(c_gate) * c_val).astype(BF16)
    f = _dot(act_sc[...], wd_ref[...])
    y = _layernorm(alpha * x + f, g_ref[...], b_ref[...])
    gate = jax.nn.sigmoid(_dot(y.astype(BF16), wpg_ref[...]))
    o_ref[...] = y + gate * _dot(p_ref[...].astype(BF16), wp_ref[...])


def _ffn(x2, p2, prev, wu, cw, cb, wd, g, b, wpg, wp, *, tm, alpha, batch, seq_rows):
    m, d = x2.shape
    d_up = prev.shape[-1]
    n_chunks = wu.shape[1] // 2 // FFN_CHUNK
    d_ple = p2.shape[1]
    if seq_rows is None:
        tiles = m // batch // tm
        grid = (batch, tiles)
        row = lambda c: pl.BlockSpec((tm, c), lambda bb, i: (bb * tiles + i, 0))
        st_spec = pl.BlockSpec((1, CONV_W - 1, d_up), lambda bb, i: (bb, 0, 0))
        sem = ("parallel", "arbitrary")
    else:
        grid = (1, m // tm)
        row = lambda c: pl.BlockSpec((tm, c), lambda bb, i: (i, 0))
        st_spec = pl.BlockSpec((tm // seq_rows, CONV_W - 1, d_up), lambda bb, i: (i, 0, 0))
        sem = ("arbitrary", "arbitrary")
    n_seq = prev.shape[0]
    return pl.pallas_call(
        functools.partial(_ffn_kernel, alpha=alpha, seq_rows=seq_rows, n_chunks=n_chunks),
        grid=grid,
        in_specs=[row(d), row(d_ple), st_spec, _resident(wu.shape), _resident(cw.shape), _resident(cb.shape),
                  _resident(wd.shape), _resident(g.shape), _resident(b.shape), _resident(wpg.shape),
                  _resident(wp.shape)],
        out_specs=[row(d), st_spec],
        out_shape=[jax.ShapeDtypeStruct((m, d), F32), jax.ShapeDtypeStruct((n_seq, CONV_W - 1, d_up), F32)],
        scratch_shapes=[pltpu.VMEM((2 * n_chunks, CONV_W - 1, FFN_CHUNK), F32),
                        pltpu.VMEM((tm, n_chunks * FFN_CHUNK), BF16),
                        pltpu.VMEM((2, 2, CONV_HIST + tm, FFN_CHUNK), F32)],
        compiler_params=_cparams(sem),
    )(x2, p2, prev, wu, cw, cb, wd, g, b, wpg, wp)


def _layer_weights(l, w):
    return {
        'w_in': w['w_in'][l].astype(BF16),
        'w_gate': w['w_gate'][l].astype(BF16),
        'b_gate': w['b_gate'][l][None, :],
        'w_branch_a': w['w_branch_a'][l].astype(BF16),
        'w_branch_b': w['w_branch_b'][l].astype(BF16),
        'w_out': w['w_out'][l].astype(BF16),
        'ln1_g': w['ln1_g'][l][None, :], 'ln1_b': w['ln1_b'][l][None, :],
        'w_up': w['w_up'][l].astype(BF16),
        'conv_w': w['conv_w'][l],
        'conv_b': w['conv_b'][l][None, :],
        'w_down': w['w_down'][l].astype(BF16),
        'ln2_g': w['ln2_g'][l][None, :], 'ln2_b': w['ln2_b'][l][None, :],
        'w_ple_gate': w['w_ple_gate'][l].astype(BF16),
        'w_ple': w['w_ple'][l].astype(BF16),
        'subln_g': w['diff_subln_g'][l][None, :],
        'lam': (jnp.exp(jnp.sum(w['diff_lq1'][l] * w['diff_lk1'][l]))
                - jnp.exp(jnp.sum(w['diff_lq2'][l] * w['diff_lk2'][l]))
                + (0.8 - 0.6 * math.exp(-0.3 * l))).reshape(1).astype(F32),
    }


def _tail(x2, ya, yb, p2, prev, wl, *, alpha, tm_mix, tm_ffn, batch, seq_rows):
    x1 = _mix(x2, ya, yb, wl['w_gate'], wl['b_gate'], wl['w_branch_a'], wl['w_branch_b'], wl['w_out'],
              wl['ln1_g'], wl['ln1_b'], tm=tm_mix, alpha=alpha)
    return _ffn(x1, p2, prev, wl['w_up'], wl['conv_w'], wl['conv_b'], wl['w_down'], wl['ln2_g'], wl['ln2_b'],
                wl['w_ple_gate'], wl['w_ple'], tm=tm_ffn, alpha=alpha, batch=batch, seq_rows=seq_rows)


def _forward(x_prompt, x_sample, cache_a_k, cache_a_v, cache_b_k, cache_b_v, state_conv, page_table,
             p_prompt, p_sample, w, *, tm_prompt=256, tm_sample=256, blocks_per_step=8):
    depth = w['w_in'].shape[0]
    batch, seq, d = x_prompt.shape
    dec_batch, n_new, _ = x_sample.shape
    h_a, dh = cache_a_k.shape[3], cache_a_k.shape[4]
    h_b, dv = cache_b_v.shape[3], cache_b_v.shape[4]
    assert dh == DH and cache_b_k.shape[4] == 2 * DH and dv == LANES and seq % ATT_TILE == 0
    d_up = state_conv.shape[-1]
    past_len = page_table.shape[1] * cache_a_k.shape[2]
    alpha = (2 * depth) ** 0.25
    tab = w['rel_bias'].T
    bias_prompt = _prompt_bias_tiles(tab)
    bias_last, bias_new = _sample_bias_tiles(tab, h_a, n_new, past_len)
    caches = (cache_a_k, cache_a_v, cache_b_k, cache_b_v)

    xp = x_prompt.reshape(batch * seq, d)
    xs = x_sample.reshape(dec_batch * n_new, d)
    conv0 = jnp.zeros((batch, CONV_W - 1, d_up), F32)
    kv_p, conv_p_all, rows_s = (), [], []
    for l in range(depth):
        wl = _layer_weights(l, w)
        lam_init = 0.8 - 0.6 * math.exp(-0.3 * l)

        hq, *kv_p, vat16, vbt16, kmean = _proj_prompt(xp, wl['w_in'], tuple(kv_p), batch=batch, seq=seq, h_b=h_b)
        ya = _moba_prompt(hq, vat16, kmean.reshape(batch, seq // MOBA_BLOCK, h_a * DH), bias_prompt[:h_a],
                          batch=batch, seq=seq, h_a=h_a)
        yb = _diff_prompt(wl['lam'], hq, vbt16, bias_prompt[h_a:], wl['subln_g'], batch=batch, seq=seq, h_b=h_b,
                          lam_init=lam_init)
        xp, conv_p = _tail(xp, ya, yb, p_prompt[l].reshape(batch * seq, -1), conv0, wl, alpha=alpha,
                           tm_mix=tm_prompt, tm_ffn=tm_prompt, batch=batch, seq_rows=None)
        conv_p_all.append(conv_p)

        hq, ka, va, kb, vb = _proj_sample(xs, wl['w_in'], tm=tm_sample)
        ya, yb = _sample_attn(page_table, wl['lam'], hq, ka, va, kb, vb, caches, l, bias_last, bias_new,
                              wl['subln_g'], n_new=n_new, h_a=h_a, h_b=h_b, lam_init=lam_init,
                              blocks_per_step=blocks_per_step)
        xs, conv_s = _tail(xs, ya, yb, p_sample[l].reshape(dec_batch * n_new, -1), state_conv[l], wl,
                           alpha=alpha, tm_mix=tm_sample, tm_ffn=tm_sample, batch=1, seq_rows=n_new)
        rows_s.append((ka.reshape(dec_batch, n_new, h_a, DH), va.reshape(dec_batch, n_new, h_a, DH),
                       kb.reshape(dec_batch, n_new, h_b, 2 * DH), vb.reshape(dec_batch, n_new, h_b, dv), conv_s))

    kat, vat, kb, vb = kv_p
    token_major = lambda a: a.reshape(depth, batch, h_a, DH, seq).transpose(0, 1, 4, 2, 3)
    stack = lambda rows, i: jnp.stack([r[i] for r in rows])
    return ((xp.reshape(batch, seq, d), xs.reshape(dec_batch, n_new, d), token_major(kat), token_major(vat),
             kb.reshape(depth, batch, seq, h_b, 2 * DH), vb.reshape(depth, batch, seq, h_b, dv),
             jnp.stack(conv_p_all)) + tuple(stack(rows_s, i) for i in range(5)))


def kernel(x_prompt, x_sample, cache_a_k, cache_a_v, cache_b_k, cache_b_v, state_conv, page_table, p_prompt, p_sample, w_in, w_gate, b_gate, w_branch_a, w_branch_b, w_out, rel_bias, diff_lq1, diff_lk1, diff_lq2, diff_lk2, diff_subln_g, ln1_g, ln1_b, w_up, conv_w, conv_b, w_down, ln2_g, ln2_b, w_ple, w_ple_gate):
    w = {'w_in': w_in, 'w_gate': w_gate, 'b_gate': b_gate, 'w_branch_a': w_branch_a,
         'w_branch_b': w_branch_b, 'w_out': w_out, 'rel_bias': rel_bias, 'diff_lq1': diff_lq1,
         'diff_lk1': diff_lk1, 'diff_lq2': diff_lq2, 'diff_lk2': diff_lk2, 'diff_subln_g': diff_subln_g,
         'ln1_g': ln1_g, 'ln1_b': ln1_b, 'w_up': w_up, 'conv_w': conv_w, 'conv_b': conv_b,
         'w_down': w_down, 'ln2_g': ln2_g, 'ln2_b': ln2_b, 'w_ple': w_ple, 'w_ple_gate': w_ple_gate}
    return _forward(x_prompt, x_sample, cache_a_k, cache_a_v, cache_b_k, cache_b_v, state_conv, page_table,
                    p_prompt, p_sample, w)
```
